```python
import math
import jax, jax.numpy as jnp
from jax import lax
import numpy as np

D_MODEL = 2048
BATCH = 4
SEQ = 8192
DEPTH = 1
DEC_BATCH = 1
DEC_SEQ = 16384
PAST_LEN = 128

GRID_W = 64
Q_BLOCK = 128
ROPE_THETA = 10000.0
A_HEADS = 16
A_Q_RANK = 512
A_KV_RANK = 512
A_NOPE = 128
A_ROPE = 64
A_V = 128
B_HEADS = 16
B_KV_HEADS = 4
B_GROUP = B_HEADS // B_KV_HEADS
B_HEAD_DIM = 128
N_GROUPS = 8
EXPERTS_PER_GROUP = 8
N_EXPERTS = N_GROUPS * EXPERTS_PER_GROUP
TOP_K = 2
D_EXPERT = 512
EXPERT_BLOCK = 128
RMS_EPS = 1e-6
LN_EPS = 1e-5
DEEPNORM_ALPHA = (2 * DEPTH) ** 0.25
DEEPNORM_BETA = (8 * DEPTH) ** -0.25
IN_SPLITS = (A_Q_RANK, A_KV_RANK, A_ROPE, B_HEADS * B_HEAD_DIM, B_KV_HEADS * B_HEAD_DIM,
             B_KV_HEADS * B_HEAD_DIM, D_MODEL, D_MODEL)
D_IN = sum(IN_SPLITS)
IN_OFFSETS = tuple(np.cumsum(IN_SPLITS)[:-1].tolist())

kernel_name = 'hybrid_mla_axialgqa_hiermoe_encoder'


def _rms_norm(x, g):
    xf = x.astype(jnp.float32)
    y = xf * lax.rsqrt(jnp.mean(xf * xf, axis=-1, keepdims=True) + RMS_EPS)
    return (y * g.astype(jnp.float32)).astype(x.dtype)


def _layer_norm(x, g, b):
    xf = x.astype(jnp.float32)
    mu = jnp.mean(xf, axis=-1, keepdims=True)
    xc = xf - mu
    var = jnp.mean(xc * xc, axis=-1, keepdims=True)
    y = xc * lax.rsqrt(var + LN_EPS) * g.astype(jnp.float32) + b.astype(jnp.float32)
    return y.astype(x.dtype)


def _rope_angles(pos, dim):
    inv = ROPE_THETA ** (-jnp.arange(0, dim, 2, dtype=jnp.float32) / dim)
    return pos.astype(jnp.float32)[:, None] * inv[None, :]


def _apply_rope(x, ang):
    d2 = x.shape[-1] // 2
    shape = (1, ang.shape[0]) + (1,) * (x.ndim - 3) + (d2,)
    cos = jnp.cos(ang).reshape(shape)
    sin = jnp.sin(ang).reshape(shape)
    xf = x.astype(jnp.float32)
    x1, x2 = xf[..., :d2], xf[..., d2:]
    return jnp.concatenate([x1 * cos - x2 * sin, x2 * cos + x1 * sin], axis=-1).astype(x.dtype)


def _axial_rope(x, ang_row, ang_col):
    h = x.shape[-1] // 2
    return jnp.concatenate([_apply_rope(x[..., :h], ang_row), _apply_rope(x[..., h:], ang_col)], axis=-1)


def _to_blocks(x):
    b, s = x.shape[:2]
    return jnp.moveaxis(x.reshape((b, s // Q_BLOCK, Q_BLOCK) + x.shape[2:]), 1, 0)


def _from_blocks(o):
    nb, b, q = o.shape[:3]
    return jnp.moveaxis(o, 0, 1).reshape((b, nb * q) + o.shape[3:])


def _mla_branch(c_q, c_kv, k_rope_raw, ang, q_norm_g, kv_norm_g, w_uq, w_ukv, w_o):
    b, s, _ = c_q.shape
    q = (_rms_norm(c_q, q_norm_g) @ w_uq).reshape(b, s, A_HEADS, A_NOPE + A_ROPE)
    q_nope = q[..., :A_NOPE]
    q_rope = _apply_rope(q[..., A_NOPE:], ang)
    kv = (_rms_norm(c_kv, kv_norm_g) @ w_ukv).reshape(b, s, A_HEADS, A_NOPE + A_V)
    k_nope, v = kv[..., :A_NOPE], kv[..., A_NOPE:]
    k_rope = _apply_rope(k_rope_raw, ang)
    scale = (A_NOPE + A_ROPE) ** -0.5

    def attend(blk):
        qn, qr = blk
        sc = (jnp.einsum('bqhd,bkhd->bhqk', qn, k_nope, preferred_element_type=jnp.float32)
              + jnp.einsum('bqhd,bkd->bhqk', qr, k_rope, preferred_element_type=jnp.float32))
        p = jax.nn.softmax(sc * scale, axis=-1).astype(v.dtype)
        return jnp.einsum('bhqk,bkhd->bqhd', p, v)

    o = _from_blocks(lax.map(attend, (_to_blocks(q_nope), _to_blocks(q_rope))))
    return o.reshape(b, s, A_HEADS * A_V) @ w_o


def _gqa_branch(q, k, v, ang_row, ang_col, q_norm_g, k_norm_g, w_o):
    b, s, _ = q.shape
    q = q.reshape(b, s, B_KV_HEADS, B_GROUP, B_HEAD_DIM)
    k = k.reshape(b, s, B_KV_HEADS, B_HEAD_DIM)
    v = v.reshape(b, s, B_KV_HEADS, B_HEAD_DIM)
    q = _axial_rope(_rms_norm(q, q_norm_g), ang_row, ang_col)
    k = _axial_rope(_rms_norm(k, k_norm_g), ang_row, ang_col)
    scale = B_HEAD_DIM ** -0.5

    def attend(qb):
        sc = jnp.einsum('bqngd,bknd->bngqk', qb, k, preferred_element_type=jnp.float32)
        p = jax.nn.softmax(sc * scale, axis=-1).astype(v.dtype)
        return jnp.einsum('bngqk,bknd->bqngd', p, v)

    o = _from_blocks(lax.map(attend, _to_blocks(q)))
    return o.reshape(b, s, B_HEADS * B_HEAD_DIM) @ w_o


def _grouped_experts(hf, eid, wt, w_gate, w_up, w_down):
    n, d = hf.shape
    a = eid.shape[0]
    tok = jnp.arange(a, dtype=jnp.int32) // TOP_K
    order = jnp.argsort(eid)
    e_sorted = eid[order]
    counts = jnp.bincount(eid, length=N_EXPERTS)
    starts = jnp.cumsum(counts) - counts
    padded = ((counts + EXPERT_BLOCK - 1) // EXPERT_BLOCK) * EXPERT_BLOCK
    pad_end = jnp.cumsum(padded)
    pad_start = pad_end - padded
    dest = pad_start[e_sorted] + (jnp.arange(a, dtype=jnp.int32) - starts[e_sorted])
    p_rows = ((a + EXPERT_BLOCK - 1) // EXPERT_BLOCK) * EXPERT_BLOCK + N_EXPERTS * EXPERT_BLOCK
    n_blk = p_rows // EXPERT_BLOCK
    buf_tok = jnp.zeros((p_rows,), jnp.int32).at[dest].set(tok[order])
    buf_w = jnp.zeros((p_rows,), hf.dtype).at[dest].set(wt[order].astype(hf.dtype))
    blk_start = jnp.arange(n_blk, dtype=jnp.int32) * EXPERT_BLOCK
    blk_e = jnp.minimum(jnp.searchsorted(pad_end, blk_start, side='right'), N_EXPERTS - 1)
    xs = hf[buf_tok].reshape(n_blk, EXPERT_BLOCK, d)

    def run(args):
        xb, e = args
        return (jax.nn.silu(xb @ w_gate[e]) * (xb @ w_up[e])) @ w_down[e]

    ys = lax.map(run, (xs, blk_e)).reshape(p_rows, d)
    return jnp.zeros_like(hf).at[buf_tok].add(ys * buf_w[:, None])


def _hier_moe(h, w_group, b_group, w_expert, b_expert, w_gate, w_up, w_down):
    b, s, d = h.shape
    n = b * s
    hf = h.reshape(n, d)
    g_logits = (hf @ w_group).astype(jnp.float32) + b_group.astype(jnp.float32)
    g_prob = jax.nn.softmax(g_logits, axis=-1)
    grp = jnp.argmax(g_logits, axis=-1).astype(jnp.int32)
    p_grp = jnp.take_along_axis(g_prob, grp[:, None], axis=-1)
    e_logits = ((hf @ w_expert).astype(jnp.float32) + b_expert.astype(jnp.float32)).reshape(
        n, N_GROUPS, EXPERTS_PER_GROUP)
    e_in = jnp.take_along_axis(e_logits, grp[:, None, None], axis=1)[:, 0]
    top_v, top_i = lax.top_k(e_in, TOP_K)
    wts = jax.nn.softmax(top_v, axis=-1) * p_grp
    eid = grp[:, None] * EXPERTS_PER_GROUP + top_i.astype(jnp.int32)
    y = _grouped_experts(hf, eid.reshape(-1), wts.reshape(-1), w_gate, w_up, w_down)
    return y.reshape(b, s, d)


def _layer(x, c, ang_a, ang_row, ang_col, w_ada, b_ada, w_in, a_q_norm, a_kv_norm, a_w_uq, a_w_ukv,
           a_w_o, b_q_norm, b_k_norm, b_w_o, w_out, ln1_g, ln1_b, w_group, b_group, w_expert,
           b_expert, e_w_gate, e_w_up, e_w_down, ln2_g, ln2_b):
    mod = (jax.nn.silu(c) @ w_ada + b_ada)[:, None, :]
    sh1, sc1, g1, sh2, sc2, g2 = jnp.split(mod, 6, axis=-1)
    h = x * (1 + sc1) + sh1
    c_q, c_kv, k_rope, q_b, k_b, v_b, gl_a, gl_b = jnp.split(h @ w_in, IN_OFFSETS, axis=-1)
    y_a = _mla_branch(c_q, c_kv, k_rope, ang_a, a_q_norm, a_kv_norm, a_w_uq, a_w_ukv, a_w_o)
    y_b = _gqa_branch(q_b, k_b, v_b, ang_row, ang_col, b_q_norm, b_k_norm, b_w_o)
    attn = (jax.nn.sigmoid(gl_a) * y_a + jax.nn.sigmoid(gl_b) * y_b) @ w_out
    x = _layer_norm(DEEPNORM_ALPHA * x + g1 * attn, ln1_g, ln1_b)
    h2 = x * (1 + sc2) + sh2
    ffn = _hier_moe(h2, w_group, b_group, w_expert, b_expert, e_w_gate, e_w_up, e_w_down)
    return _layer_norm(DEEPNORM_ALPHA * x + g2 * ffn, ln2_g, ln2_b)


def _trunk(x, c, weights):
    s = x.shape[1]
    rows = s // GRID_W
    t = jnp.arange(s, dtype=jnp.int32)
    row = jnp.repeat(jnp.arange(rows, dtype=jnp.int32), GRID_W)
    col = jnp.tile(jnp.arange(GRID_W, dtype=jnp.int32), rows)
    ang_a = _rope_angles(t, A_ROPE)
    ang_row = _rope_angles(row, B_HEAD_DIM // 2)
    ang_col = _rope_angles(col, B_HEAD_DIM // 2)
    for l in range(DEPTH):
        x = _layer(x, c, ang_a, ang_row, ang_col, *[w[l] for w in weights])
    return x


def setup_inputs(seed: int = 0) -> dict:
    key = jax.random.key(seed)
    ks = jax.random.split(key, 32)
    f32 = jnp.float32
    L = DEPTH
    D = D_MODEL

    def nrm(k, shape, scale):
        return jax.random.normal(k, shape, f32) * scale

    return {
        'x_prompt': nrm(ks[0], (BATCH, SEQ, D), 1.0),
        'x_sample': nrm(ks[1], (DEC_BATCH, DEC_SEQ, D), 1.0),
        'c_prompt': nrm(ks[2], (BATCH, D), 1.0),
        'c_sample': nrm(ks[3], (DEC_BATCH, D), 1.0),
        'w_ada': nrm(ks[4], (L, D, 6 * D), 0.5 * D ** -0.5),
        'b_ada': nrm(ks[5], (L, 6 * D), 0.02),
        'w_in': nrm(ks[6], (L, D, D_IN), D ** -0.5),
        'a_q_norm': 1.0 + nrm(ks[7], (L, A_Q_RANK), 0.02),
        'a_kv_norm': 1.0 + nrm(ks[8], (L, A_KV_RANK), 0.02),
        'a_w_uq': nrm(ks[9], (L, A_Q_RANK, A_HEADS * (A_NOPE + A_ROPE)), A_Q_RANK ** -0.5),
        'a_w_ukv': nrm(ks[10], (L, A_KV_RANK, A_HEADS * (A_NOPE + A_V)), A_KV_RANK ** -0.5),
        'a_w_o': nrm(ks[11], (L, A_HEADS * A_V, D), (A_HEADS * A_V) ** -0.5),
        'b_q_norm': 1.0 + nrm(ks[12], (L, B_HEAD_DIM), 0.02),
        'b_k_norm': 1.0 + nrm(ks[13], (L, B_HEAD_DIM), 0.02),
        'b_w_o': nrm(ks[14], (L, B_HEADS * B_HEAD_DIM, D), (B_HEADS * B_HEAD_DIM) ** -0.5),
        'w_out': nrm(ks[15], (L, D, D), DEEPNORM_BETA * D ** -0.5),
        'ln1_g': 1.0 + nrm(ks[16], (L, D), 0.02),
        'ln1_b': nrm(ks[17], (L, D), 0.02),
        'w_group': nrm(ks[18], (L, D, N_GROUPS), D ** -0.5),
        'b_group': nrm(ks[19], (L, N_GROUPS), 0.01),
        'w_expert': nrm(ks[20], (L, D, N_EXPERTS), D ** -0.5),
        'b_expert': nrm(ks[21], (L, N_EXPERTS), 0.01),
        'e_w_gate': nrm(ks[22], (L, N_EXPERTS, D, D_EXPERT), D ** -0.5),
        'e_w_up': nrm(ks[23], (L, N_EXPERTS, D, D_EXPERT), D ** -0.5),
        'e_w_down': nrm(ks[24], (L, N_EXPERTS, D_EXPERT, D), DEEPNORM_BETA * D_EXPERT ** -0.5),
        'ln2_g': 1.0 + nrm(ks[25], (L, D), 0.02),
        'ln2_b': nrm(ks[26], (L, D), 0.02),
    }


def reference(x_prompt, x_sample, c_prompt, c_sample, w_ada, b_ada, w_in, a_q_norm, a_kv_norm,
              a_w_uq, a_w_ukv, a_w_o, b_q_norm, b_k_norm, b_w_o, w_out, ln1_g, ln1_b, w_group,
              b_group, w_expert, b_expert, e_w_gate, e_w_up, e_w_down, ln2_g, ln2_b):
    weights = (w_ada, b_ada, w_in, a_q_norm, a_kv_norm, a_w_uq, a_w_ukv, a_w_o, b_q_norm, b_k_norm,
               b_w_o, w_out, ln1_g, ln1_b, w_group, b_group, w_expert, b_expert, e_w_gate, e_w_up,
               e_w_down, ln2_g, ln2_b)
    y_prompt = _trunk(x_prompt, c_prompt, weights)
    y_sample = _trunk(x_sample, c_sample, weights)
    return (y_prompt, y_sample)
```

```python
import functools
import math

import numpy as np
import jax
import jax.numpy as jnp
from jax import lax
from jax.experimental import pallas as pl
from jax.experimental.pallas import tpu as pltpu

F32 = jnp.float32
BF16 = jnp.bfloat16
HIGHEST = lax.Precision.HIGHEST

GRID_W = 64
ROPE_THETA = 10000.0
A_HEADS = 16
A_Q_RANK = 512
A_KV_RANK = 512
A_NOPE = 128
A_ROPE = 64
A_V = 128
A_QK_PAD = 256
B_HEADS = 16
B_KV_HEADS = 4
B_GROUP = B_HEADS // B_KV_HEADS
B_HEAD_DIM = 128
N_GROUPS = 8
EXPERTS_PER_GROUP = 8
N_EXPERTS = N_GROUPS * EXPERTS_PER_GROUP
TOP_K = 2
RMS_EPS = 1e-6
LN_EPS = 1e-5
DEPTH = 1
DEEPNORM_ALPHA = (2 * DEPTH) ** 0.25
LOG2E = 1.4426950408889634

LANES = 128
EXPERT_ROWS = 256
VMEM_LIMIT = 48 * 1024 * 1024


def _params(*sem):
    return pltpu.CompilerParams(dimension_semantics=sem, vmem_limit_bytes=VMEM_LIMIT)


def _tile(n, pref):
    t = min(n, pref)
    while n % t:
        t //= 2
    return t


def _ada_kernel(c_ref, w_ref, b_ref, o_ref):
    c = c_ref[...]
    a = c * jax.nn.sigmoid(c)
    o_ref[...] = jnp.dot(a, w_ref[...], precision=HIGHEST, preferred_element_type=F32) + b_ref[...]


def _ada(c_pad, w_ada, b_ada):
    d, n = w_ada.shape
    tn = _tile(n, 1024)
    return pl.pallas_call(
        _ada_kernel,
        grid=(n // tn,),
        in_specs=[pl.BlockSpec(c_pad.shape, lambda j: (0, 0)),
                  pl.BlockSpec((d, tn), lambda j: (0, j)),
                  pl.BlockSpec((1, tn), lambda j: (0, j))],
        out_specs=pl.BlockSpec((c_pad.shape[0], tn), lambda j: (0, j)),
        out_shape=jax.ShapeDtypeStruct((c_pad.shape[0], n), F32),
        compiler_params=_params("arbitrary"),
        name="ada",
    )(c_pad, w_ada, b_ada.reshape(1, n))


def _mod_proj_kernel(seg_ref, x_ref, mod_ref, w_ref, h_ref, o_ref):
    del seg_ref
    sh = mod_ref[0, 0:1, :]
    sc = mod_ref[0, 1:2, :]
    h = (x_ref[...] * (1.0 + sc) + sh).astype(BF16)
    h_ref[...] = h
    o_ref[...] = jnp.dot(h, w_ref[...], preferred_element_type=F32)


def _mod_proj(x, mod, tile_seg, w, tm):
    n, d = x.shape
    nc = w.shape[1]
    return pl.pallas_call(
        _mod_proj_kernel,
        grid_spec=pltpu.PrefetchScalarGridSpec(
            num_scalar_prefetch=1,
            grid=(n // tm,),
            in_specs=[pl.BlockSpec((tm, d), lambda i, seg: (i, 0)),
                      pl.BlockSpec((1, 8, d), lambda i, seg: (seg[i], 0, 0)),
                      pl.BlockSpec((d, nc), lambda i, seg: (0, 0))],
            out_specs=[pl.BlockSpec((tm, d), lambda i, seg: (i, 0)),
                       pl.BlockSpec((tm, nc), lambda i, seg: (i, 0))]),
        out_shape=[jax.ShapeDtypeStruct((n, d), BF16), jax.ShapeDtypeStruct((n, nc), F32)],
        compiler_params=_params("arbitrary"),
        name="mod_proj",
    )(tile_seg, x, mod, w)


def _rope(x, cos, sin_a, sin_b):
    return x * cos + pltpu.roll(x, 96, 1) * sin_a + pltpu.roll(x, 32, 1) * sin_b


def _proj_plain_kernel(h_ref, w_ref, o_ref):
    o_ref[...] = jnp.dot(h_ref[...], w_ref[...], preferred_element_type=F32).astype(o_ref.dtype)


def _proj_gate_kernel(h_ref, w_ref, o_ref):
    o_ref[...] = jax.nn.sigmoid(jnp.dot(h_ref[...], w_ref[...], preferred_element_type=F32))


def _proj_qk_kernel(pos_ref, h_ref, w_ref, g_ref, cos_ref, sa_ref, sb_ref, o_ref):
    del pos_ref
    acc = jnp.dot(h_ref[...], w_ref[...], preferred_element_type=F32)
    g = g_ref[0]
    cos, sa, sb = cos_ref[...], sa_ref[...], sb_ref[...]
    for hh in range(acc.shape[1] // B_HEAD_DIM):
        x = acc[:, hh * B_HEAD_DIM:(hh + 1) * B_HEAD_DIM]
        r = lax.rsqrt(jnp.mean(x * x, axis=-1, keepdims=True) + RMS_EPS)
        y = x * r * g
        o_ref[:, hh * B_HEAD_DIM:(hh + 1) * B_HEAD_DIM] = _rope(y, cos, sa, sb).astype(o_ref.dtype)


def _proj(h, w, tm, tn, kind, out_dtype):
    n, d = h.shape
    nc = w.shape[1]
    kern = _proj_gate_kernel if kind == "gate" else _proj_plain_kernel
    return pl.pallas_call(
        kern,
        grid=(n // tm, nc // tn),
        in_specs=[pl.BlockSpec((tm, d), lambda i, j: (i, 0)),
                  pl.BlockSpec((d, tn), lambda i, j: (0, j))],
        out_specs=pl.BlockSpec((tm, tn), lambda i, j: (i, j)),
        out_shape=jax.ShapeDtypeStruct((n, nc), out_dtype),
        compiler_params=_params("arbitrary", "arbitrary"),
        name="proj_" + kind,
    )(h, w)


def _proj_qk(h, w, gains, tabs, tile_pos, tm, n_q_tiles):
    n, d = h.shape
    nc = w.shape[1]
    tn = B_GROUP * B_HEAD_DIM
    tab_spec = pl.BlockSpec((tm, LANES), lambda i, j, pos: (pos[i], 0))
    return pl.pallas_call(
        _proj_qk_kernel,
        grid_spec=pltpu.PrefetchScalarGridSpec(
            num_scalar_prefetch=1,
            grid=(n // tm, nc // tn),
            in_specs=[pl.BlockSpec((tm, d), lambda i, j, pos: (i, 0)),
                      pl.BlockSpec((d, tn), lambda i, j, pos: (0, j)),
                      pl.BlockSpec((1, 1, LANES), lambda i, j, pos: (jnp.where(j < n_q_tiles, 0, 1), 0, 0)),
                      tab_spec, tab_spec, tab_spec],
            out_specs=pl.BlockSpec((tm, tn), lambda i, j, pos: (i, j))),
        out_shape=jax.ShapeDtypeStruct((n, nc), BF16),
        compiler_params=_params("arbitrary", "arbitrary"),
        name="proj_qk",
    )(tile_pos, h, w, gains, *tabs)


def _rms(x, g):
    return x * lax.rsqrt(jnp.mean(x * x, axis=-1, keepdims=True) + RMS_EPS) * g


def _mla_q_kernel(pos_ref, c_ref, g_ref, w_ref, cos_ref, sa_ref, sb_ref, o_ref):
    del pos_ref
    xn = _rms(c_ref[...], g_ref[...]).astype(BF16)
    acc = jnp.dot(xn, w_ref[...], preferred_element_type=F32)
    cos, sa, sb = cos_ref[...], sa_ref[...], sb_ref[...]
    for hh in range(acc.shape[1] // A_QK_PAD):
        base = hh * A_QK_PAD
        o_ref[:, base:base + A_NOPE] = acc[:, base:base + A_NOPE].astype(o_ref.dtype)
        o_ref[:, base + A_NOPE:base + A_QK_PAD] = _rope(
            acc[:, base + A_NOPE:base + A_QK_PAD], cos, sa, sb).astype(o_ref.dtype)


def _mla_q(cqkv, gain, w, tabs, tile_pos, tm):
    n = cqkv.shape[0]
    nc = w.shape[1]
    tn = 4 * A_QK_PAD
    tab_spec = pl.BlockSpec((tm, LANES), lambda i, j, pos: (pos[i], 0))
    return pl.pallas_call(
        _mla_q_kernel,
        grid_spec=pltpu.PrefetchScalarGridSpec(
            num_scalar_prefetch=1,
            grid=(n // tm, nc // tn),
            in_specs=[pl.BlockSpec((tm, A_Q_RANK), lambda i, j, pos: (i, 0)),
                      pl.BlockSpec((1, A_Q_RANK), lambda i, j, pos: (0, 0)),
                      pl.BlockSpec((A_Q_RANK, tn), lambda i, j, pos: (0, j)),
                      tab_spec, tab_spec, tab_spec],
            out_specs=pl.BlockSpec((tm, tn), lambda i, j, pos: (i, j))),
        out_shape=jax.ShapeDtypeStruct((n, nc), BF16),
        compiler_params=_params("arbitrary", "arbitrary"),
        name="mla_q",
    )(tile_pos, cqkv, gain, w, *tabs)


def _mla_kv_kernel(pos_ref, c_ref, kr_ref, g_ref, wk_ref, wv_ref, cos_ref, sa_ref, sb_ref, k_ref, v_ref):
    del pos_ref
    xn = _rms(c_ref[...], g_ref[...]).astype(BF16)
    kn = jnp.dot(xn, wk_ref[...], preferred_element_type=F32)
    v_ref[...] = jnp.dot(xn, wv_ref[...], preferred_element_type=F32).astype(v_ref.dtype)
    kr = _rope(kr_ref[...], cos_ref[...], sa_ref[...], sb_ref[...]).astype(k_ref.dtype)
    for hh in range(kn.shape[1] // A_NOPE):
        k_ref[:, hh * A_QK_PAD:hh * A_QK_PAD + A_NOPE] = kn[:, hh * A_NOPE:(hh + 1) * A_NOPE].astype(k_ref.dtype)
        k_ref[:, hh * A_QK_PAD + A_NOPE:(hh + 1) * A_QK_PAD] = kr


def _mla_kv(cqkv, gain, wk, wv, tabs, tile_pos, tm):
    n = cqkv.shape[0]
    hpt = 4
    n_ct = A_HEADS // hpt
    kr_blk = (A_Q_RANK + A_KV_RANK) // LANES
    tab_spec = pl.BlockSpec((tm, LANES), lambda i, j, pos: (pos[i], 0))
    return pl.pallas_call(
        _mla_kv_kernel,
        grid_spec=pltpu.PrefetchScalarGridSpec(
            num_scalar_prefetch=1,
            grid=(n // tm, n_ct),
            in_specs=[pl.BlockSpec((tm, A_KV_RANK), lambda i, j, pos: (i, 1)),
                      pl.BlockSpec((tm, LANES), lambda i, j, pos: (i, kr_blk)),
                      pl.BlockSpec((1, A_KV_RANK), lambda i, j, pos: (0, 0)),
                      pl.BlockSpec((A_KV_RANK, hpt * A_NOPE), lambda i, j, pos: (0, j)),
                      pl.BlockSpec((A_KV_RANK, hpt * A_V), lambda i, j, pos: (0, j)),
                      tab_spec, tab_spec, tab_spec],
            out_specs=[pl.BlockSpec((tm, hpt * A_QK_PAD), lambda i, j, pos: (i, j)),
                       pl.BlockSpec((tm, hpt * A_V), lambda i, j, pos: (i, j))]),
        out_shape=[jax.ShapeDtypeStruct((n, A_HEADS * A_QK_PAD), BF16),
                   jax.ShapeDtypeStruct((n, A_HEADS * A_V), BF16)],
        compiler_params=_params("arbitrary", "arbitrary"),
        name="mla_kv",
    )(tile_pos, cqkv, cqkv, gain, wk, wv, *tabs)


def _flash_kernel(q_ref, k_ref, v_ref, o_ref, *, group, dq, dv, tk, c):
    n_kv = k_ref.shape[0] // tk
    tq = q_ref.shape[0]
    for g in range(group):
        q = q_ref[:, g * dq:(g + 1) * dq]

        def body(j, carry, q=q):
            m, l, acc = carry
            start = pl.multiple_of(j * tk, tk)
            k = k_ref[pl.ds(start, tk), :]
            v = v_ref[pl.ds(start, tk), :]
            s = lax.dot_general(q, k, (((1,), (1,)), ((), ())), preferred_element_type=F32)
            m_new = jnp.maximum(m, jnp.max(s, axis=-1, keepdims=True))
            alpha = jnp.exp2((m - m_new) * c)
            p = jnp.exp2((s - m_new) * c)
            l = alpha * l + jnp.sum(p, axis=-1, keepdims=True)
            acc = alpha * acc + jnp.dot(p.astype(BF16), v, preferred_element_type=F32)
            return m_new, l, acc

        init = (jnp.full((tq, 1), -1e30, F32), jnp.zeros((tq, 1), F32), jnp.zeros((tq, dv), F32))
        _, l, acc = lax.fori_loop(0, n_kv, body, init)
        o_ref[:, g * dv:(g + 1) * dv] = (acc / l).astype(o_ref.dtype)


def _flash(q2d, k2d, v2d, *, batch, seq, row0, n_kv_heads, group, dq, dv, q_col0, k_col0, v_col0, scale, out_rows):
    assert row0 % seq == 0 and q_col0 % (group * dq) == 0 and k_col0 % dq == 0 and v_col0 % dv == 0
    tq = _tile(seq, 256)
    tk = _tile(seq, 512)
    qb0, kb0 = row0 // tq, row0 // seq
    nq = seq // tq
    qc0, kc0, vc0 = q_col0 // (group * dq), k_col0 // dq, v_col0 // dv
    kern = functools.partial(_flash_kernel, group=group, dq=dq, dv=dv, tk=tk, c=scale * LOG2E)
    return pl.pallas_call(
        kern,
        grid=(batch, n_kv_heads, nq),
        in_specs=[pl.BlockSpec((tq, group * dq), lambda b, h, i: (qb0 + b * nq + i, qc0 + h)),
                  pl.BlockSpec((seq, dq), lambda b, h, i: (kb0 + b, kc0 + h)),
                  pl.BlockSpec((seq, dv), lambda b, h, i: (kb0 + b, vc0 + h))],
        out_specs=pl.BlockSpec((tq, group * dv), lambda b, h, i: (b * nq + i, h)),
        out_shape=jax.ShapeDtypeStruct((out_rows, n_kv_heads * group * dv), BF16),
        compiler_params=_params("arbitrary", "arbitrary", "arbitrary"),
        name="flash",
    )(q2d, k2d, v2d)


def _gate_mix_kernel(oa_ref, ob_ref, wa_ref, wb_ref, ga_ref, gb_ref, o_ref):
    ya = jnp.dot(oa_ref[...], wa_ref[...], preferred_element_type=F32)
    yb = jnp.dot(ob_ref[...], wb_ref[...], preferred_element_type=F32)
    o_ref[...] = (ga_ref[...] * ya + gb_ref[...] * yb).astype(o_ref.dtype)


def _gate_mix(oa, ob, wa, wb, gates, tm, tn):
    n, d = oa.shape
    dm = wa.shape[1]
    nct = dm // tn
    return pl.pallas_call(
        _gate_mix_kernel,
        grid=(n // tm, nct),
        in_specs=[pl.BlockSpec((tm, d), lambda i, j: (i, 0)),
                  pl.BlockSpec((tm, d), lambda i, j: (i, 0)),
                  pl.BlockSpec((d, tn), lambda i, j: (0, j)),
                  pl.BlockSpec((d, tn), lambda i, j: (0, j)),
                  pl.BlockSpec((tm, tn), lambda i, j: (i, j)),
                  pl.BlockSpec((tm, tn), lambda i, j: (i, nct + j))],
        out_specs=pl.BlockSpec((tm, tn), lambda i, j: (i, j)),
        out_shape=jax.ShapeDtypeStruct((n, dm), BF16),
        compiler_params=_params("arbitrary", "arbitrary"),
        name="gate_mix",
    )(oa, ob, wa, wb, gates, gates)


def _layer_norm(y, g, b):
    mu = jnp.mean(y, axis=-1, keepdims=True)
    yc = y - mu
    var = jnp.mean(yc * yc, axis=-1, keepdims=True)
    return yc * lax.rsqrt(var + LN_EPS) * g + b


def _route(logits):
    lane = lax.broadcasted_iota(jnp.int32, logits.shape, 1).astype(F32)
    big = float(LANES)
    ninf = -jnp.inf
    gmask = lane < N_GROUPS
    gl = jnp.where(gmask, logits, ninf)
    gmax = jnp.max(gl, axis=-1, keepdims=True)
    grp = jnp.min(jnp.where(gl == gmax, lane, big), axis=-1, keepdims=True)
    den = jnp.sum(jnp.exp(gl - gmax), axis=-1, keepdims=True)
    p_grp = 1.0 / den
    lo = N_GROUPS + grp * EXPERTS_PER_GROUP
    el = jnp.where(lane >= lo, jnp.where(lane < lo + EXPERTS_PER_GROUP, logits, ninf), ninf)
    t1 = jnp.max(el, axis=-1, keepdims=True)
    i1 = jnp.min(jnp.where(el == t1, lane, big), axis=-1, keepdims=True)
    el2 = jnp.where(lane == i1, ninf, el)
    t2 = jnp.max(el2, axis=-1, keepdims=True)
    i2 = jnp.min(jnp.where(el2 == t2, lane, big), axis=-1, keepdims=True)
    e = jnp.exp(t2 - t1)
    w1 = p_grp / (1.0 + e)
    w2 = p_grp * e / (1.0 + e)
    out = jnp.where(lane == 0, w1, jnp.where(lane == 1, w2, jnp.where(lane == 2, i1 - N_GROUPS, i2 - N_GROUPS)))
    return jnp.where(lane < 4, out, 0.0)


def _ln1_kernel(seg_ref, g_ref, x_ref, mod_ref, w_ref, lg_ref, lb_ref, wr_ref, br_ref, x1_ref, h2_ref, rt_ref):
    del seg_ref
    attn = jnp.dot(g_ref[...], w_ref[...], preferred_element_type=F32)
    g1 = mod_ref[0, 2:3, :]
    sh2 = mod_ref[0, 3:4, :]
    sc2 = mod_ref[0, 4:5, :]
    x1 = _layer_norm(DEEPNORM_ALPHA * x_ref[...] + g1 * attn, lg_ref[...], lb_ref[...])
    x1_ref[...] = x1
    h2 = x1 * (1.0 + sc2) + sh2
    h2_ref[...] = h2
    logits = jnp.dot(h2, wr_ref[...], precision=HIGHEST, preferred_element_type=F32) + br_ref[...]
    rt_ref[...] = _route(logits)


def _ln1(gated, x, mod, tile_seg, w_out, lg, lb, wr, br, tm):
    n, d = x.shape
    row = lambda i, seg: (i, 0)
    const = lambda i, seg: (0, 0)
    return pl.pallas_call(
        _ln1_kernel,
        grid_spec=pltpu.PrefetchScalarGridSpec(
            num_scalar_prefetch=1,
            grid=(n // tm,),
            in_specs=[pl.BlockSpec((tm, d), row),
                      pl.BlockSpec((tm, d), row),
                      pl.BlockSpec((1, 8, d), lambda i, seg: (seg[i], 0, 0)),
                      pl.BlockSpec((d, d), const),
                      pl.BlockSpec((1, d), const),
                      pl.BlockSpec((1, d), const),
                      pl.BlockSpec((d, LANES), const),
                      pl.BlockSpec((1, LANES), const)],
            out_specs=[pl.BlockSpec((tm, d), row), pl.BlockSpec((tm, d), row), pl.BlockSpec((tm, LANES), row)]),
        out_shape=[jax.ShapeDtypeStruct((n, d), F32), jax.ShapeDtypeStruct((n, d), F32),
                   jax.ShapeDtypeStruct((n, LANES), F32)],
        compiler_params=_params("arbitrary"),
        name="ln1_route",
    )(tile_seg, gated, x, mod, w_out, lg, lb, wr, br)


def _row_gather_start(src_hbm, dst, sem, idx, row):
    pltpu.make_async_copy(src_hbm.at[pl.ds(idx, 1), :], dst.at[pl.ds(row, 1), :], sem).start()


def _row_gather_wait(src_hbm, dst, sem, row):
    pltpu.make_async_copy(src_hbm.at[pl.ds(0, 1), :], dst.at[pl.ds(row, 1), :], sem).wait()


def _expert_kernel(blk_e_ref, n_used_ref, tok_ref, h2_hbm, wg_ref, wu_ref, wd_ref, o_ref,
                   xbuf, wg_bf, wu_bf, wd_bf, sem):
    i = pl.program_id(0)
    e = blk_e_ref[i]
    prev = blk_e_ref[jnp.maximum(i - 1, 0)]
    rows = xbuf.shape[0]

    @pl.when(jnp.logical_or(i == 0, e != prev))
    def _():
        wg_bf[...] = wg_ref[0].astype(BF16)
        wu_bf[...] = wu_ref[0].astype(BF16)
        wd_bf[...] = wd_ref[0].astype(BF16)

    @pl.when(i < n_used_ref[0])
    def _():
        def start(r, carry):
            _row_gather_start(h2_hbm, xbuf, sem.at[0], tok_ref[0, 0, r], r)
            return carry

        def wait(r, carry):
            _row_gather_wait(h2_hbm, xbuf, sem.at[0], r)
            return carry

        lax.fori_loop(0, rows, start, 0)
        lax.fori_loop(0, rows, wait, 0)
        x = xbuf[...].astype(BF16)
        gate = jnp.dot(x, wg_bf[...], preferred_element_type=F32)
        up = jnp.dot(x, wu_bf[...], preferred_element_type=F32)
        act = (gate * jax.nn.sigmoid(gate) * up).astype(BF16)
        o_ref[...] = jnp.dot(act, wd_bf[...], preferred_element_type=F32)

    @pl.when(i >= n_used_ref[0])
    def _():
        o_ref[...] = jnp.zeros_like(o_ref)


def _experts(h2, tok_blocks, blk_e, n_used, w_gate, w_up, w_down):
    n_blk = tok_blocks.shape[0]
    rows = tok_blocks.shape[2]
    d = h2.shape[1]
    de = w_gate.shape[2]
    return pl.pallas_call(
        _expert_kernel,
        grid_spec=pltpu.PrefetchScalarGridSpec(
            num_scalar_prefetch=2,
            grid=(n_blk,),
            in_specs=[pl.BlockSpec((1, 1, rows), lambda i, be, nu: (i, 0, 0), memory_space=pltpu.SMEM),
                      pl.BlockSpec(memory_space=pl.ANY),
                      pl.BlockSpec((1, d, de), lambda i, be, nu: (be[i], 0, 0)),
                      pl.BlockSpec((1, d, de), lambda i, be, nu: (be[i], 0, 0)),
                      pl.BlockSpec((1, de, d), lambda i, be, nu: (be[i], 0, 0))],
            out_specs=pl.BlockSpec((rows, d), lambda i, be, nu: (i, 0)),
            scratch_shapes=[pltpu.VMEM((rows, d), F32),
                            pltpu.VMEM((d, de), BF16),
                            pltpu.VMEM((d, de), BF16),
                            pltpu.VMEM((de, d), BF16),
                            pltpu.SemaphoreType.DMA((1,))]),
        out_shape=jax.ShapeDtypeStruct((n_blk * rows, d), F32),
        compiler_params=_params("arbitrary"),
        name="experts",
    )(blk_e, n_used, tok_blocks, h2, w_gate, w_up, w_down)


def _final_kernel(seg_ref, pos_ref, x1_ref, mod_ref, rt_ref, lg_ref, lb_ref, ys_hbm, o_ref, buf, sem):
    del seg_ref
    tm = x1_ref.shape[0]

    def start(r, carry):
        _row_gather_start(ys_hbm, buf.at[0], sem.at[0], pos_ref[0, 0, r], r)
        _row_gather_start(ys_hbm, buf.at[1], sem.at[1], pos_ref[0, 0, tm + r], r)
        return carry

    def wait(r, carry):
        _row_gather_wait(ys_hbm, buf.at[0], sem.at[0], r)
        _row_gather_wait(ys_hbm, buf.at[1], sem.at[1], r)
        return carry

    lax.fori_loop(0, tm, start, 0)
    lax.fori_loop(0, tm, wait, 0)
    rt = rt_ref[...]
    ffn = rt[:, 0:1] * buf[0] + rt[:, 1:2] * buf[1]
    g2 = mod_ref[0, 5:6, :]
    o_ref[...] = _layer_norm(DEEPNORM_ALPHA * x1_ref[...] + g2 * ffn, lg_ref[...], lb_ref[...])


def _final(x1, mod, tile_seg, pos_blocks, rt, lg, lb, ys, tile0, n_tiles, tm):
    d = x1.shape[1]
    row = lambda i, seg: (tile0 + i, 0)
    const = lambda i, seg: (0, 0)
    return pl.pallas_call(
        _final_kernel,
        grid_spec=pltpu.PrefetchScalarGridSpec(
            num_scalar_prefetch=1,
            grid=(n_tiles,),
            in_specs=[pl.BlockSpec((1, 1, 2 * tm), lambda i, seg: (tile0 + i, 0, 0), memory_space=pltpu.SMEM),
                      pl.BlockSpec((tm, d), row),
                      pl.BlockSpec((1, 8, d), lambda i, seg: (seg[tile0 + i], 0, 0)),
                      pl.BlockSpec((tm, LANES), row),
                      pl.BlockSpec((1, d), const),
                      pl.BlockSpec((1, d), const),
                      pl.BlockSpec(memory_space=pl.ANY)],
            out_specs=pl.BlockSpec((tm, d), lambda i, seg: (i, 0)),
            scratch_shapes=[pltpu.VMEM((2, tm, d), F32), pltpu.SemaphoreType.DMA((2,))]),
        out_shape=jax.ShapeDtypeStruct((n_tiles * tm, d), F32),
        compiler_params=_params("arbitrary"),
        name="combine_ln2",
    )(tile_seg, pos_blocks, x1, mod, rt, lg, lb, ys)


def _rope_tables(pos_a, pos_b):
    inv = ROPE_THETA ** (-jnp.arange(0, 64, 2, dtype=F32) / 64)

    def half(pos):
        if pos is None:
            z = jnp.zeros((pos_a.shape[0], 64), F32)
            return z, z, z
        ang = pos.astype(F32)[:, None] * inv[None, :]
        cos, sin = jnp.cos(ang), jnp.sin(ang)
        zero = jnp.zeros_like(sin)
        return (jnp.concatenate([cos, cos], 1), jnp.concatenate([-sin, zero], 1), jnp.concatenate([zero, sin], 1))

    a, b = half(pos_a), half(pos_b)
    return tuple(jnp.concatenate([u, v], 1) for u, v in zip(a, b))


def _dispatch(eid, rows):
    n = eid.shape[0]
    a = n * TOP_K
    flat = eid.reshape(-1)
    order = jnp.argsort(flat, stable=True).astype(jnp.int32)
    e_sorted = flat[order]
    counts = jnp.bincount(flat, length=N_EXPERTS).astype(jnp.int32)
    starts = jnp.cumsum(counts) - counts
    padded = ((counts + rows - 1) // rows) * rows
    pad_end = jnp.cumsum(padded)
    pad_start = pad_end - padded
    dest = pad_start[e_sorted] + (jnp.arange(a, dtype=jnp.int32) - starts[e_sorted])
    p_rows = ((a + rows - 1) // rows) * rows + N_EXPERTS * rows
    n_blk = p_rows // rows
    buf_tok = jnp.zeros((p_rows,), jnp.int32).at[dest].set(order // TOP_K)
    pos = jnp.zeros((a,), jnp.int32).at[order].set(dest).reshape(n, TOP_K)
    blk_start = jnp.arange(n_blk, dtype=jnp.int32) * rows
    blk_e = jnp.minimum(jnp.searchsorted(pad_end, blk_start, side="right"), N_EXPERTS - 1).astype(jnp.int32)
    n_used = (pad_end[-1] // rows).astype(jnp.int32).reshape(1)
    return buf_tok.reshape(n_blk, 1, rows), pos, blk_e, n_used


def _layer(groups, w_ada, b_ada, w_in, a_q_norm, a_kv_norm, a_w_uq, a_w_ukv, a_w_o, b_q_norm, b_k_norm, b_w_o,
           w_out, ln1_g, ln1_b, w_group, b_group, w_expert, b_expert, e_w_gate, e_w_up, e_w_down, ln2_g, ln2_b):
    d = w_in.shape[0]
    seqs = [x.shape[1] for x, _ in groups]
    seg = functools.reduce(math.gcd, seqs)
    tm = _tile(seg, 512)
    s_max = max(seqs)

    x_all = jnp.concatenate([x.reshape(-1, d) for x, _ in groups], axis=0)
    n = x_all.shape[0]
    c_all = jnp.concatenate([c for _, c in groups], axis=0)
    nb = c_all.shape[0]
    c_pad = jnp.zeros((-(-nb // 8) * 8, d), F32).at[:nb].set(c_all)
    seg_batch, tile_seg, tile_pos = [], [], []
    b0 = 0
    for x, _ in groups:
        bsz, s = x.shape[0], x.shape[1]
        for b in range(bsz):
            for t in range(s // tm):
                tile_seg.append(b0 + b)
                tile_pos.append(t)
        b0 += bsz
    tile_seg = jnp.asarray(np.asarray(tile_seg, np.int32))
    tile_pos = jnp.asarray(np.asarray(tile_pos, np.int32))

    mod = _ada(c_pad, w_ada, b_ada)[:nb].reshape(nb, 6, d)
    mod = jnp.concatenate([mod, jnp.zeros((nb, 2, d), F32)], axis=1)

    o_q, o_kv, o_kr, o_qb, o_kb, o_vb, o_ga, o_gb = np.cumsum(
        [0, A_Q_RANK, A_KV_RANK, A_ROPE, B_HEADS * B_HEAD_DIM, B_KV_HEADS * B_HEAD_DIM,
         B_KV_HEADS * B_HEAD_DIM, d]).tolist()
    w_lat = jnp.pad(w_in[:, :o_qb], ((0, 0), (0, o_kr + LANES - o_qb))).astype(BF16)
    w_qk = w_in[:, o_qb:o_vb].astype(BF16)
    w_v = w_in[:, o_vb:o_ga].astype(BF16)
    w_g = w_in[:, o_ga:].astype(BF16)
    w_uq = jnp.pad(a_w_uq.reshape(A_Q_RANK, A_HEADS, A_NOPE + A_ROPE),
                   ((0, 0), (0, 0), (0, A_QK_PAD - A_NOPE - A_ROPE))).reshape(A_Q_RANK, A_HEADS * A_QK_PAD).astype(BF16)
    w_ukv = a_w_ukv.reshape(A_KV_RANK, A_HEADS, A_NOPE + A_V)
    w_uk = w_ukv[:, :, :A_NOPE].reshape(A_KV_RANK, A_HEADS * A_NOPE).astype(BF16)
    w_uv = w_ukv[:, :, A_NOPE:].reshape(A_KV_RANK, A_HEADS * A_V).astype(BF16)
    w_r = jnp.zeros((d, LANES), F32).at[:, :N_GROUPS].set(w_group).at[:, N_GROUPS:N_GROUPS + N_EXPERTS].set(w_expert)
    b_r = jnp.zeros((1, LANES), F32).at[0, :N_GROUPS].set(b_group).at[0, N_GROUPS:N_GROUPS + N_EXPERTS].set(b_expert)

    t = jnp.arange(s_max, dtype=jnp.int32)
    tabs_a = _rope_tables(t, None)
    tabs_b = _rope_tables(t // GRID_W, t % GRID_W)

    h, lat = _mod_proj(x_all, mod, tile_seg, w_lat, tm)
    qk_b = _proj_qk(h, w_qk, jnp.stack([b_q_norm, b_k_norm]).reshape(2, 1, B_HEAD_DIM), tabs_b, tile_pos, tm,
                    B_HEADS // B_GROUP)
    v_b = _proj(h, w_v, tm, 512, "plain", BF16)
    gates = _proj(h, w_g, tm, 512, "gate", F32)
    q_a = _mla_q(lat, a_q_norm.reshape(1, -1), w_uq, tabs_a, tile_pos, tm)
    k_a, v_a = _mla_kv(lat, a_kv_norm.reshape(1, -1), w_uk, w_uv, tabs_a, tile_pos, tm)

    oa_parts, ob_parts = [], []
    row0 = 0
    for x, _ in groups:
        bsz, s = x.shape[0], x.shape[1]
        oa_parts.append(_flash(q_a, k_a, v_a, batch=bsz, seq=s, row0=row0, n_kv_heads=A_HEADS, group=1,
                               dq=A_QK_PAD, dv=A_V, q_col0=0, k_col0=0, v_col0=0,
                               scale=(A_NOPE + A_ROPE) ** -0.5, out_rows=bsz * s))
        ob_parts.append(_flash(qk_b, qk_b, v_b, batch=bsz, seq=s, row0=row0, n_kv_heads=B_KV_HEADS, group=B_GROUP,
                               dq=B_HEAD_DIM, dv=B_HEAD_DIM, q_col0=0, k_col0=B_HEADS * B_HEAD_DIM, v_col0=0,
                               scale=B_HEAD_DIM ** -0.5, out_rows=bsz * s))
        row0 += bsz * s
    o_a = jnp.concatenate(oa_parts, axis=0)
    o_b = jnp.concatenate(ob_parts, axis=0)

    gated = _gate_mix(o_a, o_b, a_w_o.astype(BF16), b_w_o.astype(BF16), gates, tm, 512)
    tm2 = _tile(seg, 256)
    seg2 = jnp.repeat(tile_seg, tm // tm2)
    x1, h2, rt = _ln1(gated, x_all, mod, seg2, w_out.astype(BF16), ln1_g.reshape(1, d), ln1_b.reshape(1, d),
                      w_r, b_r, tm2)

    eid = rt[:, 2:4].astype(jnp.int32)
    tok_blocks, pos, blk_e, n_used = _dispatch(eid, EXPERT_ROWS)
    ys = _experts(h2, tok_blocks, blk_e, n_used, e_w_gate, e_w_up, e_w_down)
    pos_blocks = pos.reshape(n // tm2, tm2, TOP_K).transpose(0, 2, 1).reshape(n // tm2, 1, TOP_K * tm2)

    outs = []
    tile0 = 0
    for x, _ in groups:
        bsz, s = x.shape[0], x.shape[1]
        nt = bsz * s // tm2
        y = _final(x1, mod, seg2, pos_blocks, rt, ln2_g.reshape(1, d), ln2_b.reshape(1, d), ys, tile0, nt, tm2)
        outs.append(y.reshape(bsz, s, d))
        tile0 += nt
    return tuple(outs)


def kernel(x_prompt, x_sample, c_prompt, c_sample, w_ada, b_ada, w_in, a_q_norm, a_kv_norm, a_w_uq, a_w_ukv, a_w_o,
           b_q_norm, b_k_norm, b_w_o, w_out, ln1_g, ln1_b, w_group, b_group, w_expert, b_expert, e_w_gate, e_w_up,
           e_w_down, ln2_g, ln2_b):
    weights = (w_ada, b_ada, w_in, a_q_norm, a_kv_norm, a_w_uq, a_w_ukv, a_w_o, b_q_norm, b_k_norm, b_w_o, w_out,
               ln1_g, ln1_b, w_group, b_group, w_expert, b_expert, e_w_gate, e_w_up, e_w_down, ln2_g, ln2_b)
    return _layer(((x_prompt, c_prompt), (x_sample, c_sample)), *[w[0] for w in weights])
```

```python
import functools
import math

import numpy as np
import jax
import jax.numpy as jnp
from jax import lax
from jax.experimental import pallas as pl
from jax.experimental.pallas import tpu as pltpu

F32 = jnp.float32
BF16 = jnp.bfloat16
HIGHEST = lax.Precision.HIGHEST

GRID_W = 64
ROPE_THETA = 10000.0
A_HEADS = 16
A_Q_RANK = 512
A_KV_RANK = 512
A_NOPE = 128
A_ROPE = 64
A_V = 128
A_QK_PAD = 256
B_HEADS = 16
B_KV_HEADS = 4
B_GROUP = B_HEADS // B_KV_HEADS
B_HEAD_DIM = 128
N_GROUPS = 8
EXPERTS_PER_GROUP = 8
N_EXPERTS = N_GROUPS * EXPERTS_PER_GROUP
TOP_K = 2
RMS_EPS = 1e-6
LN_EPS = 1e-5
DEPTH = 1
DEEPNORM_ALPHA = (2 * DEPTH) ** 0.25
LOG2E = 1.4426950408889634

LANES = 128
EXPERT_ROWS = 256
VMEM_LIMIT = 48 * 1024 * 1024


def _params(*sem):
    return pltpu.CompilerParams(dimension_semantics=sem, vmem_limit_bytes=VMEM_LIMIT)


def _tile(n, pref):
    t = min(n, pref)
    while n % t:
        t //= 2
    return t


def _ada_kernel(c_ref, w_ref, b_ref, o_ref):
    c = c_ref[...]
    a = c * jax.nn.sigmoid(c)
    o_ref[...] = jnp.dot(a, w_ref[...], precision=HIGHEST, preferred_element_type=F32) + b_ref[...]


def _ada(c_pad, w_ada, b_ada):
    d, n = w_ada.shape
    tn = _tile(n, 1024)
    return pl.pallas_call(
        _ada_kernel,
        grid=(n // tn,),
        in_specs=[pl.BlockSpec(c_pad.shape, lambda j: (0, 0)),
                  pl.BlockSpec((d, tn), lambda j: (0, j)),
                  pl.BlockSpec((1, tn), lambda j: (0, j))],
        out_specs=pl.BlockSpec((c_pad.shape[0], tn), lambda j: (0, j)),
        out_shape=jax.ShapeDtypeStruct((c_pad.shape[0], n), F32),
        compiler_params=_params("arbitrary"),
        name="ada",
    )(c_pad, w_ada, b_ada.reshape(1, n))


def _mod_proj_kernel(seg_ref, x_ref, mod_ref, w_ref, h_ref, o_ref):
    del seg_ref
    sh = mod_ref[0, 0:1, :]
    sc = mod_ref[0, 1:2, :]
    h = (x_ref[...] * (1.0 + sc) + sh).astype(BF16)
    h_ref[...] = h
    o_ref[...] = jnp.dot(h, w_ref[...], preferred_element_type=F32)


def _mod_proj(x, mod, tile_seg, w, tm):
    n, d = x.shape
    nc = w.shape[1]
    return pl.pallas_call(
        _mod_proj_kernel,
        grid_spec=pltpu.PrefetchScalarGridSpec(
            num_scalar_prefetch=1,
            grid=(n // tm,),
            in_specs=[pl.BlockSpec((tm, d), lambda i, seg: (i, 0)),
                      pl.BlockSpec((1, 8, d), lambda i, seg: (seg[i], 0, 0)),
                      pl.BlockSpec((d, nc), lambda i, seg: (0, 0))],
            out_specs=[pl.BlockSpec((tm, d), lambda i, seg: (i, 0)),
                       pl.BlockSpec((tm, nc), lambda i, seg: (i, 0))]),
        out_shape=[jax.ShapeDtypeStruct((n, d), BF16), jax.ShapeDtypeStruct((n, nc), F32)],
        compiler_params=_params("arbitrary"),
        name="mod_proj",
    )(tile_seg, x, mod, w)


def _rope(x, cos, sin_a, sin_b):
    return x * cos + pltpu.roll(x, 96, 1) * sin_a + pltpu.roll(x, 32, 1) * sin_b


def _proj_t_kernel(h_ref, w_ref, o_ref):
    o_ref[...] = jnp.dot(h_ref[...], w_ref[...], preferred_element_type=F32).T.astype(o_ref.dtype)


def _proj_gate_kernel(h_ref, w_ref, o_ref):
    o_ref[...] = jax.nn.sigmoid(jnp.dot(h_ref[...], w_ref[...], preferred_element_type=F32))


def _proj_qk_kernel(pos_ref, h_ref, w_ref, g_ref, cos_ref, sa_ref, sb_ref, o_ref):
    del pos_ref
    acc = jnp.dot(h_ref[...], w_ref[...], preferred_element_type=F32)
    g = g_ref[0]
    cos, sa, sb = cos_ref[...], sa_ref[...], sb_ref[...]
    for hh in range(acc.shape[1] // B_HEAD_DIM):
        x = acc[:, hh * B_HEAD_DIM:(hh + 1) * B_HEAD_DIM]
        r = lax.rsqrt(jnp.mean(x * x, axis=-1, keepdims=True) + RMS_EPS)
        y = x * r * g
        o_ref[:, hh * B_HEAD_DIM:(hh + 1) * B_HEAD_DIM] = _rope(y, cos, sa, sb).astype(o_ref.dtype)


def _proj(h, w, tm, tn, kind, out_dtype):
    n, d = h.shape
    nc = w.shape[1]
    if kind == "gate":
        kern, out_spec, out_shape = _proj_gate_kernel, pl.BlockSpec((tm, tn), lambda i, j: (i, j)), (n, nc)
    else:
        kern, out_spec, out_shape = _proj_t_kernel, pl.BlockSpec((tn, tm), lambda i, j: (j, i)), (nc, n)
    return pl.pallas_call(
        kern,
        grid=(n // tm, nc // tn),
        in_specs=[pl.BlockSpec((tm, d), lambda i, j: (i, 0)),
                  pl.BlockSpec((d, tn), lambda i, j: (0, j))],
        out_specs=out_spec,
        out_shape=jax.ShapeDtypeStruct(out_shape, out_dtype),
        compiler_params=_params("arbitrary", "arbitrary"),
        name="proj_" + kind,
    )(h, w)


def _proj_qk(h, w, gains, tabs, tile_pos, tm, n_q_tiles):
    n, d = h.shape
    nc = w.shape[1]
    tn = B_GROUP * B_HEAD_DIM
    tab_spec = pl.BlockSpec((tm, LANES), lambda i, j, pos: (pos[i], 0))
    return pl.pallas_call(
        _proj_qk_kernel,
        grid_spec=pltpu.PrefetchScalarGridSpec(
            num_scalar_prefetch=1,
            grid=(n // tm, nc // tn),
            in_specs=[pl.BlockSpec((tm, d), lambda i, j, pos: (i, 0)),
                      pl.BlockSpec((d, tn), lambda i, j, pos: (0, j)),
                      pl.BlockSpec((1, 1, LANES), lambda i, j, pos: (jnp.where(j < n_q_tiles, 0, 1), 0, 0)),
                      tab_spec, tab_spec, tab_spec],
            out_specs=pl.BlockSpec((tm, tn), lambda i, j, pos: (i, j))),
        out_shape=jax.ShapeDtypeStruct((n, nc), BF16),
        compiler_params=_params("arbitrary", "arbitrary"),
        name="proj_qk",
    )(tile_pos, h, w, gains, *tabs)


def _rms(x, g):
    return x * lax.rsqrt(jnp.mean(x * x, axis=-1, keepdims=True) + RMS_EPS) * g


def _mla_q_kernel(pos_ref, c_ref, g_ref, w_ref, cos_ref, sa_ref, sb_ref, o_ref):
    del pos_ref
    xn = _rms(c_ref[...], g_ref[...]).astype(BF16)
    acc = jnp.dot(xn, w_ref[...], preferred_element_type=F32)
    cos, sa, sb = cos_ref[...], sa_ref[...], sb_ref[...]
    for hh in range(acc.shape[1] // A_QK_PAD):
        base = hh * A_QK_PAD
        o_ref[:, base:base + A_NOPE] = acc[:, base:base + A_NOPE].astype(o_ref.dtype)
        o_ref[:, base + A_NOPE:base + A_QK_PAD] = _rope(
            acc[:, base + A_NOPE:base + A_QK_PAD], cos, sa, sb).astype(o_ref.dtype)


def _mla_q(cqkv, gain, w, tabs, tile_pos, tm):
    n = cqkv.shape[0]
    nc = w.shape[1]
    tn = 4 * A_QK_PAD
    tab_spec = pl.BlockSpec((tm, LANES), lambda i, j, pos: (pos[i], 0))
    return pl.pallas_call(
        _mla_q_kernel,
        grid_spec=pltpu.PrefetchScalarGridSpec(
            num_scalar_prefetch=1,
            grid=(n // tm, nc // tn),
            in_specs=[pl.BlockSpec((tm, A_Q_RANK), lambda i, j, pos: (i, 0)),
                      pl.BlockSpec((1, A_Q_RANK), lambda i, j, pos: (0, 0)),
                      pl.BlockSpec((A_Q_RANK, tn), lambda i, j, pos: (0, j)),
                      tab_spec, tab_spec, tab_spec],
            out_specs=pl.BlockSpec((tm, tn), lambda i, j, pos: (i, j))),
        out_shape=jax.ShapeDtypeStruct((n, nc), BF16),
        compiler_params=_params("arbitrary", "arbitrary"),
        name="mla_q",
    )(tile_pos, cqkv, gain, w, *tabs)


def _mla_kv_kernel(pos_ref, c_ref, kr_ref, g_ref, wk_ref, wv_ref, cos_ref, sa_ref, sb_ref, k_ref, v_ref):
    del pos_ref
    xn = _rms(c_ref[...], g_ref[...]).astype(BF16)
    kn = jnp.dot(xn, wk_ref[...], preferred_element_type=F32)
    v_ref[...] = jnp.dot(xn, wv_ref[...], preferred_element_type=F32).T.astype(v_ref.dtype)
    kr = _rope(kr_ref[...], cos_ref[...], sa_ref[...], sb_ref[...]).astype(k_ref.dtype)
    for hh in range(kn.shape[1] // A_NOPE):
        k_ref[:, hh * A_QK_PAD:hh * A_QK_PAD + A_NOPE] = kn[:, hh * A_NOPE:(hh + 1) * A_NOPE].astype(k_ref.dtype)
        k_ref[:, hh * A_QK_PAD + A_NOPE:(hh + 1) * A_QK_PAD] = kr


def _mla_kv(cqkv, gain, wk, wv, tabs, tile_pos, tm):
    n = cqkv.shape[0]
    hpt = 4
    n_ct = A_HEADS // hpt
    kr_blk = (A_Q_RANK + A_KV_RANK) // LANES
    tab_spec = pl.BlockSpec((tm, LANES), lambda i, j, pos: (pos[i], 0))
    return pl.pallas_call(
        _mla_kv_kernel,
        grid_spec=pltpu.PrefetchScalarGridSpec(
            num_scalar_prefetch=1,
            grid=(n // tm, n_ct),
            in_specs=[pl.BlockSpec((tm, A_KV_RANK), lambda i, j, pos: (i, 1)),
                      pl.BlockSpec((tm, LANES), lambda i, j, pos: (i, kr_blk)),
                      pl.BlockSpec((1, A_KV_RANK), lambda i, j, pos: (0, 0)),
                      pl.BlockSpec((A_KV_RANK, hpt * A_NOPE), lambda i, j, pos: (0, j)),
                      pl.BlockSpec((A_KV_RANK, hpt * A_V), lambda i, j, pos: (0, j)),
                      tab_spec, tab_spec, tab_spec],
            out_specs=[pl.BlockSpec((tm, hpt * A_QK_PAD), lambda i, j, pos: (i, j)),
                       pl.BlockSpec((hpt * A_V, tm), lambda i, j, pos: (j, i))]),
        out_shape=[jax.ShapeDtypeStruct((n, A_HEADS * A_QK_PAD), BF16),
                   jax.ShapeDtypeStruct((A_HEADS * A_V, n), BF16)],
        compiler_params=_params("arbitrary", "arbitrary"),
        name="mla_kv",
    )(tile_pos, cqkv, cqkv, gain, wk, wv, *tabs)


def _flash_kernel(q_ref, k_ref, vt_ref, o_ref, q_sc, m_sc, l_sc, acc_sc, *, group, dq, dv, tk, c):
    tq = q_ref.shape[0]
    n_kv = k_ref.shape[0] // tk
    for g in range(group):
        q_sc[g * tq:(g + 1) * tq, :] = (q_ref[:, g * dq:(g + 1) * dq].astype(F32) * c).astype(BF16)
    m_sc[...] = jnp.full(m_sc.shape, -1e30, F32)
    l_sc[...] = jnp.zeros(l_sc.shape, F32)
    acc_sc[...] = jnp.zeros(acc_sc.shape, F32)

    def body(j, carry):
        start = pl.multiple_of(j * tk, tk)
        k = k_ref[pl.ds(start, tk), :]
        vt = vt_ref[:, pl.ds(start, tk)]
        s = lax.dot_general(k, q_sc[...], (((1,), (1,)), ((), ())), preferred_element_type=F32)
        m_prev = m_sc[...]
        m_new = jnp.maximum(m_prev, jnp.max(s, axis=0, keepdims=True))
        alpha = jnp.exp2(m_prev - m_new)
        p = jnp.exp2(s - m_new)
        l_sc[...] = alpha * l_sc[...] + jnp.sum(p, axis=0, keepdims=True)
        acc_sc[...] = alpha * acc_sc[...] + jnp.dot(vt, p.astype(BF16), preferred_element_type=F32)
        m_sc[...] = m_new
        return carry

    lax.fori_loop(0, n_kv, body, 0)
    out = (acc_sc[...] / l_sc[...]).T
    for g in range(group):
        o_ref[:, g * dv:(g + 1) * dv] = out[g * tq:(g + 1) * tq, :].astype(o_ref.dtype)


def _flash(q2d, k2d, vt2d, *, batch, seq, row0, n_kv_heads, group, dq, dv, k_col0, scale, tq, tk):
    assert row0 % seq == 0 and k_col0 % dq == 0
    tq = _tile(seq, tq)
    tk = _tile(seq, tk)
    qb0, kb0 = row0 // tq, row0 // seq
    nq = seq // tq
    kc0 = k_col0 // dq
    rows = group * tq
    kern = functools.partial(_flash_kernel, group=group, dq=dq, dv=dv, tk=tk, c=scale * LOG2E)
    return pl.pallas_call(
        kern,
        grid=(batch, n_kv_heads, nq),
        in_specs=[pl.BlockSpec((tq, group * dq), lambda b, h, i: (qb0 + b * nq + i, h)),
                  pl.BlockSpec((seq, dq), lambda b, h, i: (kb0 + b, kc0 + h)),
                  pl.BlockSpec((dv, seq), lambda b, h, i: (h, kb0 + b))],
        out_specs=pl.BlockSpec((tq, group * dv), lambda b, h, i: (b * nq + i, h)),
        out_shape=jax.ShapeDtypeStruct((batch * seq, n_kv_heads * group * dv), BF16),
        scratch_shapes=[pltpu.VMEM((rows, dq), BF16), pltpu.VMEM((1, rows), F32), pltpu.VMEM((1, rows), F32),
                        pltpu.VMEM((dv, rows), F32)],
        compiler_params=_params("arbitrary", "arbitrary", "arbitrary"),
        name="flash",
    )(q2d, k2d, vt2d)


def _gate_mix_kernel(oa_ref, ob_ref, wa_ref, wb_ref, ga_ref, gb_ref, o_ref):
    ya = jnp.dot(oa_ref[...], wa_ref[...], preferred_element_type=F32)
    yb = jnp.dot(ob_ref[...], wb_ref[...], preferred_element_type=F32)
    o_ref[...] = (ga_ref[...] * ya + gb_ref[...] * yb).astype(o_ref.dtype)


def _gate_mix(oa, ob, wa, wb, gates, tm, tn):
    n, d = oa.shape
    dm = wa.shape[1]
    nct = dm // tn
    return pl.pallas_call(
        _gate_mix_kernel,
        grid=(n // tm, nct),
        in_specs=[pl.BlockSpec((tm, d), lambda i, j: (i, 0)),
                  pl.BlockSpec((tm, d), lambda i, j: (i, 0)),
                  pl.BlockSpec((d, tn), lambda i, j: (0, j)),
                  pl.BlockSpec((d, tn), lambda i, j: (0, j)),
                  pl.BlockSpec((tm, tn), lambda i, j: (i, j)),
                  pl.BlockSpec((tm, tn), lambda i, j: (i, nct + j))],
        out_specs=pl.BlockSpec((tm, tn), lambda i, j: (i, j)),
        out_shape=jax.ShapeDtypeStruct((n, dm), BF16),
        compiler_params=_params("arbitrary", "arbitrary"),
        name="gate_mix",
    )(oa, ob, wa, wb, gates, gates)


def _layer_norm(y, g, b):
    mu = jnp.mean(y, axis=-1, keepdims=True)
    yc = y - mu
    var = jnp.mean(yc * yc, axis=-1, keepdims=True)
    return yc * lax.rsqrt(var + LN_EPS) * g + b


def _route(logits):
    lane = lax.broadcasted_iota(jnp.int32, logits.shape, 1).astype(F32)
    big = float(LANES)
    ninf = -jnp.inf
    gmask = lane < N_GROUPS
    gl = jnp.where(gmask, logits, ninf)
    gmax = jnp.max(gl, axis=-1, keepdims=True)
    grp = jnp.min(jnp.where(gl == gmax, lane, big), axis=-1, keepdims=True)
    den = jnp.sum(jnp.exp(gl - gmax), axis=-1, keepdims=True)
    p_grp = 1.0 / den
    lo = N_GROUPS + grp * EXPERTS_PER_GROUP
    el = jnp.where(lane >= lo, jnp.where(lane < lo + EXPERTS_PER_GROUP, logits, ninf), ninf)
    t1 = jnp.max(el, axis=-1, keepdims=True)
    i1 = jnp.min(jnp.where(el == t1, lane, big), axis=-1, keepdims=True)
    el2 = jnp.where(lane == i1, ninf, el)
    t2 = jnp.max(el2, axis=-1, keepdims=True)
    i2 = jnp.min(jnp.where(el2 == t2, lane, big), axis=-1, keepdims=True)
    e = jnp.exp(t2 - t1)
    w1 = p_grp / (1.0 + e)
    w2 = p_grp * e / (1.0 + e)
    out = jnp.where(lane == 0, w1, jnp.where(lane == 1, w2, jnp.where(lane == 2, i1 - N_GROUPS, i2 - N_GROUPS)))
    return jnp.where(lane < 4, out, 0.0)


def _ln1_kernel(seg_ref, g_ref, x_ref, mod_ref, w_ref, lg_ref, lb_ref, wr_ref, br_ref, x1_ref, h2_ref, rt_ref):
    del seg_ref
    attn = jnp.dot(g_ref[...], w_ref[...], preferred_element_type=F32)
    g1 = mod_ref[0, 2:3, :]
    sh2 = mod_ref[0, 3:4, :]
    sc2 = mod_ref[0, 4:5, :]
    x1 = _layer_norm(DEEPNORM_ALPHA * x_ref[...] + g1 * attn, lg_ref[...], lb_ref[...])
    x1_ref[...] = x1
    h2 = x1 * (1.0 + sc2) + sh2
    h2_ref[...] = h2
    logits = jnp.dot(h2, wr_ref[...], precision=HIGHEST, preferred_element_type=F32) + br_ref[...]
    rt_ref[...] = _route(logits)


def _ln1(gated, x, mod, tile_seg, w_out, lg, lb, wr, br, tm):
    n, d = x.shape
    row = lambda i, seg: (i, 0)
    const = lambda i, seg: (0, 0)
    return pl.pallas_call(
        _ln1_kernel,
        grid_spec=pltpu.PrefetchScalarGridSpec(
            num_scalar_prefetch=1,
            grid=(n // tm,),
            in_specs=[pl.BlockSpec((tm, d), row),
                      pl.BlockSpec((tm, d), row),
                      pl.BlockSpec((1, 8, d), lambda i, seg: (seg[i], 0, 0)),
                      pl.BlockSpec((d, d), const),
                      pl.BlockSpec((1, d), const),
                      pl.BlockSpec((1, d), const),
                      pl.BlockSpec((d, LANES), const),
                      pl.BlockSpec((1, LANES), const)],
            out_specs=[pl.BlockSpec((tm, d), row), pl.BlockSpec((tm, d), row), pl.BlockSpec((tm, LANES), row)]),
        out_shape=[jax.ShapeDtypeStruct((n, d), F32), jax.ShapeDtypeStruct((n, d), F32),
                   jax.ShapeDtypeStruct((n, LANES), F32)],
        compiler_params=_params("arbitrary"),
        name="ln1_route",
    )(tile_seg, gated, x, mod, w_out, lg, lb, wr, br)


def _row_gather_start(src_hbm, dst, sem, idx, row):
    pltpu.make_async_copy(src_hbm.at[pl.ds(idx, 1), :], dst.at[pl.ds(row, 1), :], sem).start()


def _row_gather_wait(src_hbm, dst, sem, row):
    pltpu.make_async_copy(src_hbm.at[pl.ds(0, 1), :], dst.at[pl.ds(row, 1), :], sem).wait()


def _expert_kernel(blk_e_ref, n_used_ref, tok_ref, h2_hbm, wg_ref, wu_ref, wd_ref, o_ref,
                   xbuf, wg_bf, wu_bf, wd_bf, sem):
    i = pl.program_id(0)
    e = blk_e_ref[i]
    prev = blk_e_ref[jnp.maximum(i - 1, 0)]
    rows = xbuf.shape[0]

    @pl.when(jnp.logical_or(i == 0, e != prev))
    def _():
        wg_bf[...] = wg_ref[0].astype(BF16)
        wu_bf[...] = wu_ref[0].astype(BF16)
        wd_bf[...] = wd_ref[0].astype(BF16)

    @pl.when(i < n_used_ref[0])
    def _():
        def start(r, carry):
            _row_gather_start(h2_hbm, xbuf, sem.at[0], tok_ref[0, 0, r], r)
            return carry

        def wait(r, carry):
            _row_gather_wait(h2_hbm, xbuf, sem.at[0], r)
            return carry

        lax.fori_loop(0, rows, start, 0)
        lax.fori_loop(0, rows, wait, 0)
        x = xbuf[...].astype(BF16)
        gate = jnp.dot(x, wg_bf[...], preferred_element_type=F32)
        up = jnp.dot(x, wu_bf[...], preferred_element_type=F32)
        act = (gate * jax.nn.sigmoid(gate) * up).astype(BF16)
        o_ref[...] = jnp.dot(act, wd_bf[...], preferred_element_type=F32)

    @pl.when(i >= n_used_ref[0])
    def _():
        o_ref[...] = jnp.zeros_like(o_ref)


def _experts(h2, tok_blocks, blk_e, n_used, w_gate, w_up, w_down):
    n_blk = tok_blocks.shape[0]
    rows = tok_blocks.shape[2]
    d = h2.shape[1]
    de = w_gate.shape[2]
    return pl.pallas_call(
        _expert_kernel,
        grid_spec=pltpu.PrefetchScalarGridSpec(
            num_scalar_prefetch=2,
            grid=(n_blk,),
            in_specs=[pl.BlockSpec((1, 1, rows), lambda i, be, nu: (i, 0, 0), memory_space=pltpu.SMEM),
                      pl.BlockSpec(memory_space=pl.ANY),
                      pl.BlockSpec((1, d, de), lambda i, be, nu: (be[i], 0, 0)),
                      pl.BlockSpec((1, d, de), lambda i, be, nu: (be[i], 0, 0)),
                      pl.BlockSpec((1, de, d), lambda i, be, nu: (be[i], 0, 0))],
            out_specs=pl.BlockSpec((rows, d), lambda i, be, nu: (i, 0)),
            scratch_shapes=[pltpu.VMEM((rows, d), F32),
                            pltpu.VMEM((d, de), BF16),
                            pltpu.VMEM((d, de), BF16),
                            pltpu.VMEM((de, d), BF16),
                            pltpu.SemaphoreType.DMA((1,))]),
        out_shape=jax.ShapeDtypeStruct((n_blk * rows, d), F32),
        compiler_params=_params("arbitrary"),
        name="experts",
    )(blk_e, n_used, tok_blocks, h2, w_gate, w_up, w_down)


def _final_kernel(seg_ref, pos_ref, x1_ref, mod_ref, rt_ref, lg_ref, lb_ref, ys_hbm, o_ref, buf, sem):
    del seg_ref
    tm = x1_ref.shape[0]

    def start(r, carry):
        _row_gather_start(ys_hbm, buf.at[0], sem.at[0], pos_ref[0, 0, r], r)
        _row_gather_start(ys_hbm, buf.at[1], sem.at[1], pos_ref[0, 0, tm + r], r)
        return carry

    def wait(r, carry):
        _row_gather_wait(ys_hbm, buf.at[0], sem.at[0], r)
        _row_gather_wait(ys_hbm, buf.at[1], sem.at[1], r)
        return carry

    lax.fori_loop(0, tm, start, 0)
    lax.fori_loop(0, tm, wait, 0)
    rt = rt_ref[...]
    ffn = rt[:, 0:1] * buf[0] + rt[:, 1:2] * buf[1]
    g2 = mod_ref[0, 5:6, :]
    o_ref[...] = _layer_norm(DEEPNORM_ALPHA * x1_ref[...] + g2 * ffn, lg_ref[...], lb_ref[...])


def _final(x1, mod, tile_seg, pos_blocks, rt, lg, lb, ys, tile0, n_tiles, tm):
    d = x1.shape[1]
    row = lambda i, seg: (tile0 + i, 0)
    const = lambda i, seg: (0, 0)
    return pl.pallas_call(
        _final_kernel,
        grid_spec=pltpu.PrefetchScalarGridSpec(
            num_scalar_prefetch=1,
            grid=(n_tiles,),
            in_specs=[pl.BlockSpec((1, 1, 2 * tm), lambda i, seg: (tile0 + i, 0, 0), memory_space=pltpu.SMEM),
                      pl.BlockSpec((tm, d), row),
                      pl.BlockSpec((1, 8, d), lambda i, seg: (seg[tile0 + i], 0, 0)),
                      pl.BlockSpec((tm, LANES), row),
                      pl.BlockSpec((1, d), const),
                      pl.BlockSpec((1, d), const),
                      pl.BlockSpec(memory_space=pl.ANY)],
            out_specs=pl.BlockSpec((tm, d), lambda i, seg: (i, 0)),
            scratch_shapes=[pltpu.VMEM((2, tm, d), F32), pltpu.SemaphoreType.DMA((2,))]),
        out_shape=jax.ShapeDtypeStruct((n_tiles * tm, d), F32),
        compiler_params=_params("arbitrary"),
        name="combine_ln2",
    )(tile_seg, pos_blocks, x1, mod, rt, lg, lb, ys)


def _rope_tables(pos_a, pos_b):
    inv = ROPE_THETA ** (-jnp.arange(0, 64, 2, dtype=F32) / 64)

    def half(pos):
        if pos is None:
            z = jnp.zeros((pos_a.shape[0], 64), F32)
            return z, z, z
        ang = pos.astype(F32)[:, None] * inv[None, :]
        cos, sin = jnp.cos(ang), jnp.sin(ang)
        zero = jnp.zeros_like(sin)
        return (jnp.concatenate([cos, cos], 1), jnp.concatenate([-sin, zero], 1), jnp.concatenate([zero, sin], 1))

    a, b = half(pos_a), half(pos_b)
    return tuple(jnp.concatenate([u, v], 1) for u, v in zip(a, b))


def _dispatch(eid, rows):
    n = eid.shape[0]
    a = n * TOP_K
    flat = eid.reshape(-1)
    order = jnp.argsort(flat, stable=True).astype(jnp.int32)
    e_sorted = flat[order]
    counts = jnp.bincount(flat, length=N_EXPERTS).astype(jnp.int32)
    starts = jnp.cumsum(counts) - counts
    padded = ((counts + rows - 1) // rows) * rows
    pad_end = jnp.cumsum(padded)
    pad_start = pad_end - padded
    dest = pad_start[e_sorted] + (jnp.arange(a, dtype=jnp.int32) - starts[e_sorted])
    p_rows = ((a + rows - 1) // rows) * rows + N_EXPERTS * rows
    n_blk = p_rows // rows
    buf_tok = jnp.zeros((p_rows,), jnp.int32).at[dest].set(order // TOP_K)
    pos = jnp.zeros((a,), jnp.int32).at[order].set(dest).reshape(n, TOP_K)
    blk_start = jnp.arange(n_blk, dtype=jnp.int32) * rows
    blk_e = jnp.minimum(jnp.searchsorted(pad_end, blk_start, side="right"), N_EXPERTS - 1).astype(jnp.int32)
    n_used = (pad_end[-1] // rows).astype(jnp.int32).reshape(1)
    return buf_tok.reshape(n_blk, 1, rows), pos, blk_e, n_used


def _layer(groups, w_ada, b_ada, w_in, a_q_norm, a_kv_norm, a_w_uq, a_w_ukv, a_w_o, b_q_norm, b_k_norm, b_w_o,
           w_out, ln1_g, ln1_b, w_group, b_group, w_expert, b_expert, e_w_gate, e_w_up, e_w_down, ln2_g, ln2_b):
    d = w_in.shape[0]
    seqs = [x.shape[1] for x, _ in groups]
    seg = functools.reduce(math.gcd, seqs)
    tm = _tile(seg, 512)
    s_max = max(seqs)

    x_all = jnp.concatenate([x.reshape(-1, d) for x, _ in groups], axis=0)
    n = x_all.shape[0]
    c_all = jnp.concatenate([c for _, c in groups], axis=0)
    nb = c_all.shape[0]
    c_pad = jnp.zeros((-(-nb // 8) * 8, d), F32).at[:nb].set(c_all)
    seg_batch, tile_seg, tile_pos = [], [], []
    b0 = 0
    for x, _ in groups:
        bsz, s = x.shape[0], x.shape[1]
        for b in range(bsz):
            for t in range(s // tm):
                tile_seg.append(b0 + b)
                tile_pos.append(t)
        b0 += bsz
    tile_seg = jnp.asarray(np.asarray(tile_seg, np.int32))
    tile_pos = jnp.asarray(np.asarray(tile_pos, np.int32))

    mod = _ada(c_pad, w_ada, b_ada)[:nb].reshape(nb, 6, d)
    mod = jnp.concatenate([mod, jnp.zeros((nb, 2, d), F32)], axis=1)

    o_q, o_kv, o_kr, o_qb, o_kb, o_vb, o_ga, o_gb = np.cumsum(
        [0, A_Q_RANK, A_KV_RANK, A_ROPE, B_HEADS * B_HEAD_DIM, B_KV_HEADS * B_HEAD_DIM,
         B_KV_HEADS * B_HEAD_DIM, d]).tolist()
    w_lat = jnp.pad(w_in[:, :o_qb], ((0, 0), (0, o_kr + LANES - o_qb))).astype(BF16)
    w_qk = w_in[:, o_qb:o_vb].astype(BF16)
    w_v = w_in[:, o_vb:o_ga].astype(BF16)
    w_g = w_in[:, o_ga:].astype(BF16)
    w_uq = jnp.pad(a_w_uq.reshape(A_Q_RANK, A_HEADS, A_NOPE + A_ROPE),
                   ((0, 0), (0, 0), (0, A_QK_PAD - A_NOPE - A_ROPE))).reshape(A_Q_RANK, A_HEADS * A_QK_PAD).astype(BF16)
    w_ukv = a_w_ukv.reshape(A_KV_RANK, A_HEADS, A_NOPE + A_V)
    w_uk = w_ukv[:, :, :A_NOPE].reshape(A_KV_RANK, A_HEADS * A_NOPE).astype(BF16)
    w_uv = w_ukv[:, :, A_NOPE:].reshape(A_KV_RANK, A_HEADS * A_V).astype(BF16)
    w_r = jnp.zeros((d, LANES), F32).at[:, :N_GROUPS].set(w_group).at[:, N_GROUPS:N_GROUPS + N_EXPERTS].set(w_expert)
    b_r = jnp.zeros((1, LANES), F32).at[0, :N_GROUPS].set(b_group).at[0, N_GROUPS:N_GROUPS + N_EXPERTS].set(b_expert)

    t = jnp.arange(s_max, dtype=jnp.int32)
    tabs_a = _rope_tables(t, None)
    tabs_b = _rope_tables(t // GRID_W, t % GRID_W)

    h, lat = _mod_proj(x_all, mod, tile_seg, w_lat, tm)
    qk_b = _proj_qk(h, w_qk, jnp.stack([b_q_norm, b_k_norm]).reshape(2, 1, B_HEAD_DIM), tabs_b, tile_pos, tm,
                    B_HEADS // B_GROUP)
    v_b = _proj(h, w_v, tm, 512, "t", BF16)
    gates = _proj(h, w_g, tm, 512, "gate", F32)
    q_a = _mla_q(lat, a_q_norm.reshape(1, -1), w_uq, tabs_a, tile_pos, tm)
    k_a, v_a = _mla_kv(lat, a_kv_norm.reshape(1, -1), w_uk, w_uv, tabs_a, tile_pos, tm)

    oa_parts, ob_parts = [], []
    row0 = 0
    for x, _ in groups:
        bsz, s = x.shape[0], x.shape[1]
        oa_parts.append(_flash(q_a, k_a, v_a, batch=bsz, seq=s, row0=row0, n_kv_heads=A_HEADS, group=1,
                               dq=A_QK_PAD, dv=A_V, k_col0=0, scale=(A_NOPE + A_ROPE) ** -0.5, tq=1024, tk=1024))
        ob_parts.append(_flash(qk_b, qk_b, v_b, batch=bsz, seq=s, row0=row0, n_kv_heads=B_KV_HEADS, group=B_GROUP,
                               dq=B_HEAD_DIM, dv=B_HEAD_DIM, k_col0=B_HEADS * B_HEAD_DIM, scale=B_HEAD_DIM ** -0.5,
                               tq=512, tk=512))
        row0 += bsz * s
    o_a = jnp.concatenate(oa_parts, axis=0)
    o_b = jnp.concatenate(ob_parts, axis=0)

    gated = _gate_mix(o_a, o_b, a_w_o.astype(BF16), b_w_o.astype(BF16), gates, tm, 512)
    tm2 = _tile(seg, 256)
    seg2 = jnp.repeat(tile_seg, tm // tm2)
    x1, h2, rt = _ln1(gated, x_all, mod, seg2, w_out.astype(BF16), ln1_g.reshape(1, d), ln1_b.reshape(1, d),
                      w_r, b_r, tm2)

    eid = rt[:, 2:4].astype(jnp.int32)
    tok_blocks, pos, blk_e, n_used = _dispatch(eid, EXPERT_ROWS)
    ys = _experts(h2, tok_blocks, blk_e, n_used, e_w_gate, e_w_up, e_w_down)
    pos_blocks = pos.reshape(n // tm2, tm2, TOP_K).transpose(0, 2, 1).reshape(n // tm2, 1, TOP_K * tm2)

    outs = []
    tile0 = 0
    for x, _ in groups:
        bsz, s = x.shape[0], x.shape[1]
        nt = bsz * s // tm2
        y = _final(x1, mod, seg2, pos_blocks, rt, ln2_g.reshape(1, d), ln2_b.reshape(1, d), ys, tile0, nt, tm2)
        outs.append(y.reshape(bsz, s, d))
        tile0 += nt
    return tuple(outs)


def kernel(x_prompt, x_sample, c_prompt, c_sample, w_ada, b_ada, w_in, a_q_norm, a_kv_norm, a_w_uq, a_w_ukv, a_w_o,
           b_q_norm, b_k_norm, b_w_o, w_out, ln1_g, ln1_b, w_group, b_group, w_expert, b_expert, e_w_gate, e_w_up,
           e_w_down, ln2_g, ln2_b):
    weights = (w_ada, b_ada, w_in, a_q_norm, a_kv_norm, a_w_uq, a_w_ukv, a_w_o, b_q_norm, b_k_norm, b_w_o, w_out,
               ln1_g, ln1_b, w_group, b_group, w_expert, b_expert, e_w_gate, e_w_up, e_w_down, ln2_g, ln2_b)
    return _layer(((x_prompt, c_prompt), (x_sample, c_sample)), *[w[0] for w in weights])
```

```python
import functools
import math

import numpy as np
import jax
import jax.numpy as jnp
from jax import lax
from jax.experimental import pallas as pl
from jax.experimental.pallas import tpu as pltpu

F32 = jnp.float32
BF16 = jnp.bfloat16
HIGHEST = lax.Precision.HIGHEST

GRID_W = 64
ROPE_THETA = 10000.0
A_HEADS = 16
A_Q_RANK = 512
A_KV_RANK = 512
A_NOPE = 128
A_ROPE = 64
A_V = 128
A_QK_PAD = 256
B_HEADS = 16
B_KV_HEADS = 4
B_GROUP = B_HEADS // B_KV_HEADS
B_HEAD_DIM = 128
N_GROUPS = 8
EXPERTS_PER_GROUP = 8
N_EXPERTS = N_GROUPS * EXPERTS_PER_GROUP
TOP_K = 2
RMS_EPS = 1e-6
LN_EPS = 1e-5
DEPTH = 1
DEEPNORM_ALPHA = (2 * DEPTH) ** 0.25
LOG2E = 1.4426950408889634

LANES = 128
EXPERT_ROWS = 256
ONES_ROWS = 16
VMEM_LIMIT = 48 * 1024 * 1024


def _params(*sem):
    return pltpu.CompilerParams(dimension_semantics=sem, vmem_limit_bytes=VMEM_LIMIT)


def _tile(n, pref):
    t = min(n, pref)
    while n % t:
        t //= 2
    return t


def _ada_kernel(c_ref, w_ref, b_ref, o_ref):
    c = c_ref[...]
    a = c * jax.nn.sigmoid(c)
    o_ref[...] = jnp.dot(a, w_ref[...], precision=HIGHEST, preferred_element_type=F32) + b_ref[...]


def _ada(c_pad, w_ada, b_ada):
    d, n = w_ada.shape
    tn = _tile(n, 1024)
    return pl.pallas_call(
        _ada_kernel,
        grid=(n // tn,),
        in_specs=[pl.BlockSpec(c_pad.shape, lambda j: (0, 0)),
                  pl.BlockSpec((d, tn), lambda j: (0, j)),
                  pl.BlockSpec((1, tn), lambda j: (0, j))],
        out_specs=pl.BlockSpec((c_pad.shape[0], tn), lambda j: (0, j)),
        out_shape=jax.ShapeDtypeStruct((c_pad.shape[0], n), F32),
        compiler_params=_params("arbitrary"),
        name="ada",
    )(c_pad, w_ada, b_ada.reshape(1, n))


def _mod_proj_kernel(seg_ref, x_ref, mod_ref, w_ref, h_ref, o_ref):
    del seg_ref
    sh = mod_ref[0, 0:1, :]
    sc = mod_ref[0, 1:2, :]
    h = (x_ref[...] * (1.0 + sc) + sh).astype(BF16)
    h_ref[...] = h
    o_ref[...] = jnp.dot(h, w_ref[...], preferred_element_type=F32)


def _mod_proj(x, mod, tile_seg, w, tm):
    n, d = x.shape
    nc = w.shape[1]
    return pl.pallas_call(
        _mod_proj_kernel,
        grid_spec=pltpu.PrefetchScalarGridSpec(
            num_scalar_prefetch=1,
            grid=(n // tm,),
            in_specs=[pl.BlockSpec((tm, d), lambda i, seg: (i, 0)),
                      pl.BlockSpec((1, 8, d), lambda i, seg: (seg[i], 0, 0)),
                      pl.BlockSpec((d, nc), lambda i, seg: (0, 0))],
            out_specs=[pl.BlockSpec((tm, d), lambda i, seg: (i, 0)),
                       pl.BlockSpec((tm, nc), lambda i, seg: (i, 0))]),
        out_shape=[jax.ShapeDtypeStruct((n, d), BF16), jax.ShapeDtypeStruct((n, nc), F32)],
        compiler_params=_params("arbitrary"),
        name="mod_proj",
    )(tile_seg, x, mod, w)


def _rope(x, cos, sin_a, sin_b):
    return x * cos + pltpu.roll(x, 96, 1) * sin_a + pltpu.roll(x, 32, 1) * sin_b


def _store_vt(vt_ref, v, dv):
    vt = v.T.astype(vt_ref.dtype)
    dve = dv + ONES_ROWS
    for hh in range(v.shape[1] // dv):
        vt_ref[hh * dve:hh * dve + dv, :] = vt[hh * dv:(hh + 1) * dv, :]
        vt_ref[hh * dve + dv:(hh + 1) * dve, :] = jnp.ones((ONES_ROWS, vt.shape[1]), vt_ref.dtype)


def _proj_vt_kernel(h_ref, w_ref, o_ref):
    _store_vt(o_ref, jnp.dot(h_ref[...], w_ref[...], preferred_element_type=F32), B_HEAD_DIM)


def _proj_gate_kernel(h_ref, w_ref, o_ref):
    o_ref[...] = jax.nn.sigmoid(jnp.dot(h_ref[...], w_ref[...], preferred_element_type=F32))


def _proj_qk_kernel(pos_ref, h_ref, w_ref, g_ref, cos_ref, sa_ref, sb_ref, o_ref):
    del pos_ref
    acc = jnp.dot(h_ref[...], w_ref[...], preferred_element_type=F32)
    g = g_ref[0]
    cos, sa, sb = cos_ref[...], sa_ref[...], sb_ref[...]
    for hh in range(acc.shape[1] // B_HEAD_DIM):
        x = acc[:, hh * B_HEAD_DIM:(hh + 1) * B_HEAD_DIM]
        r = lax.rsqrt(jnp.mean(x * x, axis=-1, keepdims=True) + RMS_EPS)
        y = x * r * g
        o_ref[:, hh * B_HEAD_DIM:(hh + 1) * B_HEAD_DIM] = _rope(y, cos, sa, sb).astype(o_ref.dtype)


def _proj(h, w, tm, tn, kind, out_dtype):
    n, d = h.shape
    nc = w.shape[1]
    if kind == "gate":
        kern, out_spec, out_shape = _proj_gate_kernel, pl.BlockSpec((tm, tn), lambda i, j: (i, j)), (n, nc)
    else:
        tne = tn // B_HEAD_DIM * (B_HEAD_DIM + ONES_ROWS)
        kern, out_spec = _proj_vt_kernel, pl.BlockSpec((tne, tm), lambda i, j: (j, i))
        out_shape = (nc // tn * tne, n)
    return pl.pallas_call(
        kern,
        grid=(n // tm, nc // tn),
        in_specs=[pl.BlockSpec((tm, d), lambda i, j: (i, 0)),
                  pl.BlockSpec((d, tn), lambda i, j: (0, j))],
        out_specs=out_spec,
        out_shape=jax.ShapeDtypeStruct(out_shape, out_dtype),
        compiler_params=_params("arbitrary", "arbitrary"),
        name="proj_" + kind,
    )(h, w)


def _proj_qk(h, w, gains, tabs, tile_pos, tm, n_q_tiles):
    n, d = h.shape
    nc = w.shape[1]
    tn = B_GROUP * B_HEAD_DIM
    tab_spec = pl.BlockSpec((tm, LANES), lambda i, j, pos: (pos[i], 0))
    return pl.pallas_call(
        _proj_qk_kernel,
        grid_spec=pltpu.PrefetchScalarGridSpec(
            num_scalar_prefetch=1,
            grid=(n // tm, nc // tn),
            in_specs=[pl.BlockSpec((tm, d), lambda i, j, pos: (i, 0)),
                      pl.BlockSpec((d, tn), lambda i, j, pos: (0, j)),
                      pl.BlockSpec((1, 1, LANES), lambda i, j, pos: (jnp.where(j < n_q_tiles, 0, 1), 0, 0)),
                      tab_spec, tab_spec, tab_spec],
            out_specs=pl.BlockSpec((tm, tn), lambda i, j, pos: (i, j))),
        out_shape=jax.ShapeDtypeStruct((n, nc), BF16),
        compiler_params=_params("arbitrary", "arbitrary"),
        name="proj_qk",
    )(tile_pos, h, w, gains, *tabs)


def _rms(x, g):
    return x * lax.rsqrt(jnp.mean(x * x, axis=-1, keepdims=True) + RMS_EPS) * g


def _mla_q_kernel(pos_ref, c_ref, g_ref, w_ref, cos_ref, sa_ref, sb_ref, o_ref):
    del pos_ref
    xn = _rms(c_ref[...], g_ref[...]).astype(BF16)
    acc = jnp.dot(xn, w_ref[...], preferred_element_type=F32)
    cos, sa, sb = cos_ref[...], sa_ref[...], sb_ref[...]
    for hh in range(acc.shape[1] // A_QK_PAD):
        base = hh * A_QK_PAD
        o_ref[:, base:base + A_NOPE] = acc[:, base:base + A_NOPE].astype(o_ref.dtype)
        o_ref[:, base + A_NOPE:base + A_QK_PAD] = _rope(
            acc[:, base + A_NOPE:base + A_QK_PAD], cos, sa, sb).astype(o_ref.dtype)


def _mla_q(cqkv, gain, w, tabs, tile_pos, tm):
    n = cqkv.shape[0]
    nc = w.shape[1]
    tn = 4 * A_QK_PAD
    tab_spec = pl.BlockSpec((tm, LANES), lambda i, j, pos: (pos[i], 0))
    return pl.pallas_call(
        _mla_q_kernel,
        grid_spec=pltpu.PrefetchScalarGridSpec(
            num_scalar_prefetch=1,
            grid=(n // tm, nc // tn),
            in_specs=[pl.BlockSpec((tm, A_Q_RANK), lambda i, j, pos: (i, 0)),
                      pl.BlockSpec((1, A_Q_RANK), lambda i, j, pos: (0, 0)),
                      pl.BlockSpec((A_Q_RANK, tn), lambda i, j, pos: (0, j)),
                      tab_spec, tab_spec, tab_spec],
            out_specs=pl.BlockSpec((tm, tn), lambda i, j, pos: (i, j))),
        out_shape=jax.ShapeDtypeStruct((n, nc), BF16),
        compiler_params=_params("arbitrary", "arbitrary"),
        name="mla_q",
    )(tile_pos, cqkv, gain, w, *tabs)


def _mla_kv_kernel(pos_ref, c_ref, kr_ref, g_ref, wk_ref, wv_ref, cos_ref, sa_ref, sb_ref, k_ref, v_ref):
    del pos_ref
    xn = _rms(c_ref[...], g_ref[...]).astype(BF16)
    kn = jnp.dot(xn, wk_ref[...], preferred_element_type=F32)
    _store_vt(v_ref, jnp.dot(xn, wv_ref[...], preferred_element_type=F32), A_V)
    kr = _rope(kr_ref[...], cos_ref[...], sa_ref[...], sb_ref[...]).astype(k_ref.dtype)
    for hh in range(kn.shape[1] // A_NOPE):
        k_ref[:, hh * A_QK_PAD:hh * A_QK_PAD + A_NOPE] = kn[:, hh * A_NOPE:(hh + 1) * A_NOPE].astype(k_ref.dtype)
        k_ref[:, hh * A_QK_PAD + A_NOPE:(hh + 1) * A_QK_PAD] = kr


def _mla_kv(cqkv, gain, wk, wv, tabs, tile_pos, tm):
    n = cqkv.shape[0]
    hpt = 4
    n_ct = A_HEADS // hpt
    kr_blk = (A_Q_RANK + A_KV_RANK) // LANES
    tab_spec = pl.BlockSpec((tm, LANES), lambda i, j, pos: (pos[i], 0))
    return pl.pallas_call(
        _mla_kv_kernel,
        grid_spec=pltpu.PrefetchScalarGridSpec(
            num_scalar_prefetch=1,
            grid=(n // tm, n_ct),
            in_specs=[pl.BlockSpec((tm, A_KV_RANK), lambda i, j, pos: (i, 1)),
                      pl.BlockSpec((tm, LANES), lambda i, j, pos: (i, kr_blk)),
                      pl.BlockSpec((1, A_KV_RANK), lambda i, j, pos: (0, 0)),
                      pl.BlockSpec((A_KV_RANK, hpt * A_NOPE), lambda i, j, pos: (0, j)),
                      pl.BlockSpec((A_KV_RANK, hpt * A_V), lambda i, j, pos: (0, j)),
                      tab_spec, tab_spec, tab_spec],
            out_specs=[pl.BlockSpec((tm, hpt * A_QK_PAD), lambda i, j, pos: (i, j)),
                       pl.BlockSpec((hpt * (A_V + ONES_ROWS), tm), lambda i, j, pos: (j, i))]),
        out_shape=[jax.ShapeDtypeStruct((n, A_HEADS * A_QK_PAD), BF16),
                   jax.ShapeDtypeStruct((A_HEADS * (A_V + ONES_ROWS), n), BF16)],
        compiler_params=_params("arbitrary", "arbitrary"),
        name="mla_kv",
    )(tile_pos, cqkv, cqkv, gain, wk, wv, *tabs)


def _flash_kernel(q_ref, k_ref, vt_ref, o_ref, q_sc, m_sc, acc_sc, s_sc, *, group, dq, dv, tk, c):
    tq = q_ref.shape[0]
    n_kv = k_ref.shape[0] // tk
    for g in range(group):
        q_sc[g * tq:(g + 1) * tq, :] = (q_ref[:, g * dq:(g + 1) * dq].astype(F32) * c).astype(BF16)
    m_sc[...] = jnp.full(m_sc.shape, -1e30, F32)
    acc_sc[...] = jnp.zeros(acc_sc.shape, F32)

    def scores(j):
        k = k_ref[pl.ds(pl.multiple_of(j * tk, tk), tk), :]
        return lax.dot_general(k, q_sc[...], (((1,), (1,)), ((), ())), preferred_element_type=F32)

    def step(j, slot, prefetch):
        s = s_sc[slot]
        if prefetch:
            s_sc[1 - slot] = scores(j + 1)
        vt = vt_ref[:, pl.ds(pl.multiple_of(j * tk, tk), tk)]
        m_prev = m_sc[...]
        m_new = jnp.maximum(m_prev, jnp.max(s, axis=0, keepdims=True))
        alpha = jnp.exp2(m_prev - m_new)
        p = jnp.exp2(s - m_new).astype(BF16)
        acc_sc[...] = alpha * acc_sc[...] + jnp.dot(vt, p, preferred_element_type=F32)
        m_sc[...] = m_new

    s_sc[0] = scores(0)

    def pair(jj, carry):
        step(2 * jj, 0, True)
        step(2 * jj + 1, 1, True)
        return carry

    lax.fori_loop(0, n_kv // 2 - 1, pair, 0)
    step(n_kv - 2, 0, True)
    step(n_kv - 1, 1, False)
    acc = acc_sc[...]
    out = (acc[:dv] / acc[dv:dv + 1]).T
    for g in range(group):
        o_ref[:, g * dv:(g + 1) * dv] = out[g * tq:(g + 1) * tq, :].astype(o_ref.dtype)


def _flash(q2d, k2d, vt2d, *, batch, seq, row0, n_kv_heads, group, dq, dv, k_col0, scale, tq, tk):
    assert row0 % seq == 0 and k_col0 % dq == 0
    tq = _tile(seq, tq)
    tk = _tile(seq // 2, tk)
    qb0, kb0 = row0 // tq, row0 // seq
    nq = seq // tq
    kc0 = k_col0 // dq
    rows = group * tq
    dve = dv + ONES_ROWS
    kern = functools.partial(_flash_kernel, group=group, dq=dq, dv=dv, tk=tk, c=scale * LOG2E)
    return pl.pallas_call(
        kern,
        grid=(batch, n_kv_heads, nq),
        in_specs=[pl.BlockSpec((tq, group * dq), lambda b, h, i: (qb0 + b * nq + i, h)),
                  pl.BlockSpec((seq, dq), lambda b, h, i: (kb0 + b, kc0 + h)),
                  pl.BlockSpec((dve, seq), lambda b, h, i: (h, kb0 + b))],
        out_specs=pl.BlockSpec((tq, group * dv), lambda b, h, i: (b * nq + i, h)),
        out_shape=jax.ShapeDtypeStruct((batch * seq, n_kv_heads * group * dv), BF16),
        scratch_shapes=[pltpu.VMEM((rows, dq), BF16), pltpu.VMEM((1, rows), F32), pltpu.VMEM((dve, rows), F32),
                        pltpu.VMEM((2, tk, rows), F32)],
        compiler_params=_params("arbitrary", "arbitrary", "arbitrary"),
        name="flash",
    )(q2d, k2d, vt2d)


def _gate_mix_kernel(oa_ref, ob_ref, wa_ref, wb_ref, ga_ref, gb_ref, o_ref):
    ya = jnp.dot(oa_ref[...], wa_ref[...], preferred_element_type=F32)
    yb = jnp.dot(ob_ref[...], wb_ref[...], preferred_element_type=F32)
    o_ref[...] = (ga_ref[...] * ya + gb_ref[...] * yb).astype(o_ref.dtype)


def _gate_mix(oa, ob, wa, wb, gates, tm, tn):
    n, d = oa.shape
    dm = wa.shape[1]
    nct = dm // tn
    return pl.pallas_call(
        _gate_mix_kernel,
        grid=(n // tm, nct),
        in_specs=[pl.BlockSpec((tm, d), lambda i, j: (i, 0)),
                  pl.BlockSpec((tm, d), lambda i, j: (i, 0)),
                  pl.BlockSpec((d, tn), lambda i, j: (0, j)),
                  pl.BlockSpec((d, tn), lambda i, j: (0, j)),
                  pl.BlockSpec((tm, tn), lambda i, j: (i, j)),
                  pl.BlockSpec((tm, tn), lambda i, j: (i, nct + j))],
        out_specs=pl.BlockSpec((tm, tn), lambda i, j: (i, j)),
        out_shape=jax.ShapeDtypeStruct((n, dm), BF16),
        compiler_params=_params("arbitrary", "arbitrary"),
        name="gate_mix",
    )(oa, ob, wa, wb, gates, gates)


def _layer_norm(y, g, b):
    mu = jnp.mean(y, axis=-1, keepdims=True)
    yc = y - mu
    var = jnp.mean(yc * yc, axis=-1, keepdims=True)
    return yc * lax.rsqrt(var + LN_EPS) * g + b


def _route(logits):
    lane = lax.broadcasted_iota(jnp.int32, logits.shape, 1).astype(F32)
    big = float(LANES)
    ninf = -jnp.inf
    gmask = lane < N_GROUPS
    gl = jnp.where(gmask, logits, ninf)
    gmax = jnp.max(gl, axis=-1, keepdims=True)
    grp = jnp.min(jnp.where(gl == gmax, lane, big), axis=-1, keepdims=True)
    den = jnp.sum(jnp.exp(gl - gmax), axis=-1, keepdims=True)
    p_grp = 1.0 / den
    lo = N_GROUPS + grp * EXPERTS_PER_GROUP
    el = jnp.where(lane >= lo, jnp.where(lane < lo + EXPERTS_PER_GROUP, logits, ninf), ninf)
    t1 = jnp.max(el, axis=-1, keepdims=True)
    i1 = jnp.min(jnp.where(el == t1, lane, big), axis=-1, keepdims=True)
    el2 = jnp.where(lane == i1, ninf, el)
    t2 = jnp.max(el2, axis=-1, keepdims=True)
    i2 = jnp.min(jnp.where(el2 == t2, lane, big), axis=-1, keepdims=True)
    e = jnp.exp(t2 - t1)
    w1 = p_grp / (1.0 + e)
    w2 = p_grp * e / (1.0 + e)
    out = jnp.where(lane == 0, w1, jnp.where(lane == 1, w2, jnp.where(lane == 2, i1 - N_GROUPS, i2 - N_GROUPS)))
    return jnp.where(lane < 4, out, 0.0)


def _ln1_kernel(seg_ref, g_ref, x_ref, mod_ref, w_ref, lg_ref, lb_ref, wr_ref, br_ref, x1_ref, h2_ref, rt_ref):
    del seg_ref
    attn = jnp.dot(g_ref[...], w_ref[...], preferred_element_type=F32)
    g1 = mod_ref[0, 2:3, :]
    sh2 = mod_ref[0, 3:4, :]
    sc2 = mod_ref[0, 4:5, :]
    x1 = _layer_norm(DEEPNORM_ALPHA * x_ref[...] + g1 * attn, lg_ref[...], lb_ref[...])
    x1_ref[...] = x1
    h2 = x1 * (1.0 + sc2) + sh2
    h2_ref[...] = h2
    logits = jnp.dot(h2, wr_ref[...], precision=HIGHEST, preferred_element_type=F32) + br_ref[...]
    rt_ref[...] = _route(logits)


def _ln1(gated, x, mod, tile_seg, w_out, lg, lb, wr, br, tm):
    n, d = x.shape
    row = lambda i, seg: (i, 0)
    const = lambda i, seg: (0, 0)
    return pl.pallas_call(
        _ln1_kernel,
        grid_spec=pltpu.PrefetchScalarGridSpec(
            num_scalar_prefetch=1,
            grid=(n // tm,),
            in_specs=[pl.BlockSpec((tm, d), row),
                      pl.BlockSpec((tm, d), row),
                      pl.BlockSpec((1, 8, d), lambda i, seg: (seg[i], 0, 0)),
                      pl.BlockSpec((d, d), const),
                      pl.BlockSpec((1, d), const),
                      pl.BlockSpec((1, d), const),
                      pl.BlockSpec((d, LANES), const),
                      pl.BlockSpec((1, LANES), const)],
            out_specs=[pl.BlockSpec((tm, d), row), pl.BlockSpec((tm, d), row), pl.BlockSpec((tm, LANES), row)]),
        out_shape=[jax.ShapeDtypeStruct((n, d), F32), jax.ShapeDtypeStruct((n, d), F32),
                   jax.ShapeDtypeStruct((n, LANES), F32)],
        compiler_params=_params("arbitrary"),
        name="ln1_route",
    )(tile_seg, gated, x, mod, w_out, lg, lb, wr, br)


def _row_gather_start(src_hbm, dst, sem, idx, row):
    pltpu.make_async_copy(src_hbm.at[pl.ds(idx, 1), :], dst.at[pl.ds(row, 1), :], sem).start()


def _row_gather_wait(src_hbm, dst, sem, row):
    pltpu.make_async_copy(src_hbm.at[pl.ds(0, 1), :], dst.at[pl.ds(row, 1), :], sem).wait()


def _expert_kernel(blk_e_ref, n_used_ref, tok_ref, h2_hbm, wg_ref, wu_ref, wd_ref, o_ref,
                   xbuf, wg_bf, wu_bf, wd_bf, sem):
    i = pl.program_id(0)
    e = blk_e_ref[i]
    prev = blk_e_ref[jnp.maximum(i - 1, 0)]
    rows = xbuf.shape[0]

    @pl.when(jnp.logical_or(i == 0, e != prev))
    def _():
        wg_bf[...] = wg_ref[0].astype(BF16)
        wu_bf[...] = wu_ref[0].astype(BF16)
        wd_bf[...] = wd_ref[0].astype(BF16)

    @pl.when(i < n_used_ref[0])
    def _():
        def start(r, carry):
            _row_gather_start(h2_hbm, xbuf, sem.at[0], tok_ref[0, 0, r], r)
            return carry

        def wait(r, carry):
            _row_gather_wait(h2_hbm, xbuf, sem.at[0], r)
            return carry

        lax.fori_loop(0, rows, start, 0)
        lax.fori_loop(0, rows, wait, 0)
        x = xbuf[...].astype(BF16)
        gate = jnp.dot(x, wg_bf[...], preferred_element_type=F32)
        up = jnp.dot(x, wu_bf[...], preferred_element_type=F32)
        act = (gate * jax.nn.sigmoid(gate) * up).astype(BF16)
        o_ref[...] = jnp.dot(act, wd_bf[...], preferred_element_type=F32)

    @pl.when(i >= n_used_ref[0])
    def _():
        o_ref[...] = jnp.zeros_like(o_ref)


def _experts(h2, tok_blocks, blk_e, n_used, w_gate, w_up, w_down):
    n_blk = tok_blocks.shape[0]
    rows = tok_blocks.shape[2]
    d = h2.shape[1]
    de = w_gate.shape[2]
    return pl.pallas_call(
        _expert_kernel,
        grid_spec=pltpu.PrefetchScalarGridSpec(
            num_scalar_prefetch=2,
            grid=(n_blk,),
            in_specs=[pl.BlockSpec((1, 1, rows), lambda i, be, nu: (i, 0, 0), memory_space=pltpu.SMEM),
                      pl.BlockSpec(memory_space=pl.ANY),
                      pl.BlockSpec((1, d, de), lambda i, be, nu: (be[i], 0, 0)),
                      pl.BlockSpec((1, d, de), lambda i, be, nu: (be[i], 0, 0)),
                      pl.BlockSpec((1, de, d), lambda i, be, nu: (be[i], 0, 0))],
            out_specs=pl.BlockSpec((rows, d), lambda i, be, nu: (i, 0)),
            scratch_shapes=[pltpu.VMEM((rows, d), F32),
                            pltpu.VMEM((d, de), BF16),
                            pltpu.VMEM((d, de), BF16),
                            pltpu.VMEM((de, d), BF16),
                            pltpu.SemaphoreType.DMA((1,))]),
        out_shape=jax.ShapeDtypeStruct((n_blk * rows, d), F32),
        compiler_params=_params("arbitrary"),
        name="experts",
    )(blk_e, n_used, tok_blocks, h2, w_gate, w_up, w_down)


def _final_kernel(seg_ref, pos_ref, x1_ref, mod_ref, rt_ref, lg_ref, lb_ref, ys_hbm, o_ref, buf, sem):
    del seg_ref
    tm = x1_ref.shape[0]

    def start(r, carry):
        _row_gather_start(ys_hbm, buf.at[0], sem.at[0], pos_ref[0, 0, r], r)
        _row_gather_start(ys_hbm, buf.at[1], sem.at[1], pos_ref[0, 0, tm + r], r)
        return carry

    def wait(r, carry):
        _row_gather_wait(ys_hbm, buf.at[0], sem.at[0], r)
        _row_gather_wait(ys_hbm, buf.at[1], sem.at[1], r)
        return carry

    lax.fori_loop(0, tm, start, 0)
    lax.fori_loop(0, tm, wait, 0)
    rt = rt_ref[...]
    ffn = rt[:, 0:1] * buf[0] + rt[:, 1:2] * buf[1]
    g2 = mod_ref[0, 5:6, :]
    o_ref[...] = _layer_norm(DEEPNORM_ALPHA * x1_ref[...] + g2 * ffn, lg_ref[...], lb_ref[...])


def _final(x1, mod, tile_seg, pos_blocks, rt, lg, lb, ys, tile0, n_tiles, tm):
    d = x1.shape[1]
    row = lambda i, seg: (tile0 + i, 0)
    const = lambda i, seg: (0, 0)
    return pl.pallas_call(
        _final_kernel,
        grid_spec=pltpu.PrefetchScalarGridSpec(
            num_scalar_prefetch=1,
            grid=(n_tiles,),
            in_specs=[pl.BlockSpec((1, 1, 2 * tm), lambda i, seg: (tile0 + i, 0, 0), memory_space=pltpu.SMEM),
                      pl.BlockSpec((tm, d), row),
                      pl.BlockSpec((1, 8, d), lambda i, seg: (seg[tile0 + i], 0, 0)),
                      pl.BlockSpec((tm, LANES), row),
                      pl.BlockSpec((1, d), const),
                      pl.BlockSpec((1, d), const),
                      pl.BlockSpec(memory_space=pl.ANY)],
            out_specs=pl.BlockSpec((tm, d), lambda i, seg: (i, 0)),
            scratch_shapes=[pltpu.VMEM((2, tm, d), F32), pltpu.SemaphoreType.DMA((2,))]),
        out_shape=jax.ShapeDtypeStruct((n_tiles * tm, d), F32),
        compiler_params=_params("arbitrary"),
        name="combine_ln2",
    )(tile_seg, pos_blocks, x1, mod, rt, lg, lb, ys)


def _rope_tables(pos_a, pos_b):
    inv = ROPE_THETA ** (-jnp.arange(0, 64, 2, dtype=F32) / 64)

    def half(pos):
        if pos is None:
            z = jnp.zeros((pos_a.shape[0], 64), F32)
            return z, z, z
        ang = pos.astype(F32)[:, None] * inv[None, :]
        cos, sin = jnp.cos(ang), jnp.sin(ang)
        zero = jnp.zeros_like(sin)
        return (jnp.concatenate([cos, cos], 1), jnp.concatenate([-sin, zero], 1), jnp.concatenate([zero, sin], 1))

    a, b = half(pos_a), half(pos_b)
    return tuple(jnp.concatenate([u, v], 1) for u, v in zip(a, b))


def _dispatch(eid, rows):
    n = eid.shape[0]
    a = n * TOP_K
    flat = eid.reshape(-1)
    order = jnp.argsort(flat, stable=True).astype(jnp.int32)
    e_sorted = flat[order]
    counts = jnp.bincount(flat, length=N_EXPERTS).astype(jnp.int32)
    starts = jnp.cumsum(counts) - counts
    padded = ((counts + rows - 1) // rows) * rows
    pad_end = jnp.cumsum(padded)
    pad_start = pad_end - padded
    dest = pad_start[e_sorted] + (jnp.arange(a, dtype=jnp.int32) - starts[e_sorted])
    p_rows = ((a + rows - 1) // rows) * rows + N_EXPERTS * rows
    n_blk = p_rows // rows
    buf_tok = jnp.zeros((p_rows,), jnp.int32).at[dest].set(order // TOP_K)
    pos = jnp.zeros((a,), jnp.int32).at[order].set(dest).reshape(n, TOP_K)
    blk_start = jnp.arange(n_blk, dtype=jnp.int32) * rows
    blk_e = jnp.minimum(jnp.searchsorted(pad_end, blk_start, side="right"), N_EXPERTS - 1).astype(jnp.int32)
    n_used = (pad_end[-1] // rows).astype(jnp.int32).reshape(1)
    return buf_tok.reshape(n_blk, 1, rows), pos, blk_e, n_used


def _layer(groups, w_ada, b_ada, w_in, a_q_norm, a_kv_norm, a_w_uq, a_w_ukv, a_w_o, b_q_norm, b_k_norm, b_w_o,
           w_out, ln1_g, ln1_b, w_group, b_group, w_expert, b_expert, e_w_gate, e_w_up, e_w_down, ln2_g, ln2_b):
    d = w_in.shape[0]
    seqs = [x.shape[1] for x, _ in groups]
    seg = functools.reduce(math.gcd, seqs)
    tm = _tile(seg, 512)
    s_max = max(seqs)

    x_all = jnp.concatenate([x.reshape(-1, d) for x, _ in groups], axis=0)
    n = x_all.shape[0]
    c_all = jnp.concatenate([c for _, c in groups], axis=0)
    nb = c_all.shape[0]
    c_pad = jnp.zeros((-(-nb // 8) * 8, d), F32).at[:nb].set(c_all)
    seg_batch, tile_seg, tile_pos = [], [], []
    b0 = 0
    for x, _ in groups:
        bsz, s = x.shape[0], x.shape[1]
        for b in range(bsz):
            for t in range(s // tm):
                tile_seg.append(b0 + b)
                tile_pos.append(t)
        b0 += bsz
    tile_seg = jnp.asarray(np.asarray(tile_seg, np.int32))
    tile_pos = jnp.asarray(np.asarray(tile_pos, np.int32))

    mod = _ada(c_pad, w_ada, b_ada)[:nb].reshape(nb, 6, d)
    mod = jnp.concatenate([mod, jnp.zeros((nb, 2, d), F32)], axis=1)

    o_q, o_kv, o_kr, o_qb, o_kb, o_vb, o_ga, o_gb = np.cumsum(
        [0, A_Q_RANK, A_KV_RANK, A_ROPE, B_HEADS * B_HEAD_DIM, B_KV_HEADS * B_HEAD_DIM,
         B_KV_HEADS * B_HEAD_DIM, d]).tolist()
    w_lat = jnp.pad(w_in[:, :o_qb], ((0, 0), (0, o_kr + LANES - o_qb))).astype(BF16)
    w_qk = w_in[:, o_qb:o_vb].astype(BF16)
    w_v = w_in[:, o_vb:o_ga].astype(BF16)
    w_g = w_in[:, o_ga:].astype(BF16)
    w_uq = jnp.pad(a_w_uq.reshape(A_Q_RANK, A_HEADS, A_NOPE + A_ROPE),
                   ((0, 0), (0, 0), (0, A_QK_PAD - A_NOPE - A_ROPE))).reshape(A_Q_RANK, A_HEADS * A_QK_PAD).astype(BF16)
    w_ukv = a_w_ukv.reshape(A_KV_RANK, A_HEADS, A_NOPE + A_V)
    w_uk = w_ukv[:, :, :A_NOPE].reshape(A_KV_RANK, A_HEADS * A_NOPE).astype(BF16)
    w_uv = w_ukv[:, :, A_NOPE:].reshape(A_KV_RANK, A_HEADS * A_V).astype(BF16)
    w_r = jnp.zeros((d, LANES), F32).at[:, :N_GROUPS].set(w_group).at[:, N_GROUPS:N_GROUPS + N_EXPERTS].set(w_expert)
    b_r = jnp.zeros((1, LANES), F32).at[0, :N_GROUPS].set(b_group).at[0, N_GROUPS:N_GROUPS + N_EXPERTS].set(b_expert)

    t = jnp.arange(s_max, dtype=jnp.int32)
    tabs_a = _rope_tables(t, None)
    tabs_b = _rope_tables(t // GRID_W, t % GRID_W)

    h, lat = _mod_proj(x_all, mod, tile_seg, w_lat, tm)
    qk_b = _proj_qk(h, w_qk, jnp.stack([b_q_norm, b_k_norm]).reshape(2, 1, B_HEAD_DIM), tabs_b, tile_pos, tm,
                    B_HEADS // B_GROUP)
    v_b = _proj(h, w_v, tm, 512, "vt", BF16)
    gates = _proj(h, w_g, tm, 512, "gate", F32)
    q_a = _mla_q(lat, a_q_norm.reshape(1, -1), w_uq, tabs_a, tile_pos, tm)
    k_a, v_a = _mla_kv(lat, a_kv_norm.reshape(1, -1), w_uk, w_uv, tabs_a, tile_pos, tm)

    oa_parts, ob_parts = [], []
    row0 = 0
    for x, _ in groups:
        bsz, s = x.shape[0], x.shape[1]
        oa_parts.append(_flash(q_a, k_a, v_a, batch=bsz, seq=s, row0=row0, n_kv_heads=A_HEADS, group=1,
                               dq=A_QK_PAD, dv=A_V, k_col0=0, scale=(A_NOPE + A_ROPE) ** -0.5, tq=1024, tk=1024))
        ob_parts.append(_flash(qk_b, qk_b, v_b, batch=bsz, seq=s, row0=row0, n_kv_heads=B_KV_HEADS, group=B_GROUP,
                               dq=B_HEAD_DIM, dv=B_HEAD_DIM, k_col0=B_HEADS * B_HEAD_DIM, scale=B_HEAD_DIM ** -0.5,
                               tq=256, tk=1024))
        row0 += bsz * s
    o_a = jnp.concatenate(oa_parts, axis=0)
    o_b = jnp.concatenate(ob_parts, axis=0)

    gated = _gate_mix(o_a, o_b, a_w_o.astype(BF16), b_w_o.astype(BF16), gates, tm, 512)
    tm2 = _tile(seg, 256)
    seg2 = jnp.repeat(tile_seg, tm // tm2)
    x1, h2, rt = _ln1(gated, x_all, mod, seg2, w_out.astype(BF16), ln1_g.reshape(1, d), ln1_b.reshape(1, d),
                      w_r, b_r, tm2)

    eid = rt[:, 2:4].astype(jnp.int32)
    tok_blocks, pos, blk_e, n_used = _dispatch(eid, EXPERT_ROWS)
    ys = _experts(h2, tok_blocks, blk_e, n_used, e_w_gate, e_w_up, e_w_down)
    pos_blocks = pos.reshape(n // tm2, tm2, TOP_K).transpose(0, 2, 1).reshape(n // tm2, 1, TOP_K * tm2)

    outs = []
    tile0 = 0
    for x, _ in groups:
        bsz, s = x.shape[0], x.shape[1]
        nt = bsz * s // tm2
        y = _final(x1, mod, seg2, pos_blocks, rt, ln2_g.reshape(1, d), ln2_b.reshape(1, d), ys, tile0, nt, tm2)
        outs.append(y.reshape(bsz, s, d))
        tile0 += nt
    return tuple(outs)


def kernel(x_prompt, x_sample, c_prompt, c_sample, w_ada, b_ada, w_in, a_q_norm, a_kv_norm, a_w_uq, a_w_ukv, a_w_o,
           b_q_norm, b_k_norm, b_w_o, w_out, ln1_g, ln1_b, w_group, b_group, w_expert, b_expert, e_w_gate, e_w_up,
           e_w_down, ln2_g, ln2_b):
    weights = (w_ada, b_ada, w_in, a_q_norm, a_kv_norm, a_w_uq, a_w_ukv, a_w_o, b_q_norm, b_k_norm, b_w_o, w_out,
               ln1_g, ln1_b, w_group, b_group, w_expert, b_expert, e_w_gate, e_w_up, e_w_down, ln2_g, ln2_b)
    return _layer(((x_prompt, c_prompt), (x_sample, c_sample)), *[w[0] for w in weights])
```

```python
import functools
import math

import numpy as np
import jax
import jax.numpy as jnp
from jax import lax
from jax.experimental import pallas as pl
from jax.experimental.pallas import tpu as pltpu

F32 = jnp.float32
BF16 = jnp.bfloat16
HIGHEST = lax.Precision.HIGHEST

GRID_W = 64
ROPE_THETA = 10000.0
A_HEADS = 16
A_Q_RANK = 512
A_KV_RANK = 512
A_NOPE = 128
A_ROPE = 64
A_V = 128
A_QK_PAD = 256
B_HEADS = 16
B_KV_HEADS = 4
B_GROUP = B_HEADS // B_KV_HEADS
B_HEAD_DIM = 128
N_GROUPS = 8
EXPERTS_PER_GROUP = 8
N_EXPERTS = N_GROUPS * EXPERTS_PER_GROUP
TOP_K = 2
RMS_EPS = 1e-6
LN_EPS = 1e-5
DEPTH = 1
DEEPNORM_ALPHA = (2 * DEPTH) ** 0.25
LOG2E = 1.4426950408889634

LANES = 128
EXPERT_ROWS = 256
SUB_ROWS = 128
GATHER_UNROLL = 8
ONES_ROWS = 16
VMEM_LIMIT = 48 * 1024 * 1024


def _params(*sem):
    return pltpu.CompilerParams(dimension_semantics=sem, vmem_limit_bytes=VMEM_LIMIT)


def _tile(n, pref):
    t = min(n, pref)
    while n % t:
        t //= 2
    return t


def _sub_blocks(rows, sub):
    sub = _tile(rows, sub)
    return [slice(r, r + sub) for r in range(0, rows, sub)]


def _ada_kernel(c_ref, w_ref, b_ref, o_ref):
    c = c_ref[...]
    a = c * jax.nn.sigmoid(c)
    o_ref[...] = jnp.dot(a, w_ref[...], precision=HIGHEST, preferred_element_type=F32) + b_ref[...]


def _ada(c_pad, w_ada, b_ada):
    d, n = w_ada.shape
    tn = _tile(n, 1024)
    return pl.pallas_call(
        _ada_kernel,
        grid=(n // tn,),
        in_specs=[pl.BlockSpec(c_pad.shape, lambda j: (0, 0)),
                  pl.BlockSpec((d, tn), lambda j: (0, j)),
                  pl.BlockSpec((1, tn), lambda j: (0, j))],
        out_specs=pl.BlockSpec((c_pad.shape[0], tn), lambda j: (0, j)),
        out_shape=jax.ShapeDtypeStruct((c_pad.shape[0], n), F32),
        compiler_params=_params("arbitrary"),
        name="ada",
    )(c_pad, w_ada, b_ada.reshape(1, n))


def _mod_proj_kernel(seg_ref, x_ref, mod_ref, w_ref, h_ref, o_ref):
    del seg_ref
    sh = mod_ref[0, 0:1, :]
    sc = mod_ref[0, 1:2, :]
    h = (x_ref[...] * (1.0 + sc) + sh).astype(BF16)
    h_ref[...] = h
    o_ref[...] = jnp.dot(h, w_ref[...], preferred_element_type=F32)


def _mod_proj(x, mod, tile_seg, w, tm):
    n, d = x.shape
    nc = w.shape[1]
    return pl.pallas_call(
        _mod_proj_kernel,
        grid_spec=pltpu.PrefetchScalarGridSpec(
            num_scalar_prefetch=1,
            grid=(n // tm,),
            in_specs=[pl.BlockSpec((tm, d), lambda i, seg: (i, 0)),
                      pl.BlockSpec((1, 8, d), lambda i, seg: (seg[i], 0, 0)),
                      pl.BlockSpec((d, nc), lambda i, seg: (0, 0))],
            out_specs=[pl.BlockSpec((tm, d), lambda i, seg: (i, 0)),
                       pl.BlockSpec((tm, nc), lambda i, seg: (i, 0))]),
        out_shape=[jax.ShapeDtypeStruct((n, d), BF16), jax.ShapeDtypeStruct((n, nc), F32)],
        compiler_params=_params("arbitrary"),
        name="mod_proj",
    )(tile_seg, x, mod, w)


def _rope(x, cos, sin_a, sin_b):
    return x * cos + pltpu.roll(x, 96, 1) * sin_a + pltpu.roll(x, 32, 1) * sin_b


def _store_vt(vt_ref, v, dv):
    vt = v.T.astype(vt_ref.dtype)
    dve = dv + ONES_ROWS
    for hh in range(v.shape[1] // dv):
        vt_ref[hh * dve:hh * dve + dv, :] = vt[hh * dv:(hh + 1) * dv, :]
        vt_ref[hh * dve + dv:(hh + 1) * dve, :] = jnp.ones((ONES_ROWS, vt.shape[1]), vt_ref.dtype)


def _proj_vt_kernel(h_ref, w_ref, o_ref):
    _store_vt(o_ref, jnp.dot(h_ref[...], w_ref[...], preferred_element_type=F32), B_HEAD_DIM)


def _proj_qk_kernel(pos_ref, h_ref, w_ref, g_ref, cos_ref, sa_ref, sb_ref, o_ref):
    del pos_ref
    g = g_ref[0]
    for rs in _sub_blocks(h_ref.shape[0], SUB_ROWS):
        acc = jnp.dot(h_ref[rs, :], w_ref[...], preferred_element_type=F32)
        cos, sa, sb = cos_ref[rs, :], sa_ref[rs, :], sb_ref[rs, :]
        for hh in range(acc.shape[1] // B_HEAD_DIM):
            x = acc[:, hh * B_HEAD_DIM:(hh + 1) * B_HEAD_DIM]
            r = lax.rsqrt(jnp.mean(x * x, axis=-1, keepdims=True) + RMS_EPS)
            y = x * r * g
            o_ref[rs, hh * B_HEAD_DIM:(hh + 1) * B_HEAD_DIM] = _rope(y, cos, sa, sb).astype(o_ref.dtype)


def _proj_vt(h, w, tm, tn):
    n, d = h.shape
    nc = w.shape[1]
    tne = tn // B_HEAD_DIM * (B_HEAD_DIM + ONES_ROWS)
    return pl.pallas_call(
        _proj_vt_kernel,
        grid=(n // tm, nc // tn),
        in_specs=[pl.BlockSpec((tm, d), lambda i, j: (i, 0)),
                  pl.BlockSpec((d, tn), lambda i, j: (0, j))],
        out_specs=pl.BlockSpec((tne, tm), lambda i, j: (j, i)),
        out_shape=jax.ShapeDtypeStruct((nc // tn * tne, n), BF16),
        compiler_params=_params("arbitrary", "arbitrary"),
        name="proj_vt",
    )(h, w)


def _proj_qk(h, w, gains, tabs, tile_pos, tm, n_q_tiles):
    n, d = h.shape
    nc = w.shape[1]
    tn = B_GROUP * B_HEAD_DIM
    tab_spec = pl.BlockSpec((tm, LANES), lambda i, j, pos: (pos[i], 0))
    return pl.pallas_call(
        _proj_qk_kernel,
        grid_spec=pltpu.PrefetchScalarGridSpec(
            num_scalar_prefetch=1,
            grid=(n // tm, nc // tn),
            in_specs=[pl.BlockSpec((tm, d), lambda i, j, pos: (i, 0)),
                      pl.BlockSpec((d, tn), lambda i, j, pos: (0, j)),
                      pl.BlockSpec((1, 1, LANES), lambda i, j, pos: (jnp.where(j < n_q_tiles, 0, 1), 0, 0)),
                      tab_spec, tab_spec, tab_spec],
            out_specs=pl.BlockSpec((tm, tn), lambda i, j, pos: (i, j))),
        out_shape=jax.ShapeDtypeStruct((n, nc), BF16),
        compiler_params=_params("arbitrary", "arbitrary"),
        name="proj_qk",
    )(tile_pos, h, w, gains, *tabs)


def _rms(x, g):
    return x * lax.rsqrt(jnp.mean(x * x, axis=-1, keepdims=True) + RMS_EPS) * g


def _mla_q_kernel(pos_ref, c_ref, g_ref, w_ref, cos_ref, sa_ref, sb_ref, o_ref):
    del pos_ref
    for rs in _sub_blocks(c_ref.shape[0], SUB_ROWS):
        xn = _rms(c_ref[rs, :], g_ref[...]).astype(BF16)
        acc = jnp.dot(xn, w_ref[...], preferred_element_type=F32)
        cos, sa, sb = cos_ref[rs, :], sa_ref[rs, :], sb_ref[rs, :]
        for hh in range(acc.shape[1] // A_QK_PAD):
            base = hh * A_QK_PAD
            o_ref[rs, base:base + A_NOPE] = acc[:, base:base + A_NOPE].astype(o_ref.dtype)
            o_ref[rs, base + A_NOPE:base + A_QK_PAD] = _rope(
                acc[:, base + A_NOPE:base + A_QK_PAD], cos, sa, sb).astype(o_ref.dtype)


def _mla_q(cqkv, gain, w, tabs, tile_pos, tm):
    n = cqkv.shape[0]
    nc = w.shape[1]
    tn = 4 * A_QK_PAD
    tab_spec = pl.BlockSpec((tm, LANES), lambda i, j, pos: (pos[i], 0))
    return pl.pallas_call(
        _mla_q_kernel,
        grid_spec=pltpu.PrefetchScalarGridSpec(
            num_scalar_prefetch=1,
            grid=(n // tm, nc // tn),
            in_specs=[pl.BlockSpec((tm, A_Q_RANK), lambda i, j, pos: (i, 0)),
                      pl.BlockSpec((1, A_Q_RANK), lambda i, j, pos: (0, 0)),
                      pl.BlockSpec((A_Q_RANK, tn), lambda i, j, pos: (0, j)),
                      tab_spec, tab_spec, tab_spec],
            out_specs=pl.BlockSpec((tm, tn), lambda i, j, pos: (i, j))),
        out_shape=jax.ShapeDtypeStruct((n, nc), BF16),
        compiler_params=_params("arbitrary", "arbitrary"),
        name="mla_q",
    )(tile_pos, cqkv, gain, w, *tabs)


def _mla_kv_kernel(pos_ref, c_ref, kr_ref, g_ref, wk_ref, wv_ref, cos_ref, sa_ref, sb_ref, k_ref, v_ref):
    del pos_ref
    xn = _rms(c_ref[...], g_ref[...]).astype(BF16)
    kn = jnp.dot(xn, wk_ref[...], preferred_element_type=F32)
    _store_vt(v_ref, jnp.dot(xn, wv_ref[...], preferred_element_type=F32), A_V)
    kr = _rope(kr_ref[...], cos_ref[...], sa_ref[...], sb_ref[...]).astype(k_ref.dtype)
    for hh in range(kn.shape[1] // A_NOPE):
        k_ref[:, hh * A_QK_PAD:hh * A_QK_PAD + A_NOPE] = kn[:, hh * A_NOPE:(hh + 1) * A_NOPE].astype(k_ref.dtype)
        k_ref[:, hh * A_QK_PAD + A_NOPE:(hh + 1) * A_QK_PAD] = kr


def _mla_kv(cqkv, gain, wk, wv, tabs, tile_pos, tm):
    n = cqkv.shape[0]
    hpt = 4
    n_ct = A_HEADS // hpt
    kr_blk = (A_Q_RANK + A_KV_RANK) // LANES
    tab_spec = pl.BlockSpec((tm, LANES), lambda i, j, pos: (pos[i], 0))
    return pl.pallas_call(
        _mla_kv_kernel,
        grid_spec=pltpu.PrefetchScalarGridSpec(
            num_scalar_prefetch=1,
            grid=(n // tm, n_ct),
            in_specs=[pl.BlockSpec((tm, A_KV_RANK), lambda i, j, pos: (i, 1)),
                      pl.BlockSpec((tm, LANES), lambda i, j, pos: (i, kr_blk)),
                      pl.BlockSpec((1, A_KV_RANK), lambda i, j, pos: (0, 0)),
                      pl.BlockSpec((A_KV_RANK, hpt * A_NOPE), lambda i, j, pos: (0, j)),
                      pl.BlockSpec((A_KV_RANK, hpt * A_V), lambda i, j, pos: (0, j)),
                      tab_spec, tab_spec, tab_spec],
            out_specs=[pl.BlockSpec((tm, hpt * A_QK_PAD), lambda i, j, pos: (i, j)),
                       pl.BlockSpec((hpt * (A_V + ONES_ROWS), tm), lambda i, j, pos: (j, i))]),
        out_shape=[jax.ShapeDtypeStruct((n, A_HEADS * A_QK_PAD), BF16),
                   jax.ShapeDtypeStruct((A_HEADS * (A_V + ONES_ROWS), n), BF16)],
        compiler_params=_params("arbitrary", "arbitrary"),
        name="mla_kv",
    )(tile_pos, cqkv, cqkv, gain, wk, wv, *tabs)


def _flash_kernel(q_ref, k_ref, vt_ref, o_ref, q_sc, m_sc, acc_sc, s_sc, *, group, dq, dv, tk, c):
    tq = q_ref.shape[0]
    n_kv = k_ref.shape[0] // tk
    for g in range(group):
        q_sc[g * tq:(g + 1) * tq, :] = (q_ref[:, g * dq:(g + 1) * dq].astype(F32) * c).astype(BF16)
    m_sc[...] = jnp.full(m_sc.shape, -1e30, F32)
    acc_sc[...] = jnp.zeros(acc_sc.shape, F32)

    def scores(j):
        k = k_ref[pl.ds(pl.multiple_of(j * tk, tk), tk), :]
        return lax.dot_general(k, q_sc[...], (((1,), (1,)), ((), ())), preferred_element_type=F32)

    def step(j, slot, prefetch):
        s = s_sc[slot]
        if prefetch:
            s_sc[1 - slot] = scores(j + 1)
        vt = vt_ref[:, pl.ds(pl.multiple_of(j * tk, tk), tk)]
        m_prev = m_sc[...]
        m_new = jnp.maximum(m_prev, jnp.max(s, axis=0, keepdims=True))
        alpha = jnp.exp2(m_prev - m_new)
        p = jnp.exp2(s - m_new).astype(BF16)
        acc_sc[...] = alpha * acc_sc[...] + jnp.dot(vt, p, preferred_element_type=F32)
        m_sc[...] = m_new

    s_sc[0] = scores(0)

    def pair(jj, carry):
        step(2 * jj, 0, True)
        step(2 * jj + 1, 1, True)
        return carry

    lax.fori_loop(0, n_kv // 2 - 1, pair, 0)
    step(n_kv - 2, 0, True)
    step(n_kv - 1, 1, False)
    acc = acc_sc[...]
    out = (acc[:dv] / acc[dv:dv + 1]).T
    for g in range(group):
        o_ref[:, g * dv:(g + 1) * dv] = out[g * tq:(g + 1) * tq, :].astype(o_ref.dtype)


def _flash(q2d, k2d, vt2d, *, batch, seq, row0, n_kv_heads, group, dq, dv, k_col0, scale, tq, tk):
    assert row0 % seq == 0 and k_col0 % dq == 0
    tq = _tile(seq, tq)
    tk = _tile(seq // 2, tk)
    qb0, kb0 = row0 // tq, row0 // seq
    nq = seq // tq
    kc0 = k_col0 // dq
    rows = group * tq
    dve = dv + ONES_ROWS
    kern = functools.partial(_flash_kernel, group=group, dq=dq, dv=dv, tk=tk, c=scale * LOG2E)
    return pl.pallas_call(
        kern,
        grid=(batch, n_kv_heads, nq),
        in_specs=[pl.BlockSpec((tq, group * dq), lambda b, h, i: (qb0 + b * nq + i, h)),
                  pl.BlockSpec((seq, dq), lambda b, h, i: (kb0 + b, kc0 + h)),
                  pl.BlockSpec((dve, seq), lambda b, h, i: (h, kb0 + b))],
        out_specs=pl.BlockSpec((tq, group * dv), lambda b, h, i: (b * nq + i, h)),
        out_shape=jax.ShapeDtypeStruct((batch * seq, n_kv_heads * group * dv), BF16),
        scratch_shapes=[pltpu.VMEM((rows, dq), BF16), pltpu.VMEM((1, rows), F32), pltpu.VMEM((dve, rows), F32),
                        pltpu.VMEM((2, tk, rows), F32)],
        compiler_params=_params("arbitrary", "arbitrary", "arbitrary"),
        name="flash",
    )(q2d, k2d, vt2d)


def _gate_mix_kernel(h_ref, oa_ref, ob_ref, wga_ref, wgb_ref, wa_ref, wb_ref, o_ref):
    for rs in _sub_blocks(h_ref.shape[0], 2 * SUB_ROWS):
        h = h_ref[rs, :]
        ga = jax.nn.sigmoid(jnp.dot(h, wga_ref[...], preferred_element_type=F32))
        gb = jax.nn.sigmoid(jnp.dot(h, wgb_ref[...], preferred_element_type=F32))
        ya = jnp.dot(oa_ref[rs, :], wa_ref[...], preferred_element_type=F32)
        yb = jnp.dot(ob_ref[rs, :], wb_ref[...], preferred_element_type=F32)
        o_ref[rs, :] = (ga * ya + gb * yb).astype(o_ref.dtype)


def _gate_mix(h, oa, ob, w_gates, wa, wb, tm, tn):
    n, d = oa.shape
    dm = wa.shape[1]
    nct = dm // tn
    rows = pl.BlockSpec((tm, d), lambda i, j: (i, 0))
    cols = pl.BlockSpec((d, tn), lambda i, j: (0, j))
    return pl.pallas_call(
        _gate_mix_kernel,
        grid=(n // tm, nct),
        in_specs=[rows, rows, rows, cols, pl.BlockSpec((d, tn), lambda i, j: (0, nct + j)), cols, cols],
        out_specs=pl.BlockSpec((tm, tn), lambda i, j: (i, j)),
        out_shape=jax.ShapeDtypeStruct((n, dm), BF16),
        compiler_params=_params("arbitrary", "arbitrary"),
        name="gate_mix",
    )(h, oa, ob, w_gates, w_gates, wa, wb)


def _layer_norm(y, g, b):
    mu = jnp.mean(y, axis=-1, keepdims=True)
    yc = y - mu
    var = jnp.mean(yc * yc, axis=-1, keepdims=True)
    return yc * lax.rsqrt(var + LN_EPS) * g + b


def _route(logits):
    lane = lax.broadcasted_iota(jnp.int32, logits.shape, 1).astype(F32)
    big = float(LANES)
    ninf = -jnp.inf
    gmask = lane < N_GROUPS
    gl = jnp.where(gmask, logits, ninf)
    gmax = jnp.max(gl, axis=-1, keepdims=True)
    grp = jnp.min(jnp.where(gl == gmax, lane, big), axis=-1, keepdims=True)
    den = jnp.sum(jnp.exp(gl - gmax), axis=-1, keepdims=True)
    p_grp = 1.0 / den
    lo = N_GROUPS + grp * EXPERTS_PER_GROUP
    el = jnp.where(lane >= lo, jnp.where(lane < lo + EXPERTS_PER_GROUP, logits, ninf), ninf)
    t1 = jnp.max(el, axis=-1, keepdims=True)
    i1 = jnp.min(jnp.where(el == t1, lane, big), axis=-1, keepdims=True)
    el2 = jnp.where(lane == i1, ninf, el)
    t2 = jnp.max(el2, axis=-1, keepdims=True)
    i2 = jnp.min(jnp.where(el2 == t2, lane, big), axis=-1, keepdims=True)
    e = jnp.exp(t2 - t1)
    w1 = p_grp / (1.0 + e)
    w2 = p_grp * e / (1.0 + e)
    out = jnp.where(lane == 0, w1, jnp.where(lane == 1, w2, jnp.where(lane == 2, i1 - N_GROUPS, i2 - N_GROUPS)))
    return jnp.where(lane < 4, out, 0.0)


def _ln1_kernel(seg_ref, g_ref, x_ref, mod_ref, w_ref, lg_ref, lb_ref, wr_ref, br_ref, x1_ref, h2_ref, rt_ref):
    del seg_ref
    g1 = mod_ref[0, 2:3, :]
    sh2 = mod_ref[0, 3:4, :]
    sc2 = mod_ref[0, 4:5, :]
    for rs in _sub_blocks(x_ref.shape[0], SUB_ROWS):
        attn = jnp.dot(g_ref[rs, :], w_ref[...], preferred_element_type=F32)
        x1 = _layer_norm(DEEPNORM_ALPHA * x_ref[rs, :] + g1 * attn, lg_ref[...], lb_ref[...])
        x1_ref[rs, :] = x1
        h2 = x1 * (1.0 + sc2) + sh2
        h2_ref[rs, :] = h2
        h_hi = h2.astype(BF16)
        h_lo = (h2 - h_hi.astype(F32)).astype(BF16)
        a = jnp.dot(h_hi, wr_ref[...], preferred_element_type=F32)
        b = jnp.dot(h_lo, wr_ref[:, :LANES], preferred_element_type=F32)
        rt_ref[rs, :] = _route(a[:, :LANES] + a[:, LANES:] + b + br_ref[...])


def _ln1(gated, x, mod, tile_seg, w_out, lg, lb, wr, br, tm):
    n, d = x.shape
    row = lambda i, seg: (i, 0)
    const = lambda i, seg: (0, 0)
    return pl.pallas_call(
        _ln1_kernel,
        grid_spec=pltpu.PrefetchScalarGridSpec(
            num_scalar_prefetch=1,
            grid=(n // tm,),
            in_specs=[pl.BlockSpec((tm, d), row),
                      pl.BlockSpec((tm, d), row),
                      pl.BlockSpec((1, 8, d), lambda i, seg: (seg[i], 0, 0)),
                      pl.BlockSpec((d, d), const),
                      pl.BlockSpec((1, d), const),
                      pl.BlockSpec((1, d), const),
                      pl.BlockSpec((d, 2 * LANES), const),
                      pl.BlockSpec((1, LANES), const)],
            out_specs=[pl.BlockSpec((tm, d), row), pl.BlockSpec((tm, d), row), pl.BlockSpec((tm, LANES), row)]),
        out_shape=[jax.ShapeDtypeStruct((n, d), F32), jax.ShapeDtypeStruct((n, d), F32),
                   jax.ShapeDtypeStruct((n, LANES), F32)],
        compiler_params=_params("arbitrary"),
        name="ln1_route",
    )(tile_seg, gated, x, mod, w_out, lg, lb, wr, br)


def _row_gather_start(src_hbm, dst, sem, idx, row):
    pltpu.make_async_copy(src_hbm.at[pl.ds(idx, 1), :], dst.at[pl.ds(row, 1), :], sem).start()


def _row_gather_wait(src_hbm, dst, sem, row):
    pltpu.make_async_copy(src_hbm.at[pl.ds(0, 1), :], dst.at[pl.ds(row, 1), :], sem).wait()


def _gather_rows_start(src_hbm, dst, sem, idx_ref, idx0, rows):
    def body(r, carry):
        _row_gather_start(src_hbm, dst, sem, idx_ref[0, 0, idx0 + r], r)
        return carry

    lax.fori_loop(0, rows, body, 0, unroll=GATHER_UNROLL)


def _gather_rows_wait(src_hbm, dst, sem, rows):
    def body(r, carry):
        _row_gather_wait(src_hbm, dst, sem, r)
        return carry

    lax.fori_loop(0, rows, body, 0, unroll=GATHER_UNROLL)


def _expert_kernel(blk_e_ref, n_used_ref, tok_ref, tok_next_ref, h2_hbm, wg_ref, wu_ref, wd_ref, o_ref,
                   xbuf, wg_bf, wu_bf, wd_bf, sem):
    i = pl.program_id(0)
    n_used = n_used_ref[0]
    e = blk_e_ref[i]
    prev = blk_e_ref[jnp.maximum(i - 1, 0)]
    rows = xbuf.shape[1]
    slot = i % 2

    @pl.when(i == 0)
    def _():
        _gather_rows_start(h2_hbm, xbuf.at[0], sem.at[0], tok_ref, 0, rows)

    @pl.when(i + 1 < n_used)
    def _():
        _gather_rows_start(h2_hbm, xbuf.at[1 - slot], sem.at[1 - slot], tok_next_ref, 0, rows)

    @pl.when(jnp.logical_or(i == 0, e != prev))
    def _():
        wg_bf[...] = wg_ref[0].astype(BF16)
        wu_bf[...] = wu_ref[0].astype(BF16)
        wd_bf[...] = wd_ref[0].astype(BF16)

    @pl.when(i < n_used)
    def _():
        _gather_rows_wait(h2_hbm, xbuf.at[slot], sem.at[slot], rows)
        for rs in _sub_blocks(rows, SUB_ROWS):
            x = xbuf[slot, rs, :].astype(BF16)
            gate = jnp.dot(x, wg_bf[...], preferred_element_type=F32)
            up = jnp.dot(x, wu_bf[...], preferred_element_type=F32)
            act = (gate * jax.nn.sigmoid(gate) * up).astype(BF16)
            o_ref[rs, :] = jnp.dot(act, wd_bf[...], preferred_element_type=F32)

    @pl.when(i >= n_used)
    def _():
        o_ref[...] = jnp.zeros_like(o_ref)


def _experts(h2, tok_blocks, blk_e, n_used, w_gate, w_up, w_down):
    n_blk = tok_blocks.shape[0]
    rows = tok_blocks.shape[2]
    d = h2.shape[1]
    de = w_gate.shape[2]
    return pl.pallas_call(
        _expert_kernel,
        grid_spec=pltpu.PrefetchScalarGridSpec(
            num_scalar_prefetch=2,
            grid=(n_blk,),
            in_specs=[pl.BlockSpec((1, 1, rows), lambda i, be, nu: (i, 0, 0), memory_space=pltpu.SMEM),
                      pl.BlockSpec((1, 1, rows), lambda i, be, nu: (jnp.minimum(i + 1, n_blk - 1), 0, 0),
                                   memory_space=pltpu.SMEM),
                      pl.BlockSpec(memory_space=pl.ANY),
                      pl.BlockSpec((1, d, de), lambda i, be, nu: (be[i], 0, 0)),
                      pl.BlockSpec((1, d, de), lambda i, be, nu: (be[i], 0, 0)),
                      pl.BlockSpec((1, de, d), lambda i, be, nu: (be[i], 0, 0))],
            out_specs=pl.BlockSpec((rows, d), lambda i, be, nu: (i, 0)),
            scratch_shapes=[pltpu.VMEM((2, rows, d), F32),
                            pltpu.VMEM((d, de), BF16),
                            pltpu.VMEM((d, de), BF16),
                            pltpu.VMEM((de, d), BF16),
                            pltpu.SemaphoreType.DMA((2,))]),
        out_shape=jax.ShapeDtypeStruct((n_blk * rows, d), F32),
        compiler_params=_params("arbitrary"),
        name="experts",
    )(blk_e, n_used, tok_blocks, tok_blocks, h2, w_gate, w_up, w_down)


def _final_kernel(seg_ref, pos_ref, pos_next_ref, x1_ref, mod_ref, rt_ref, lg_ref, lb_ref, ys_hbm, o_ref, buf, sem):
    del seg_ref
    i = pl.program_id(0)
    tm = x1_ref.shape[0]
    slot = i % 2

    def start_tile(idx_ref, s):
        for k in range(TOP_K):
            _gather_rows_start(ys_hbm, buf.at[s, k], sem.at[s], idx_ref, k * tm, tm)

    @pl.when(i == 0)
    def _():
        start_tile(pos_ref, 0)

    @pl.when(i + 1 < pl.num_programs(0))
    def _():
        start_tile(pos_next_ref, 1 - slot)

    for k in range(TOP_K):
        _gather_rows_wait(ys_hbm, buf.at[slot, k], sem.at[slot], tm)
    g2 = mod_ref[0, 5:6, :]
    for rs in _sub_blocks(tm, SUB_ROWS):
        rt = rt_ref[rs, :]
        ffn = rt[:, 0:1] * buf[slot, 0, rs, :] + rt[:, 1:2] * buf[slot, 1, rs, :]
        o_ref[rs, :] = _layer_norm(DEEPNORM_ALPHA * x1_ref[rs, :] + g2 * ffn, lg_ref[...], lb_ref[...])


def _final(x1, mod, tile_seg, pos_blocks, rt, lg, lb, ys, tile0, n_tiles, tm):
    d = x1.shape[1]
    row = lambda i, seg: (tile0 + i, 0)
    const = lambda i, seg: (0, 0)
    return pl.pallas_call(
        _final_kernel,
        grid_spec=pltpu.PrefetchScalarGridSpec(
            num_scalar_prefetch=1,
            grid=(n_tiles,),
            in_specs=[pl.BlockSpec((1, 1, TOP_K * tm), lambda i, seg: (tile0 + i, 0, 0), memory_space=pltpu.SMEM),
                      pl.BlockSpec((1, 1, TOP_K * tm),
                                   lambda i, seg: (tile0 + jnp.minimum(i + 1, n_tiles - 1), 0, 0),
                                   memory_space=pltpu.SMEM),
                      pl.BlockSpec((tm, d), row),
                      pl.BlockSpec((1, 8, d), lambda i, seg: (seg[tile0 + i], 0, 0)),
                      pl.BlockSpec((tm, LANES), row),
                      pl.BlockSpec((1, d), const),
                      pl.BlockSpec((1, d), const),
                      pl.BlockSpec(memory_space=pl.ANY)],
            out_specs=pl.BlockSpec((tm, d), lambda i, seg: (i, 0)),
            scratch_shapes=[pltpu.VMEM((2, TOP_K, tm, d), F32), pltpu.SemaphoreType.DMA((2,))]),
        out_shape=jax.ShapeDtypeStruct((n_tiles * tm, d), F32),
        compiler_params=_params("arbitrary"),
        name="combine_ln2",
    )(tile_seg, pos_blocks, pos_blocks, x1, mod, rt, lg, lb, ys)


def _rope_tables(pos_a, pos_b):
    inv = ROPE_THETA ** (-jnp.arange(0, 64, 2, dtype=F32) / 64)

    def half(pos):
        if pos is None:
            z = jnp.zeros((pos_a.shape[0], 64), F32)
            return z, z, z
        ang = pos.astype(F32)[:, None] * inv[None, :]
        cos, sin = jnp.cos(ang), jnp.sin(ang)
        zero = jnp.zeros_like(sin)
        return (jnp.concatenate([cos, cos], 1), jnp.concatenate([-sin, zero], 1), jnp.concatenate([zero, sin], 1))

    a, b = half(pos_a), half(pos_b)
    return tuple(jnp.concatenate([u, v], 1) for u, v in zip(a, b))


def _dispatch(eid, rows):
    n = eid.shape[0]
    a = n * TOP_K
    i32 = jnp.int32
    iota = jnp.arange(a, dtype=i32)
    experts = jnp.arange(N_EXPERTS, dtype=i32)
    e_sorted, order = lax.sort((eid.reshape(-1), iota), num_keys=1, is_stable=True)
    starts = jnp.sum((e_sorted[None, :] < experts[:, None]).astype(i32), axis=1)
    counts = jnp.diff(starts, append=a)
    padded = ((counts + rows - 1) // rows) * rows
    pad_end = jnp.cumsum(padded)
    pad_start = pad_end - padded
    gap_step = jnp.diff(pad_start - starts, prepend=0)
    dest = iota + jnp.sum(jnp.where(e_sorted[:, None] >= experts[None, :], gap_step[None, :], 0), axis=1)
    _, pos = lax.sort((order, dest), num_keys=1)
    p_rows = ((a + rows - 1) // rows) * rows + N_EXPERTS * rows
    n_blk = p_rows // rows
    blk = jnp.arange(n_blk, dtype=i32)
    blk_e = jnp.minimum(jnp.sum((blk[:, None] * rows >= pad_end[None, :]).astype(i32), axis=1), N_EXPERTS - 1)
    onehot = blk_e[:, None] == experts[None, :]
    src0 = jnp.sum(jnp.where(onehot, (starts - pad_start)[None, :], 0), axis=1) + blk * rows
    src0 = jnp.minimum(src0, a)
    order_ext = jnp.concatenate([order, jnp.zeros((rows,), i32)])
    tok = jax.vmap(lambda s: lax.dynamic_slice(order_ext, (s,), (rows,)))(src0) // TOP_K
    n_used = (pad_end[-1] // rows).astype(i32).reshape(1)
    return tok.reshape(n_blk, 1, rows), pos.reshape(n, TOP_K), blk_e.astype(i32), n_used


def _layer(groups, w_ada, b_ada, w_in, a_q_norm, a_kv_norm, a_w_uq, a_w_ukv, a_w_o, b_q_norm, b_k_norm, b_w_o,
           w_out, ln1_g, ln1_b, w_group, b_group, w_expert, b_expert, e_w_gate, e_w_up, e_w_down, ln2_g, ln2_b):
    d = w_in.shape[0]
    seqs = [x.shape[1] for x, _ in groups]
    seg = functools.reduce(math.gcd, seqs)
    tm = _tile(seg, 512)
    s_max = max(seqs)

    x_all = jnp.concatenate([x.reshape(-1, d) for x, _ in groups], axis=0)
    n = x_all.shape[0]
    c_all = jnp.concatenate([c for _, c in groups], axis=0)
    nb = c_all.shape[0]
    c_pad = jnp.zeros((-(-nb // 8) * 8, d), F32).at[:nb].set(c_all)
    seg_batch, tile_seg, tile_pos = [], [], []
    b0 = 0
    for x, _ in groups:
        bsz, s = x.shape[0], x.shape[1]
        for b in range(bsz):
            for t in range(s // tm):
                tile_seg.append(b0 + b)
                tile_pos.append(t)
        b0 += bsz
    tile_seg = jnp.asarray(np.asarray(tile_seg, np.int32))
    tile_pos = jnp.asarray(np.asarray(tile_pos, np.int32))

    mod = _ada(c_pad, w_ada, b_ada)[:nb].reshape(nb, 6, d)
    mod = jnp.concatenate([mod, jnp.zeros((nb, 2, d), F32)], axis=1)

    o_q, o_kv, o_kr, o_qb, o_kb, o_vb, o_ga, o_gb = np.cumsum(
        [0, A_Q_RANK, A_KV_RANK, A_ROPE, B_HEADS * B_HEAD_DIM, B_KV_HEADS * B_HEAD_DIM,
         B_KV_HEADS * B_HEAD_DIM, d]).tolist()
    w_lat = jnp.pad(w_in[:, :o_qb], ((0, 0), (0, o_kr + LANES - o_qb))).astype(BF16)
    w_qk = w_in[:, o_qb:o_vb].astype(BF16)
    w_v = w_in[:, o_vb:o_ga].astype(BF16)
    w_g = w_in[:, o_ga:].astype(BF16)
    w_uq = jnp.pad(a_w_uq.reshape(A_Q_RANK, A_HEADS, A_NOPE + A_ROPE),
                   ((0, 0), (0, 0), (0, A_QK_PAD - A_NOPE - A_ROPE))).reshape(A_Q_RANK, A_HEADS * A_QK_PAD).astype(BF16)
    w_ukv = a_w_ukv.reshape(A_KV_RANK, A_HEADS, A_NOPE + A_V)
    w_uk = w_ukv[:, :, :A_NOPE].reshape(A_KV_RANK, A_HEADS * A_NOPE).astype(BF16)
    w_uv = w_ukv[:, :, A_NOPE:].reshape(A_KV_RANK, A_HEADS * A_V).astype(BF16)
    w_r = jnp.zeros((d, LANES), F32).at[:, :N_GROUPS].set(w_group).at[:, N_GROUPS:N_GROUPS + N_EXPERTS].set(w_expert)
    b_r = jnp.zeros((1, LANES), F32).at[0, :N_GROUPS].set(b_group).at[0, N_GROUPS:N_GROUPS + N_EXPERTS].set(b_expert)
    w_r_hi = w_r.astype(BF16)
    w_r = jnp.concatenate([w_r_hi, (w_r - w_r_hi.astype(F32)).astype(BF16)], axis=1)

    t = jnp.arange(s_max, dtype=jnp.int32)
    tabs_a = _rope_tables(t, None)
    tabs_b = _rope_tables(t // GRID_W, t % GRID_W)

    h, lat = _mod_proj(x_all, mod, tile_seg, w_lat, tm)
    qk_b = _proj_qk(h, w_qk, jnp.stack([b_q_norm, b_k_norm]).reshape(2, 1, B_HEAD_DIM), tabs_b, tile_pos, tm,
                    B_HEADS // B_GROUP)
    v_b = _proj_vt(h, w_v, tm, 512)
    q_a = _mla_q(lat, a_q_norm.reshape(1, -1), w_uq, tabs_a, tile_pos, tm)
    k_a, v_a = _mla_kv(lat, a_kv_norm.reshape(1, -1), w_uk, w_uv, tabs_a, tile_pos, tm)

    oa_parts, ob_parts = [], []
    row0 = 0
    for x, _ in groups:
        bsz, s = x.shape[0], x.shape[1]
        oa_parts.append(_flash(q_a, k_a, v_a, batch=bsz, seq=s, row0=row0, n_kv_heads=A_HEADS, group=1,
                               dq=A_QK_PAD, dv=A_V, k_col0=0, scale=(A_NOPE + A_ROPE) ** -0.5, tq=1024, tk=1024))
        ob_parts.append(_flash(qk_b, qk_b, v_b, batch=bsz, seq=s, row0=row0, n_kv_heads=B_KV_HEADS, group=B_GROUP,
                               dq=B_HEAD_DIM, dv=B_HEAD_DIM, k_col0=B_HEADS * B_HEAD_DIM, scale=B_HEAD_DIM ** -0.5,
                               tq=256, tk=1024))
        row0 += bsz * s
    o_a = jnp.concatenate(oa_parts, axis=0)
    o_b = jnp.concatenate(ob_parts, axis=0)

    gated = _gate_mix(h, o_a, o_b, w_g, a_w_o.astype(BF16), b_w_o.astype(BF16), tm, 512)
    tm2 = _tile(seg, 256)
    seg2 = jnp.repeat(tile_seg, tm // tm2)
    x1, h2, rt = _ln1(gated, x_all, mod, seg2, w_out.astype(BF16), ln1_g.reshape(1, d), ln1_b.reshape(1, d),
                      w_r, b_r, tm2)

    eid = rt[:, 2:4].astype(jnp.int32)
    tok_blocks, pos, blk_e, n_used = _dispatch(eid, EXPERT_ROWS)
    ys = _experts(h2, tok_blocks, blk_e, n_used, e_w_gate, e_w_up, e_w_down)
    pos_blocks = pos.reshape(n // tm2, tm2, TOP_K).transpose(0, 2, 1).reshape(n // tm2, 1, TOP_K * tm2)

    outs = []
    tile0 = 0
    for x, _ in groups:
        bsz, s = x.shape[0], x.shape[1]
        nt = bsz * s // tm2
        y = _final(x1, mod, seg2, pos_blocks, rt, ln2_g.reshape(1, d), ln2_b.reshape(1, d), ys, tile0, nt, tm2)
        outs.append(y.reshape(bsz, s, d))
        tile0 += nt
    return tuple(outs)


def kernel(x_prompt, x_sample, c_prompt, c_sample, w_ada, b_ada, w_in, a_q_norm, a_kv_norm, a_w_uq, a_w_ukv, a_w_o,
           b_q_norm, b_k_norm, b_w_o, w_out, ln1_g, ln1_b, w_group, b_group, w_expert, b_expert, e_w_gate, e_w_up,
           e_w_down, ln2_g, ln2_b):
    weights = (w_ada, b_ada, w_in, a_q_norm, a_kv_norm, a_w_uq, a_w_ukv, a_w_o, b_q_norm, b_k_norm, b_w_o, w_out,
               ln1_g, ln1_b, w_group, b_group, w_expert, b_expert, e_w_gate, e_w_up, e_w_down, ln2_g, ln2_b)
    return _layer(((x_prompt, c_prompt), (x_sample, c_sample)), *[w[0] for w in weights])
```

```python
import functools
import math

import numpy as np
import jax
import jax.numpy as jnp
from jax import lax
from jax.experimental import pallas as pl
from jax.experimental.pallas import tpu as pltpu

F32 = jnp.float32
BF16 = jnp.bfloat16
HIGHEST = lax.Precision.HIGHEST

GRID_W = 64
ROPE_THETA = 10000.0
A_HEADS = 16
A_Q_RANK = 512
A_KV_RANK = 512
A_NOPE = 128
A_ROPE = 64
A_V = 128
A_QK_PAD = 256
B_HEADS = 16
B_KV_HEADS = 4
B_GROUP = B_HEADS // B_KV_HEADS
B_HEAD_DIM = 128
N_GROUPS = 8
EXPERTS_PER_GROUP = 8
N_EXPERTS = N_GROUPS * EXPERTS_PER_GROUP
TOP_K = 2
RMS_EPS = 1e-6
LN_EPS = 1e-5
DEPTH = 1
DEEPNORM_ALPHA = (2 * DEPTH) ** 0.25
LOG2E = 1.4426950408889634

LANES = 128
EXPERT_ROWS = 256
SUB_ROWS = 128
GATHER_UNROLL = 8
ONES_ROWS = 16
VMEM_LIMIT = 48 * 1024 * 1024


def _params(*sem):
    return pltpu.CompilerParams(dimension_semantics=sem, vmem_limit_bytes=VMEM_LIMIT)


def _tile(n, pref):
    t = min(n, pref)
    while n % t:
        t //= 2
    return t


def _sub_blocks(rows, sub):
    sub = _tile(rows, sub)
    return [slice(r, r + sub) for r in range(0, rows, sub)]


def _ada_kernel(c_ref, w_ref, b_ref, o_ref):
    c = c_ref[...]
    a = c * jax.nn.sigmoid(c)
    o_ref[...] = jnp.dot(a, w_ref[...], precision=HIGHEST, preferred_element_type=F32) + b_ref[...]


def _ada(c_pad, w_ada, b_ada):
    d, n = w_ada.shape
    tn = _tile(n, 1024)
    return pl.pallas_call(
        _ada_kernel,
        grid=(n // tn,),
        in_specs=[pl.BlockSpec(c_pad.shape, lambda j: (0, 0)),
                  pl.BlockSpec((d, tn), lambda j: (0, j)),
                  pl.BlockSpec((1, tn), lambda j: (0, j))],
        out_specs=pl.BlockSpec((c_pad.shape[0], tn), lambda j: (0, j)),
        out_shape=jax.ShapeDtypeStruct((c_pad.shape[0], n), F32),
        compiler_params=_params("arbitrary"),
        name="ada",
    )(c_pad, w_ada, b_ada.reshape(1, n))


def _group_specs(xs, tm):
    specs, bounds = [], []
    t0 = 0
    for x in xs:
        nt = x.shape[0] // tm
        specs.append(pl.BlockSpec((tm, x.shape[1]), lambda i, *_, t0=t0, nt=nt: (jnp.clip(i - t0, 0, nt - 1), 0)))
        t0 += nt
        bounds.append(t0)
    return specs, tuple(bounds[:-1])


def _live_rows(i, refs, bounds, rs):
    val = refs[-1][rs, :]
    for g in reversed(range(len(refs) - 1)):
        val = jnp.where(i < bounds[g], refs[g][rs, :], val)
    return val


def _mod_proj_kernel(seg_ref, *refs, bounds):
    del seg_ref
    *x_refs, mod_ref, w_ref, h_ref, o_ref = refs
    sh = mod_ref[0, 0:1, :]
    sc = mod_ref[0, 1:2, :]
    x = _live_rows(pl.program_id(0), x_refs, bounds, slice(None))
    h = (x * (1.0 + sc) + sh).astype(BF16)
    h_ref[...] = h
    o_ref[...] = jnp.dot(h, w_ref[...], preferred_element_type=F32)


def _mod_proj(xs, mod, tile_seg, w, tm):
    n = sum(x.shape[0] for x in xs)
    d = xs[0].shape[1]
    nc = w.shape[1]
    x_specs, bounds = _group_specs(xs, tm)
    return pl.pallas_call(
        functools.partial(_mod_proj_kernel, bounds=bounds),
        grid_spec=pltpu.PrefetchScalarGridSpec(
            num_scalar_prefetch=1,
            grid=(n // tm,),
            in_specs=x_specs + [pl.BlockSpec((1, 8, d), lambda i, seg: (seg[i], 0, 0)),
                                pl.BlockSpec((d, nc), lambda i, seg: (0, 0))],
            out_specs=[pl.BlockSpec((tm, d), lambda i, seg: (i, 0)),
                       pl.BlockSpec((tm, nc), lambda i, seg: (i, 0))]),
        out_shape=[jax.ShapeDtypeStruct((n, d), BF16), jax.ShapeDtypeStruct((n, nc), F32)],
        compiler_params=_params("arbitrary"),
        name="mod_proj",
    )(tile_seg, *xs, mod, w)


def _rope(x, cos, sin_a, sin_b):
    return x * cos + pltpu.roll(x, 96, 1) * sin_a + pltpu.roll(x, 32, 1) * sin_b


def _store_vt(vt_ref, v, dv):
    vt = v.T.astype(vt_ref.dtype)
    dve = dv + ONES_ROWS
    for hh in range(v.shape[1] // dv):
        vt_ref[hh * dve:hh * dve + dv, :] = vt[hh * dv:(hh + 1) * dv, :]
        vt_ref[hh * dve + dv:(hh + 1) * dve, :] = jnp.ones((ONES_ROWS, vt.shape[1]), vt_ref.dtype)


def _proj_vt_kernel(h_ref, w_ref, o_ref):
    _store_vt(o_ref, jnp.dot(h_ref[...], w_ref[...], preferred_element_type=F32), B_HEAD_DIM)


def _proj_qk_kernel(pos_ref, h_ref, w_ref, g_ref, cos_ref, sa_ref, sb_ref, o_ref):
    del pos_ref
    g = g_ref[0]
    for rs in _sub_blocks(h_ref.shape[0], SUB_ROWS):
        acc = jnp.dot(h_ref[rs, :], w_ref[...], preferred_element_type=F32)
        cos, sa, sb = cos_ref[rs, :], sa_ref[rs, :], sb_ref[rs, :]
        for hh in range(acc.shape[1] // B_HEAD_DIM):
            x = acc[:, hh * B_HEAD_DIM:(hh + 1) * B_HEAD_DIM]
            r = lax.rsqrt(jnp.mean(x * x, axis=-1, keepdims=True) + RMS_EPS)
            y = x * r * g
            o_ref[rs, hh * B_HEAD_DIM:(hh + 1) * B_HEAD_DIM] = _rope(y, cos, sa, sb).astype(o_ref.dtype)


def _proj_vt(h, w, tm, tn):
    n, d = h.shape
    nc = w.shape[1]
    tne = tn // B_HEAD_DIM * (B_HEAD_DIM + ONES_ROWS)
    return pl.pallas_call(
        _proj_vt_kernel,
        grid=(n // tm, nc // tn),
        in_specs=[pl.BlockSpec((tm, d), lambda i, j: (i, 0)),
                  pl.BlockSpec((d, tn), lambda i, j: (0, j))],
        out_specs=pl.BlockSpec((tne, tm), lambda i, j: (j, i)),
        out_shape=jax.ShapeDtypeStruct((nc // tn * tne, n), BF16),
        compiler_params=_params("arbitrary", "arbitrary"),
        name="proj_vt",
    )(h, w)


def _proj_qk(h, w, gains, tabs, tile_pos, tm, n_q_tiles):
    n, d = h.shape
    nc = w.shape[1]
    tn = B_GROUP * B_HEAD_DIM
    tab_spec = pl.BlockSpec((tm, LANES), lambda i, j, pos: (pos[i], 0))
    return pl.pallas_call(
        _proj_qk_kernel,
        grid_spec=pltpu.PrefetchScalarGridSpec(
            num_scalar_prefetch=1,
            grid=(n // tm, nc // tn),
            in_specs=[pl.BlockSpec((tm, d), lambda i, j, pos: (i, 0)),
                      pl.BlockSpec((d, tn), lambda i, j, pos: (0, j)),
                      pl.BlockSpec((1, 1, LANES), lambda i, j, pos: (jnp.where(j < n_q_tiles, 0, 1), 0, 0)),
                      tab_spec, tab_spec, tab_spec],
            out_specs=pl.BlockSpec((tm, tn), lambda i, j, pos: (i, j))),
        out_shape=jax.ShapeDtypeStruct((n, nc), BF16),
        compiler_params=_params("arbitrary", "arbitrary"),
        name="proj_qk",
    )(tile_pos, h, w, gains, *tabs)


def _rms(x, g):
    return x * lax.rsqrt(jnp.mean(x * x, axis=-1, keepdims=True) + RMS_EPS) * g


def _mla_q_kernel(pos_ref, c_ref, g_ref, w_ref, cos_ref, sa_ref, sb_ref, o_ref):
    del pos_ref
    for rs in _sub_blocks(c_ref.shape[0], SUB_ROWS):
        xn = _rms(c_ref[rs, :], g_ref[...]).astype(BF16)
        acc = jnp.dot(xn, w_ref[...], preferred_element_type=F32)
        cos, sa, sb = cos_ref[rs, :], sa_ref[rs, :], sb_ref[rs, :]
        for hh in range(acc.shape[1] // A_QK_PAD):
            base = hh * A_QK_PAD
            o_ref[rs, base:base + A_NOPE] = acc[:, base:base + A_NOPE].astype(o_ref.dtype)
            o_ref[rs, base + A_NOPE:base + A_QK_PAD] = _rope(
                acc[:, base + A_NOPE:base + A_QK_PAD], cos, sa, sb).astype(o_ref.dtype)


def _mla_q(cqkv, gain, w, tabs, tile_pos, tm):
    n = cqkv.shape[0]
    nc = w.shape[1]
    tn = 4 * A_QK_PAD
    tab_spec = pl.BlockSpec((tm, LANES), lambda i, j, pos: (pos[i], 0))
    return pl.pallas_call(
        _mla_q_kernel,
        grid_spec=pltpu.PrefetchScalarGridSpec(
            num_scalar_prefetch=1,
            grid=(n // tm, nc // tn),
            in_specs=[pl.BlockSpec((tm, A_Q_RANK), lambda i, j, pos: (i, 0)),
                      pl.BlockSpec((1, A_Q_RANK), lambda i, j, pos: (0, 0)),
                      pl.BlockSpec((A_Q_RANK, tn), lambda i, j, pos: (0, j)),
                      tab_spec, tab_spec, tab_spec],
            out_specs=pl.BlockSpec((tm, tn), lambda i, j, pos: (i, j))),
        out_shape=jax.ShapeDtypeStruct((n, nc), BF16),
        compiler_params=_params("arbitrary", "arbitrary"),
        name="mla_q",
    )(tile_pos, cqkv, gain, w, *tabs)


def _mla_kv_kernel(pos_ref, c_ref, kr_ref, g_ref, wk_ref, wv_ref, cos_ref, sa_ref, sb_ref, k_ref, v_ref):
    del pos_ref
    xn = _rms(c_ref[...], g_ref[...]).astype(BF16)
    kn = jnp.dot(xn, wk_ref[...], preferred_element_type=F32)
    _store_vt(v_ref, jnp.dot(xn, wv_ref[...], preferred_element_type=F32), A_V)
    kr = _rope(kr_ref[...], cos_ref[...], sa_ref[...], sb_ref[...]).astype(k_ref.dtype)
    for hh in range(kn.shape[1] // A_NOPE):
        k_ref[:, hh * A_QK_PAD:hh * A_QK_PAD + A_NOPE] = kn[:, hh * A_NOPE:(hh + 1) * A_NOPE].astype(k_ref.dtype)
        k_ref[:, hh * A_QK_PAD + A_NOPE:(hh + 1) * A_QK_PAD] = kr


def _mla_kv(cqkv, gain, wk, wv, tabs, tile_pos, tm):
    n = cqkv.shape[0]
    hpt = 4
    n_ct = A_HEADS // hpt
    kr_blk = (A_Q_RANK + A_KV_RANK) // LANES
    tab_spec = pl.BlockSpec((tm, LANES), lambda i, j, pos: (pos[i], 0))
    return pl.pallas_call(
        _mla_kv_kernel,
        grid_spec=pltpu.PrefetchScalarGridSpec(
            num_scalar_prefetch=1,
            grid=(n // tm, n_ct),
            in_specs=[pl.BlockSpec((tm, A_KV_RANK), lambda i, j, pos: (i, 1)),
                      pl.BlockSpec((tm, LANES), lambda i, j, pos: (i, kr_blk)),
                      pl.BlockSpec((1, A_KV_RANK), lambda i, j, pos: (0, 0)),
                      pl.BlockSpec((A_KV_RANK, hpt * A_NOPE), lambda i, j, pos: (0, j)),
                      pl.BlockSpec((A_KV_RANK, hpt * A_V), lambda i, j, pos: (0, j)),
                      tab_spec, tab_spec, tab_spec],
            out_specs=[pl.BlockSpec((tm, hpt * A_QK_PAD), lambda i, j, pos: (i, j)),
                       pl.BlockSpec((hpt * (A_V + ONES_ROWS), tm), lambda i, j, pos: (j, i))]),
        out_shape=[jax.ShapeDtypeStruct((n, A_HEADS * A_QK_PAD), BF16),
                   jax.ShapeDtypeStruct((A_HEADS * (A_V + ONES_ROWS), n), BF16)],
        compiler_params=_params("arbitrary", "arbitrary"),
        name="mla_kv",
    )(tile_pos, cqkv, cqkv, gain, wk, wv, *tabs)


def _flash_kernel(q_ref, k_ref, vt_ref, o_ref, q_sc, m_sc, acc_sc, s_sc, *, group, dq, dv, tk, c):
    tq = q_ref.shape[0]
    n_kv = k_ref.shape[0] // tk
    for g in range(group):
        q_sc[g * tq:(g + 1) * tq, :] = (q_ref[:, g * dq:(g + 1) * dq].astype(F32) * c).astype(BF16)
    m_sc[...] = jnp.full(m_sc.shape, -1e30, F32)
    acc_sc[...] = jnp.zeros(acc_sc.shape, F32)

    def scores(j):
        k = k_ref[pl.ds(pl.multiple_of(j * tk, tk), tk), :]
        return lax.dot_general(k, q_sc[...], (((1,), (1,)), ((), ())), preferred_element_type=F32)

    def step(j, slot, prefetch):
        s = s_sc[slot]
        if prefetch:
            s_sc[1 - slot] = scores(j + 1)
        vt = vt_ref[:, pl.ds(pl.multiple_of(j * tk, tk), tk)]
        m_prev = m_sc[...]
        m_new = jnp.maximum(m_prev, jnp.max(s, axis=0, keepdims=True))
        alpha = jnp.exp2(m_prev - m_new)
        p = jnp.exp2(s - m_new).astype(BF16)
        acc_sc[...] = alpha * acc_sc[...] + jnp.dot(vt, p, preferred_element_type=F32)
        m_sc[...] = m_new

    s_sc[0] = scores(0)

    def pair(jj, carry):
        step(2 * jj, 0, True)
        step(2 * jj + 1, 1, True)
        return carry

    lax.fori_loop(0, n_kv // 2 - 1, pair, 0)
    step(n_kv - 2, 0, True)
    step(n_kv - 1, 1, False)
    acc = acc_sc[...]
    out = (acc[:dv] / acc[dv:dv + 1]).T
    for g in range(group):
        o_ref[:, g * dv:(g + 1) * dv] = out[g * tq:(g + 1) * tq, :].astype(o_ref.dtype)


def _flash_into_kernel(q_ref, k_ref, vt_ref, prev_ref, o_ref, *scratch, **kw):
    del prev_ref
    _flash_kernel(q_ref, k_ref, vt_ref, o_ref, *scratch, **kw)


def _flash(q2d, k2d, vt2d, out, *, batch, seq, row0, n_kv_heads, group, dq, dv, k_col0, scale, tq, tk):
    assert row0 % seq == 0 and k_col0 % dq == 0
    n = q2d.shape[0]
    tq = _tile(seq, tq)
    tk = _tile(seq // 2, tk)
    qb0, kb0 = row0 // tq, row0 // seq
    nq = seq // tq
    kc0 = k_col0 // dq
    rows = group * tq
    dve = dv + ONES_ROWS
    kw = dict(group=group, dq=dq, dv=dv, tk=tk, c=scale * LOG2E)
    in_specs = [pl.BlockSpec((tq, group * dq), lambda b, h, i: (qb0 + b * nq + i, h)),
                pl.BlockSpec((seq, dq), lambda b, h, i: (kb0 + b, kc0 + h)),
                pl.BlockSpec((dve, seq), lambda b, h, i: (h, kb0 + b))]
    if out is None:
        kern, operands, aliases = functools.partial(_flash_kernel, **kw), (q2d, k2d, vt2d), {}
    else:
        kern, operands, aliases = functools.partial(_flash_into_kernel, **kw), (q2d, k2d, vt2d, out), {3: 0}
        in_specs.append(pl.BlockSpec(memory_space=pl.ANY))
    return pl.pallas_call(
        kern,
        grid=(batch, n_kv_heads, nq),
        in_specs=in_specs,
        out_specs=pl.BlockSpec((tq, group * dv), lambda b, h, i: (qb0 + b * nq + i, h)),
        out_shape=jax.ShapeDtypeStruct((n, n_kv_heads * group * dv), BF16),
        scratch_shapes=[pltpu.VMEM((rows, dq), BF16), pltpu.VMEM((1, rows), F32), pltpu.VMEM((dve, rows), F32),
                        pltpu.VMEM((2, tk, rows), F32)],
        input_output_aliases=aliases,
        compiler_params=_params("arbitrary", "arbitrary", "arbitrary"),
        name="flash",
    )(*operands)


def _gate_mix_kernel(h_ref, oa_ref, ob_ref, wga_ref, wgb_ref, wa_ref, wb_ref, o_ref):
    for rs in _sub_blocks(h_ref.shape[0], 2 * SUB_ROWS):
        h = h_ref[rs, :]
        ga = jax.nn.sigmoid(jnp.dot(h, wga_ref[...], preferred_element_type=F32))
        gb = jax.nn.sigmoid(jnp.dot(h, wgb_ref[...], preferred_element_type=F32))
        ya = jnp.dot(oa_ref[rs, :], wa_ref[...], preferred_element_type=F32)
        yb = jnp.dot(ob_ref[rs, :], wb_ref[...], preferred_element_type=F32)
        o_ref[rs, :] = (ga * ya + gb * yb).astype(o_ref.dtype)


def _gate_mix(h, oa, ob, w_gates, wa, wb, tm, tn):
    n, d = oa.shape
    dm = wa.shape[1]
    nct = dm // tn
    rows = pl.BlockSpec((tm, d), lambda i, j: (i, 0))
    cols = pl.BlockSpec((d, tn), lambda i, j: (0, j))
    return pl.pallas_call(
        _gate_mix_kernel,
        grid=(n // tm, nct),
        in_specs=[rows, rows, rows, cols, pl.BlockSpec((d, tn), lambda i, j: (0, nct + j)), cols, cols],
        out_specs=pl.BlockSpec((tm, tn), lambda i, j: (i, j)),
        out_shape=jax.ShapeDtypeStruct((n, dm), BF16),
        compiler_params=_params("arbitrary", "arbitrary"),
        name="gate_mix",
    )(h, oa, ob, w_gates, w_gates, wa, wb)


def _layer_norm(y, g, b):
    mu = jnp.mean(y, axis=-1, keepdims=True)
    yc = y - mu
    var = jnp.mean(yc * yc, axis=-1, keepdims=True)
    return yc * lax.rsqrt(var + LN_EPS) * g + b


def _route(logits):
    lane = lax.broadcasted_iota(jnp.int32, logits.shape, 1).astype(F32)
    big = float(LANES)
    ninf = -jnp.inf
    gmask = lane < N_GROUPS
    gl = jnp.where(gmask, logits, ninf)
    gmax = jnp.max(gl, axis=-1, keepdims=True)
    grp = jnp.min(jnp.where(gl == gmax, lane, big), axis=-1, keepdims=True)
    den = jnp.sum(jnp.exp(gl - gmax), axis=-1, keepdims=True)
    p_grp = 1.0 / den
    lo = N_GROUPS + grp * EXPERTS_PER_GROUP
    el = jnp.where(lane >= lo, jnp.where(lane < lo + EXPERTS_PER_GROUP, logits, ninf), ninf)
    t1 = jnp.max(el, axis=-1, keepdims=True)
    i1 = jnp.min(jnp.where(el == t1, lane, big), axis=-1, keepdims=True)
    el2 = jnp.where(lane == i1, ninf, el)
    t2 = jnp.max(el2, axis=-1, keepdims=True)
    i2 = jnp.min(jnp.where(el2 == t2, lane, big), axis=-1, keepdims=True)
    e = jnp.exp(t2 - t1)
    w1 = p_grp / (1.0 + e)
    w2 = p_grp * e / (1.0 + e)
    out = jnp.where(lane == 0, w1, jnp.where(lane == 1, w2, jnp.where(lane == 2, i1 - N_GROUPS, i2 - N_GROUPS)))
    return jnp.where(lane < 4, out, 0.0)


def _ln1_kernel(seg_ref, g_ref, *refs, bounds):
    del seg_ref
    *x_refs, mod_ref, w_ref, lg_ref, lb_ref, wr_ref, br_ref, x1_ref, h2_ref, rt_ref = refs
    i = pl.program_id(0)
    g1 = mod_ref[0, 2:3, :]
    sh2 = mod_ref[0, 3:4, :]
    sc2 = mod_ref[0, 4:5, :]
    for rs in _sub_blocks(g_ref.shape[0], SUB_ROWS):
        attn = jnp.dot(g_ref[rs, :], w_ref[...], preferred_element_type=F32)
        x = _live_rows(i, x_refs, bounds, rs)
        x1 = _layer_norm(DEEPNORM_ALPHA * x + g1 * attn, lg_ref[...], lb_ref[...])
        x1_ref[rs, :] = x1
        h2 = x1 * (1.0 + sc2) + sh2
        h2_ref[rs, :] = h2
        h_hi = h2.astype(BF16)
        h_lo = (h2 - h_hi.astype(F32)).astype(BF16)
        a = jnp.dot(h_hi, wr_ref[...], preferred_element_type=F32)
        b = jnp.dot(h_lo, wr_ref[:, :LANES], preferred_element_type=F32)
        rt_ref[rs, :] = _route(a[:, :LANES] + a[:, LANES:] + b + br_ref[...])


def _ln1(gated, xs, mod, tile_seg, w_out, lg, lb, wr, br, tm):
    n, d = gated.shape
    row = lambda i, seg: (i, 0)
    const = lambda i, seg: (0, 0)
    x_specs, bounds = _group_specs(xs, tm)
    return pl.pallas_call(
        functools.partial(_ln1_kernel, bounds=bounds),
        grid_spec=pltpu.PrefetchScalarGridSpec(
            num_scalar_prefetch=1,
            grid=(n // tm,),
            in_specs=[pl.BlockSpec((tm, d), row)] + x_specs + [
                      pl.BlockSpec((1, 8, d), lambda i, seg: (seg[i], 0, 0)),
                      pl.BlockSpec((d, d), const),
                      pl.BlockSpec((1, d), const),
                      pl.BlockSpec((1, d), const),
                      pl.BlockSpec((d, 2 * LANES), const),
                      pl.BlockSpec((1, LANES), const)],
            out_specs=[pl.BlockSpec((tm, d), row), pl.BlockSpec((tm, d), row), pl.BlockSpec((tm, LANES), row)]),
        out_shape=[jax.ShapeDtypeStruct((n, d), F32), jax.ShapeDtypeStruct((n, d), F32),
                   jax.ShapeDtypeStruct((n, LANES), F32)],
        compiler_params=_params("arbitrary"),
        name="ln1_route",
    )(tile_seg, gated, *xs, mod, w_out, lg, lb, wr, br)


def _row_gather_start(src_hbm, dst, sem, idx, row):
    pltpu.make_async_copy(src_hbm.at[pl.ds(idx, 1), :], dst.at[pl.ds(row, 1), :], sem).start()


def _row_gather_wait(src_hbm, dst, sem, row):
    pltpu.make_async_copy(src_hbm.at[pl.ds(0, 1), :], dst.at[pl.ds(row, 1), :], sem).wait()


def _gather_rows_start(src_hbm, dst, sem, idx_ref, idx0, rows):
    def body(r, carry):
        _row_gather_start(src_hbm, dst, sem, idx_ref[0, 0, idx0 + r], r)
        return carry

    lax.fori_loop(0, rows, body, 0, unroll=GATHER_UNROLL)


def _gather_rows_wait(src_hbm, dst, sem, rows):
    def body(r, carry):
        _row_gather_wait(src_hbm, dst, sem, r)
        return carry

    lax.fori_loop(0, rows, body, 0, unroll=GATHER_UNROLL)


def _expert_kernel(blk_e_ref, n_used_ref, tok_ref, tok_next_ref, h2_hbm, wg_ref, wu_ref, wd_ref, o_ref,
                   xbuf, wg_bf, wu_bf, wd_bf, sem):
    i = pl.program_id(0)
    n_used = n_used_ref[0]
    e = blk_e_ref[i]
    prev = blk_e_ref[jnp.maximum(i - 1, 0)]
    rows = xbuf.shape[1]
    slot = i % 2

    @pl.when(i == 0)
    def _():
        _gather_rows_start(h2_hbm, xbuf.at[0], sem.at[0], tok_ref, 0, rows)

    @pl.when(i + 1 < n_used)
    def _():
        _gather_rows_start(h2_hbm, xbuf.at[1 - slot], sem.at[1 - slot], tok_next_ref, 0, rows)

    @pl.when(jnp.logical_or(i == 0, e != prev))
    def _():
        wg_bf[...] = wg_ref[0].astype(BF16)
        wu_bf[...] = wu_ref[0].astype(BF16)
        wd_bf[...] = wd_ref[0].astype(BF16)

    @pl.when(i < n_used)
    def _():
        _gather_rows_wait(h2_hbm, xbuf.at[slot], sem.at[slot], rows)
        for rs in _sub_blocks(rows, SUB_ROWS):
            x = xbuf[slot, rs, :].astype(BF16)
            gate = jnp.dot(x, wg_bf[...], preferred_element_type=F32)
            up = jnp.dot(x, wu_bf[...], preferred_element_type=F32)
            act = (gate * jax.nn.sigmoid(gate) * up).astype(BF16)
            o_ref[rs, :] = jnp.dot(act, wd_bf[...], preferred_element_type=F32)

    @pl.when(i >= n_used)
    def _():
        o_ref[...] = jnp.zeros_like(o_ref)


def _experts(h2, tok_blocks, blk_e, n_used, w_gate, w_up, w_down):
    n_blk = tok_blocks.shape[0]
    rows = tok_blocks.shape[2]
    d = h2.shape[1]
    de = w_gate.shape[2]
    return pl.pallas_call(
        _expert_kernel,
        grid_spec=pltpu.PrefetchScalarGridSpec(
            num_scalar_prefetch=2,
            grid=(n_blk,),
            in_specs=[pl.BlockSpec((1, 1, rows), lambda i, be, nu: (i, 0, 0), memory_space=pltpu.SMEM),
                      pl.BlockSpec((1, 1, rows), lambda i, be, nu: (jnp.minimum(i + 1, n_blk - 1), 0, 0),
                                   memory_space=pltpu.SMEM),
                      pl.BlockSpec(memory_space=pl.ANY),
                      pl.BlockSpec((1, d, de), lambda i, be, nu: (be[i], 0, 0)),
                      pl.BlockSpec((1, d, de), lambda i, be, nu: (be[i], 0, 0)),
                      pl.BlockSpec((1, de, d), lambda i, be, nu: (be[i], 0, 0))],
            out_specs=pl.BlockSpec((rows, d), lambda i, be, nu: (i, 0)),
            scratch_shapes=[pltpu.VMEM((2, rows, d), F32),
                            pltpu.VMEM((d, de), BF16),
                            pltpu.VMEM((d, de), BF16),
                            pltpu.VMEM((de, d), BF16),
                            pltpu.SemaphoreType.DMA((2,))]),
        out_shape=jax.ShapeDtypeStruct((n_blk * rows, d), F32),
        compiler_params=_params("arbitrary"),
        name="experts",
    )(blk_e, n_used, tok_blocks, tok_blocks, h2, w_gate, w_up, w_down)


def _final_kernel(seg_ref, pos_ref, pos_next_ref, x1_ref, mod_ref, rt_ref, lg_ref, lb_ref, ys_hbm, o_ref, buf, sem):
    del seg_ref
    i = pl.program_id(0)
    tm = x1_ref.shape[0]
    slot = i % 2

    def start_tile(idx_ref, s):
        for k in range(TOP_K):
            _gather_rows_start(ys_hbm, buf.at[s, k], sem.at[s], idx_ref, k * tm, tm)

    @pl.when(i == 0)
    def _():
        start_tile(pos_ref, 0)

    @pl.when(i + 1 < pl.num_programs(0))
    def _():
        start_tile(pos_next_ref, 1 - slot)

    for k in range(TOP_K):
        _gather_rows_wait(ys_hbm, buf.at[slot, k], sem.at[slot], tm)
    g2 = mod_ref[0, 5:6, :]
    for rs in _sub_blocks(tm, SUB_ROWS):
        rt = rt_ref[rs, :]
        ffn = rt[:, 0:1] * buf[slot, 0, rs, :] + rt[:, 1:2] * buf[slot, 1, rs, :]
        o_ref[rs, :] = _layer_norm(DEEPNORM_ALPHA * x1_ref[rs, :] + g2 * ffn, lg_ref[...], lb_ref[...])


def _final(x1, mod, tile_seg, pos_blocks, rt, lg, lb, ys, tile0, n_tiles, tm):
    d = x1.shape[1]
    row = lambda i, seg: (tile0 + i, 0)
    const = lambda i, seg: (0, 0)
    return pl.pallas_call(
        _final_kernel,
        grid_spec=pltpu.PrefetchScalarGridSpec(
            num_scalar_prefetch=1,
            grid=(n_tiles,),
            in_specs=[pl.BlockSpec((1, 1, TOP_K * tm), lambda i, seg: (tile0 + i, 0, 0), memory_space=pltpu.SMEM),
                      pl.BlockSpec((1, 1, TOP_K * tm),
                                   lambda i, seg: (tile0 + jnp.minimum(i + 1, n_tiles - 1), 0, 0),
                                   memory_space=pltpu.SMEM),
                      pl.BlockSpec((tm, d), row),
                      pl.BlockSpec((1, 8, d), lambda i, seg: (seg[tile0 + i], 0, 0)),
                      pl.BlockSpec((tm, LANES), row),
                      pl.BlockSpec((1, d), const),
                      pl.BlockSpec((1, d), const),
                      pl.BlockSpec(memory_space=pl.ANY)],
            out_specs=pl.BlockSpec((tm, d), lambda i, seg: (i, 0)),
            scratch_shapes=[pltpu.VMEM((2, TOP_K, tm, d), F32), pltpu.SemaphoreType.DMA((2,))]),
        out_shape=jax.ShapeDtypeStruct((n_tiles * tm, d), F32),
        compiler_params=_params("arbitrary"),
        name="combine_ln2",
    )(tile_seg, pos_blocks, pos_blocks, x1, mod, rt, lg, lb, ys)


def _rope_tables(pos_a, pos_b):
    inv = ROPE_THETA ** (-jnp.arange(0, 64, 2, dtype=F32) / 64)

    def half(pos):
        if pos is None:
            z = jnp.zeros((pos_a.shape[0], 64), F32)
            return z, z, z
        ang = pos.astype(F32)[:, None] * inv[None, :]
        cos, sin = jnp.cos(ang), jnp.sin(ang)
        zero = jnp.zeros_like(sin)
        return (jnp.concatenate([cos, cos], 1), jnp.concatenate([-sin, zero], 1), jnp.concatenate([zero, sin], 1))

    a, b = half(pos_a), half(pos_b)
    return tuple(jnp.concatenate([u, v], 1) for u, v in zip(a, b))


def _dispatch(eid, rows):
    n = eid.shape[0]
    a = n * TOP_K
    i32 = jnp.int32
    iota = jnp.arange(a, dtype=i32)
    experts = jnp.arange(N_EXPERTS, dtype=i32)
    e_sorted, order = lax.sort((eid.reshape(-1), iota), num_keys=1, is_stable=True)
    starts = jnp.sum((e_sorted[None, :] < experts[:, None]).astype(i32), axis=1)
    counts = jnp.diff(starts, append=a)
    padded = ((counts + rows - 1) // rows) * rows
    pad_end = jnp.cumsum(padded)
    pad_start = pad_end - padded
    gap_step = jnp.diff(pad_start - starts, prepend=0)
    dest = iota + jnp.sum(jnp.where(e_sorted[:, None] >= experts[None, :], gap_step[None, :], 0), axis=1)
    _, pos = lax.sort((order, dest), num_keys=1)
    p_rows = ((a + rows - 1) // rows) * rows + N_EXPERTS * rows
    n_blk = p_rows // rows
    blk = jnp.arange(n_blk, dtype=i32)
    blk_e = jnp.minimum(jnp.sum((blk[:, None] * rows >= pad_end[None, :]).astype(i32), axis=1), N_EXPERTS - 1)
    onehot = blk_e[:, None] == experts[None, :]
    src0 = jnp.sum(jnp.where(onehot, (starts - pad_start)[None, :], 0), axis=1) + blk * rows
    src0 = jnp.minimum(src0, a)
    order_ext = jnp.concatenate([order, jnp.zeros((rows,), i32)])
    tok = jnp.take(order_ext, src0[:, None] + jnp.arange(rows, dtype=i32)[None, :], axis=0) // TOP_K
    n_used = (pad_end[-1] // rows).astype(i32).reshape(1)
    return tok.reshape(n_blk, 1, rows), pos.reshape(n, TOP_K), blk_e.astype(i32), n_used


def _layer(groups, w_ada, b_ada, w_in, a_q_norm, a_kv_norm, a_w_uq, a_w_ukv, a_w_o, b_q_norm, b_k_norm, b_w_o,
           w_out, ln1_g, ln1_b, w_group, b_group, w_expert, b_expert, e_w_gate, e_w_up, e_w_down, ln2_g, ln2_b):
    d = w_in.shape[0]
    seqs = [x.shape[1] for x, _ in groups]
    seg = functools.reduce(math.gcd, seqs)
    tm = _tile(seg, 512)
    s_max = max(seqs)

    xs = [x.reshape(-1, d) for x, _ in groups]
    n = sum(x.shape[0] for x in xs)
    c_all = jnp.concatenate([c for _, c in groups], axis=0)
    nb = c_all.shape[0]
    c_pad = jnp.zeros((-(-nb // 8) * 8, d), F32).at[:nb].set(c_all)
    seg_batch, tile_seg, tile_pos = [], [], []
    b0 = 0
    for x, _ in groups:
        bsz, s = x.shape[0], x.shape[1]
        for b in range(bsz):
            for t in range(s // tm):
                tile_seg.append(b0 + b)
                tile_pos.append(t)
        b0 += bsz
    tile_seg = jnp.asarray(np.asarray(tile_seg, np.int32))
    tile_pos = jnp.asarray(np.asarray(tile_pos, np.int32))

    mod = _ada(c_pad, w_ada, b_ada)[:nb].reshape(nb, 6, d)
    mod = jnp.concatenate([mod, jnp.zeros((nb, 2, d), F32)], axis=1)

    o_q, o_kv, o_kr, o_qb, o_kb, o_vb, o_ga, o_gb = np.cumsum(
        [0, A_Q_RANK, A_KV_RANK, A_ROPE, B_HEADS * B_HEAD_DIM, B_KV_HEADS * B_HEAD_DIM,
         B_KV_HEADS * B_HEAD_DIM, d]).tolist()
    w_lat = jnp.pad(w_in[:, :o_qb], ((0, 0), (0, o_kr + LANES - o_qb))).astype(BF16)
    w_qk = w_in[:, o_qb:o_vb].astype(BF16)
    w_v = w_in[:, o_vb:o_ga].astype(BF16)
    w_g = w_in[:, o_ga:].astype(BF16)
    w_uq = jnp.pad(a_w_uq.reshape(A_Q_RANK, A_HEADS, A_NOPE + A_ROPE),
                   ((0, 0), (0, 0), (0, A_QK_PAD - A_NOPE - A_ROPE))).reshape(A_Q_RANK, A_HEADS * A_QK_PAD).astype(BF16)
    w_ukv = a_w_ukv.reshape(A_KV_RANK, A_HEADS, A_NOPE + A_V)
    w_uk = w_ukv[:, :, :A_NOPE].reshape(A_KV_RANK, A_HEADS * A_NOPE).astype(BF16)
    w_uv = w_ukv[:, :, A_NOPE:].reshape(A_KV_RANK, A_HEADS * A_V).astype(BF16)
    w_r = jnp.zeros((d, LANES), F32).at[:, :N_GROUPS].set(w_group).at[:, N_GROUPS:N_GROUPS + N_EXPERTS].set(w_expert)
    b_r = jnp.zeros((1, LANES), F32).at[0, :N_GROUPS].set(b_group).at[0, N_GROUPS:N_GROUPS + N_EXPERTS].set(b_expert)
    w_r_hi = w_r.astype(BF16)
    w_r = jnp.concatenate([w_r_hi, (w_r - w_r_hi.astype(F32)).astype(BF16)], axis=1)

    t = jnp.arange(s_max, dtype=jnp.int32)
    tabs_a = _rope_tables(t, None)
    tabs_b = _rope_tables(t // GRID_W, t % GRID_W)

    h, lat = _mod_proj(xs, mod, tile_seg, w_lat, tm)
    qk_b = _proj_qk(h, w_qk, jnp.stack([b_q_norm, b_k_norm]).reshape(2, 1, B_HEAD_DIM), tabs_b, tile_pos, tm,
                    B_HEADS // B_GROUP)
    v_b = _proj_vt(h, w_v, tm, 512)
    q_a = _mla_q(lat, a_q_norm.reshape(1, -1), w_uq, tabs_a, tile_pos, tm)
    k_a, v_a = _mla_kv(lat, a_kv_norm.reshape(1, -1), w_uk, w_uv, tabs_a, tile_pos, tm)

    o_a = o_b = None
    row0 = 0
    for x, _ in groups:
        bsz, s = x.shape[0], x.shape[1]
        o_a = _flash(q_a, k_a, v_a, o_a, batch=bsz, seq=s, row0=row0, n_kv_heads=A_HEADS, group=1,
                     dq=A_QK_PAD, dv=A_V, k_col0=0, scale=(A_NOPE + A_ROPE) ** -0.5, tq=1024, tk=1024)
        o_b = _flash(qk_b, qk_b, v_b, o_b, batch=bsz, seq=s, row0=row0, n_kv_heads=B_KV_HEADS, group=B_GROUP,
                     dq=B_HEAD_DIM, dv=B_HEAD_DIM, k_col0=B_HEADS * B_HEAD_DIM, scale=B_HEAD_DIM ** -0.5,
                     tq=256, tk=1024)
        row0 += bsz * s

    gated = _gate_mix(h, o_a, o_b, w_g, a_w_o.astype(BF16), b_w_o.astype(BF16), tm, 512)
    tm2 = _tile(seg, 256)
    seg2 = jnp.repeat(tile_seg, tm // tm2)
    x1, h2, rt = _ln1(gated, xs, mod, seg2, w_out.astype(BF16), ln1_g.reshape(1, d), ln1_b.reshape(1, d),
                      w_r, b_r, tm2)

    eid = rt[:, 2:4].astype(jnp.int32)
    tok_blocks, pos, blk_e, n_used = _dispatch(eid, EXPERT_ROWS)
    ys = _experts(h2, tok_blocks, blk_e, n_used, e_w_gate, e_w_up, e_w_down)
    pos_blocks = pos.reshape(n // tm2, tm2, TOP_K).transpose(0, 2, 1).reshape(n // tm2, 1, TOP_K * tm2)

    outs = []
    tile0 = 0
    for x, _ in groups:
        bsz, s = x.shape[0], x.shape[1]
        nt = bsz * s // tm2
        y = _final(x1, mod, seg2, pos_blocks, rt, ln2_g.reshape(1, d), ln2_b.reshape(1, d), ys, tile0, nt, tm2)
        outs.append(y.reshape(bsz, s, d))
        tile0 += nt
    return tuple(outs)


def kernel(x_prompt, x_sample, c_prompt, c_sample, w_ada, b_ada, w_in, a_q_norm, a_kv_norm, a_w_uq, a_w_ukv, a_w_o,
           b_q_norm, b_k_norm, b_w_o, w_out, ln1_g, ln1_b, w_group, b_group, w_expert, b_expert, e_w_gate, e_w_up,
           e_w_down, ln2_g, ln2_b):
    weights = (w_ada, b_ada, w_in, a_q_norm, a_kv_norm, a_w_uq, a_w_ukv, a_w_o, b_q_norm, b_k_norm, b_w_o, w_out,
               ln1_g, ln1_b, w_group, b_group, w_expert, b_expert, e_w_gate, e_w_up, e_w_down, ln2_g, ln2_b)
    return _layer(((x_prompt, c_prompt), (x_sample, c_sample)), *[w[0] for w in weights])
```

```python
import functools
import math

import numpy as np
import jax
import jax.numpy as jnp
from jax import lax
from jax.experimental import pallas as pl
from jax.experimental.pallas import tpu as pltpu

F32 = jnp.float32
BF16 = jnp.bfloat16
HIGHEST = lax.Precision.HIGHEST

GRID_W = 64
ROPE_THETA = 10000.0
A_HEADS = 16
A_Q_RANK = 512
A_KV_RANK = 512
A_NOPE = 128
A_ROPE = 64
A_V = 128
A_QK_PAD = 256
B_HEADS = 16
B_KV_HEADS = 4
B_GROUP = B_HEADS // B_KV_HEADS
B_HEAD_DIM = 128
N_GROUPS = 8
EXPERTS_PER_GROUP = 8
N_EXPERTS = N_GROUPS * EXPERTS_PER_GROUP
TOP_K = 2
RMS_EPS = 1e-6
LN_EPS = 1e-5
DEPTH = 1
DEEPNORM_ALPHA = (2 * DEPTH) ** 0.25
LOG2E = 1.4426950408889634

LANES = 128
EXPERT_ROWS = 256
SUB_ROWS = 128
GATHER_UNROLL = 8
ONES_ROWS = 16
VMEM_LIMIT = 48 * 1024 * 1024


def _params(*sem):
    return pltpu.CompilerParams(dimension_semantics=sem, vmem_limit_bytes=VMEM_LIMIT)


def _tile(n, pref):
    t = min(n, pref)
    while n % t:
        t //= 2
    return t


def _sub_blocks(rows, sub):
    sub = _tile(rows, sub)
    return [slice(r, r + sub) for r in range(0, rows, sub)]


def _ada_kernel(c_ref, w_ref, b_ref, o_ref):
    c = c_ref[...]
    a = c * jax.nn.sigmoid(c)
    o_ref[...] = jnp.dot(a, w_ref[...], precision=HIGHEST, preferred_element_type=F32) + b_ref[...]


def _ada(c_pad, w_ada, b_ada):
    d, n = w_ada.shape
    tn = _tile(n, 1024)
    return pl.pallas_call(
        _ada_kernel,
        grid=(n // tn,),
        in_specs=[pl.BlockSpec(c_pad.shape, lambda j: (0, 0)),
                  pl.BlockSpec((d, tn), lambda j: (0, j)),
                  pl.BlockSpec((1, tn), lambda j: (0, j))],
        out_specs=pl.BlockSpec((c_pad.shape[0], tn), lambda j: (0, j)),
        out_shape=jax.ShapeDtypeStruct((c_pad.shape[0], n), F32),
        compiler_params=_params("arbitrary"),
        name="ada",
    )(c_pad, w_ada, b_ada.reshape(1, n))


def _group_specs(xs, tm):
    specs, bounds = [], []
    t0 = 0
    for x in xs:
        nt = x.shape[0] // tm
        specs.append(pl.BlockSpec((tm, x.shape[1]), lambda i, *_, t0=t0, nt=nt: (jnp.clip(i - t0, 0, nt - 1), 0)))
        t0 += nt
        bounds.append(t0)
    return specs, tuple(bounds[:-1])


def _live_rows(i, refs, bounds, rs):
    val = refs[-1][rs, :]
    for g in reversed(range(len(refs) - 1)):
        val = jnp.where(i < bounds[g], refs[g][rs, :], val)
    return val


def _mod_proj_kernel(seg_ref, *refs, bounds):
    del seg_ref
    *x_refs, mod_ref, w_ref, h_ref, o_ref = refs
    sh = mod_ref[0, 0:1, :]
    sc = mod_ref[0, 1:2, :]
    x = _live_rows(pl.program_id(0), x_refs, bounds, slice(None))
    h = (x * (1.0 + sc) + sh).astype(BF16)
    h_ref[...] = h
    o_ref[...] = jnp.dot(h, w_ref[...], preferred_element_type=F32)


def _mod_proj(xs, mod, tile_seg, w, tm):
    n = sum(x.shape[0] for x in xs)
    d = xs[0].shape[1]
    nc = w.shape[1]
    x_specs, bounds = _group_specs(xs, tm)
    return pl.pallas_call(
        functools.partial(_mod_proj_kernel, bounds=bounds),
        grid_spec=pltpu.PrefetchScalarGridSpec(
            num_scalar_prefetch=1,
            grid=(n // tm,),
            in_specs=x_specs + [pl.BlockSpec((1, 8, d), lambda i, seg: (seg[i], 0, 0)),
                                pl.BlockSpec((d, nc), lambda i, seg: (0, 0))],
            out_specs=[pl.BlockSpec((tm, d), lambda i, seg: (i, 0)),
                       pl.BlockSpec((tm, nc), lambda i, seg: (i, 0))]),
        out_shape=[jax.ShapeDtypeStruct((n, d), BF16), jax.ShapeDtypeStruct((n, nc), F32)],
        compiler_params=_params("arbitrary"),
        name="mod_proj",
    )(tile_seg, *xs, mod, w)


def _rope(x, cos, sin_a, sin_b):
    return x * cos + pltpu.roll(x, 96, 1) * sin_a + pltpu.roll(x, 32, 1) * sin_b


def _store_vt(vt_ref, cols, v, dv):
    vt = v.T.astype(vt_ref.dtype)
    dve = dv + ONES_ROWS
    for hh in range(v.shape[1] // dv):
        vt_ref[hh * dve:hh * dve + dv, cols] = vt[hh * dv:(hh + 1) * dv, :]
        vt_ref[hh * dve + dv:(hh + 1) * dve, cols] = jnp.ones((ONES_ROWS, vt.shape[1]), vt_ref.dtype)


def _proj_vt_kernel(h_ref, w_ref, o_ref):
    for rs in _sub_blocks(h_ref.shape[0], SUB_ROWS):
        _store_vt(o_ref, rs, jnp.dot(h_ref[rs, :], w_ref[...], preferred_element_type=F32), B_HEAD_DIM)


def _proj_qk_kernel(pos_ref, h_ref, w_ref, g_ref, cos_ref, sa_ref, sb_ref, o_ref):
    del pos_ref
    g = g_ref[0]
    for rs in _sub_blocks(h_ref.shape[0], SUB_ROWS):
        acc = jnp.dot(h_ref[rs, :], w_ref[...], preferred_element_type=F32)
        cos, sa, sb = cos_ref[rs, :], sa_ref[rs, :], sb_ref[rs, :]
        for hh in range(acc.shape[1] // B_HEAD_DIM):
            x = acc[:, hh * B_HEAD_DIM:(hh + 1) * B_HEAD_DIM]
            r = lax.rsqrt(jnp.mean(x * x, axis=-1, keepdims=True) + RMS_EPS)
            y = x * r * g
            o_ref[rs, hh * B_HEAD_DIM:(hh + 1) * B_HEAD_DIM] = _rope(y, cos, sa, sb).astype(o_ref.dtype)


def _proj_vt(h, w, tm, tn):
    n, d = h.shape
    nc = w.shape[1]
    tne = tn // B_HEAD_DIM * (B_HEAD_DIM + ONES_ROWS)
    return pl.pallas_call(
        _proj_vt_kernel,
        grid=(n // tm, nc // tn),
        in_specs=[pl.BlockSpec((tm, d), lambda i, j: (i, 0)),
                  pl.BlockSpec((d, tn), lambda i, j: (0, j))],
        out_specs=pl.BlockSpec((tne, tm), lambda i, j: (j, i)),
        out_shape=jax.ShapeDtypeStruct((nc // tn * tne, n), BF16),
        compiler_params=_params("arbitrary", "arbitrary"),
        name="proj_vt",
    )(h, w)


def _proj_qk(h, w, gains, tabs, tile_pos, tm, n_q_tiles):
    n, d = h.shape
    nc = w.shape[1]
    tn = B_GROUP * B_HEAD_DIM
    tab_spec = pl.BlockSpec((tm, LANES), lambda i, j, pos: (pos[i], 0))
    return pl.pallas_call(
        _proj_qk_kernel,
        grid_spec=pltpu.PrefetchScalarGridSpec(
            num_scalar_prefetch=1,
            grid=(n // tm, nc // tn),
            in_specs=[pl.BlockSpec((tm, d), lambda i, j, pos: (i, 0)),
                      pl.BlockSpec((d, tn), lambda i, j, pos: (0, j)),
                      pl.BlockSpec((1, 1, LANES), lambda i, j, pos: (jnp.where(j < n_q_tiles, 0, 1), 0, 0)),
                      tab_spec, tab_spec, tab_spec],
            out_specs=pl.BlockSpec((tm, tn), lambda i, j, pos: (i, j))),
        out_shape=jax.ShapeDtypeStruct((n, nc), BF16),
        compiler_params=_params("arbitrary", "arbitrary"),
        name="proj_qk",
    )(tile_pos, h, w, gains, *tabs)


def _rms(x, g):
    return x * lax.rsqrt(jnp.mean(x * x, axis=-1, keepdims=True) + RMS_EPS) * g


def _mla_q_kernel(pos_ref, c_ref, g_ref, w_ref, cos_ref, sa_ref, sb_ref, o_ref):
    del pos_ref
    for rs in _sub_blocks(c_ref.shape[0], SUB_ROWS):
        xn = _rms(c_ref[rs, :], g_ref[...]).astype(BF16)
        acc = jnp.dot(xn, w_ref[...], preferred_element_type=F32)
        cos, sa, sb = cos_ref[rs, :], sa_ref[rs, :], sb_ref[rs, :]
        for hh in range(acc.shape[1] // A_QK_PAD):
            base = hh * A_QK_PAD
            o_ref[rs, base:base + A_NOPE] = acc[:, base:base + A_NOPE].astype(o_ref.dtype)
            o_ref[rs, base + A_NOPE:base + A_QK_PAD] = _rope(
                acc[:, base + A_NOPE:base + A_QK_PAD], cos, sa, sb).astype(o_ref.dtype)


def _mla_q(cqkv, gain, w, tabs, tile_pos, tm):
    n = cqkv.shape[0]
    nc = w.shape[1]
    tn = 4 * A_QK_PAD
    tab_spec = pl.BlockSpec((tm, LANES), lambda i, j, pos: (pos[i], 0))
    return pl.pallas_call(
        _mla_q_kernel,
        grid_spec=pltpu.PrefetchScalarGridSpec(
            num_scalar_prefetch=1,
            grid=(n // tm, nc // tn),
            in_specs=[pl.BlockSpec((tm, A_Q_RANK), lambda i, j, pos: (i, 0)),
                      pl.BlockSpec((1, A_Q_RANK), lambda i, j, pos: (0, 0)),
                      pl.BlockSpec((A_Q_RANK, tn), lambda i, j, pos: (0, j)),
                      tab_spec, tab_spec, tab_spec],
            out_specs=pl.BlockSpec((tm, tn), lambda i, j, pos: (i, j))),
        out_shape=jax.ShapeDtypeStruct((n, nc), BF16),
        compiler_params=_params("arbitrary", "arbitrary"),
        name="mla_q",
    )(tile_pos, cqkv, gain, w, *tabs)


def _mla_kv_kernel(pos_ref, c_ref, kr_ref, g_ref, wk_ref, wv_ref, cos_ref, sa_ref, sb_ref, k_ref, v_ref):
    del pos_ref
    for rs in _sub_blocks(c_ref.shape[0], SUB_ROWS):
        xn = _rms(c_ref[rs, :], g_ref[...]).astype(BF16)
        kn = jnp.dot(xn, wk_ref[...], preferred_element_type=F32)
        _store_vt(v_ref, rs, jnp.dot(xn, wv_ref[...], preferred_element_type=F32), A_V)
        kr = _rope(kr_ref[rs, :], cos_ref[rs, :], sa_ref[rs, :], sb_ref[rs, :]).astype(k_ref.dtype)
        for hh in range(kn.shape[1] // A_NOPE):
            k_ref[rs, hh * A_QK_PAD:hh * A_QK_PAD + A_NOPE] = kn[:, hh * A_NOPE:(hh + 1) * A_NOPE].astype(k_ref.dtype)
            k_ref[rs, hh * A_QK_PAD + A_NOPE:(hh + 1) * A_QK_PAD] = kr


def _mla_kv(cqkv, gain, wk, wv, tabs, tile_pos, tm):
    n = cqkv.shape[0]
    hpt = 4
    n_ct = A_HEADS // hpt
    kr_blk = (A_Q_RANK + A_KV_RANK) // LANES
    tab_spec = pl.BlockSpec((tm, LANES), lambda i, j, pos: (pos[i], 0))
    return pl.pallas_call(
        _mla_kv_kernel,
        grid_spec=pltpu.PrefetchScalarGridSpec(
            num_scalar_prefetch=1,
            grid=(n // tm, n_ct),
            in_specs=[pl.BlockSpec((tm, A_KV_RANK), lambda i, j, pos: (i, 1)),
                      pl.BlockSpec((tm, LANES), lambda i, j, pos: (i, kr_blk)),
                      pl.BlockSpec((1, A_KV_RANK), lambda i, j, pos: (0, 0)),
                      pl.BlockSpec((A_KV_RANK, hpt * A_NOPE), lambda i, j, pos: (0, j)),
                      pl.BlockSpec((A_KV_RANK, hpt * A_V), lambda i, j, pos: (0, j)),
                      tab_spec, tab_spec, tab_spec],
            out_specs=[pl.BlockSpec((tm, hpt * A_QK_PAD), lambda i, j, pos: (i, j)),
                       pl.BlockSpec((hpt * (A_V + ONES_ROWS), tm), lambda i, j, pos: (j, i))]),
        out_shape=[jax.ShapeDtypeStruct((n, A_HEADS * A_QK_PAD), BF16),
                   jax.ShapeDtypeStruct((A_HEADS * (A_V + ONES_ROWS), n), BF16)],
        compiler_params=_params("arbitrary", "arbitrary"),
        name="mla_kv",
    )(tile_pos, cqkv, cqkv, gain, wk, wv, *tabs)


def _flash_kernel(q_ref, qn_ref, k_ref, vt_ref, o_ref, q_sc, qn_sc, m_sc, acc_sc, s_sc, *, group, dq, dv, tk, c):
    tq = q_ref.shape[0]
    n_kv = k_ref.shape[0] // tk
    for g in range(group):
        q_sc[g * tq:(g + 1) * tq, :] = (q_ref[:, g * dq:(g + 1) * dq].astype(F32) * c).astype(BF16)
        qn_sc[g * tq:(g + 1) * tq, :] = (qn_ref[:, g * dq:(g + 1) * dq].astype(F32) * c).astype(BF16)
    m_sc[...] = jnp.full(m_sc.shape, -1e30, F32)
    acc_sc[...] = jnp.zeros(acc_sc.shape, F32)

    def scores(j, queries):
        k = k_ref[pl.ds(pl.multiple_of(j * tk, tk), tk), :]
        return lax.dot_general(k, queries[...], (((1,), (1,)), ((), ())), preferred_element_type=F32)

    def step(j, slot, prefetch):
        s = s_sc[slot]
        if prefetch:
            s_sc[1 - slot] = scores(j + 1, q_sc)
        else:
            s_sc[1 - slot] = scores(0, qn_sc)
        vt = vt_ref[:, pl.ds(pl.multiple_of(j * tk, tk), tk)]
        m_prev = m_sc[...]
        m_new = jnp.maximum(m_prev, jnp.max(s, axis=0, keepdims=True))
        alpha = jnp.exp2(m_prev - m_new)
        p = jnp.exp2(s - m_new).astype(BF16)
        acc_sc[...] = alpha * acc_sc[...] + jnp.dot(vt, p, preferred_element_type=F32)
        m_sc[...] = m_new

    @pl.when(pl.program_id(2) == 0)
    def _():
        s_sc[0] = scores(0, q_sc)

    def pair(jj, carry):
        step(2 * jj, 0, True)
        step(2 * jj + 1, 1, True)
        return carry

    lax.fori_loop(0, n_kv // 2 - 1, pair, 0)
    step(n_kv - 2, 0, True)
    step(n_kv - 1, 1, False)
    acc = acc_sc[...]
    out = (acc[:dv] / acc[dv:dv + 1]).T
    for g in range(group):
        o_ref[:, g * dv:(g + 1) * dv] = out[g * tq:(g + 1) * tq, :].astype(o_ref.dtype)


def _flash_into_kernel(q_ref, qn_ref, k_ref, vt_ref, prev_ref, o_ref, *scratch, **kw):
    del prev_ref
    _flash_kernel(q_ref, qn_ref, k_ref, vt_ref, o_ref, *scratch, **kw)


def _flash(q2d, k2d, vt2d, out, *, batch, seq, row0, n_kv_heads, group, dq, dv, k_col0, scale, tq, tk):
    assert row0 % seq == 0 and k_col0 % dq == 0
    n = q2d.shape[0]
    tq = _tile(seq, tq)
    tk = _tile(seq // 2, tk)
    qb0, kb0 = row0 // tq, row0 // seq
    nq = seq // tq
    kc0 = k_col0 // dq
    rows = group * tq
    dve = dv + ONES_ROWS
    kw = dict(group=group, dq=dq, dv=dv, tk=tk, c=scale * LOG2E)
    in_specs = [pl.BlockSpec((tq, group * dq), lambda b, h, i: (qb0 + b * nq + i, h)),
                pl.BlockSpec((tq, group * dq), lambda b, h, i: (qb0 + b * nq + jnp.minimum(i + 1, nq - 1), h)),
                pl.BlockSpec((seq, dq), lambda b, h, i: (kb0 + b, kc0 + h)),
                pl.BlockSpec((dve, seq), lambda b, h, i: (h, kb0 + b))]
    if out is None:
        kern, operands, aliases = functools.partial(_flash_kernel, **kw), (q2d, q2d, k2d, vt2d), {}
    else:
        kern, operands, aliases = functools.partial(_flash_into_kernel, **kw), (q2d, q2d, k2d, vt2d, out), {4: 0}
        in_specs.append(pl.BlockSpec(memory_space=pl.ANY))
    return pl.pallas_call(
        kern,
        grid=(batch, n_kv_heads, nq),
        in_specs=in_specs,
        out_specs=pl.BlockSpec((tq, group * dv), lambda b, h, i: (qb0 + b * nq + i, h)),
        out_shape=jax.ShapeDtypeStruct((n, n_kv_heads * group * dv), BF16),
        scratch_shapes=[pltpu.VMEM((rows, dq), BF16), pltpu.VMEM((rows, dq), BF16), pltpu.VMEM((1, rows), F32),
                        pltpu.VMEM((dve, rows), F32), pltpu.VMEM((2, tk, rows), F32)],
        input_output_aliases=aliases,
        compiler_params=_params("arbitrary", "arbitrary", "arbitrary"),
        name="flash",
    )(*operands)


def _gate_mix_kernel(h_ref, oa_ref, ob_ref, wga_ref, wgb_ref, wa_ref, wb_ref, o_ref):
    for rs in _sub_blocks(h_ref.shape[0], 2 * SUB_ROWS):
        h = h_ref[rs, :]
        ga = jax.nn.sigmoid(jnp.dot(h, wga_ref[...], preferred_element_type=F32))
        gb = jax.nn.sigmoid(jnp.dot(h, wgb_ref[...], preferred_element_type=F32))
        ya = jnp.dot(oa_ref[rs, :], wa_ref[...], preferred_element_type=F32)
        yb = jnp.dot(ob_ref[rs, :], wb_ref[...], preferred_element_type=F32)
        o_ref[rs, :] = (ga * ya + gb * yb).astype(o_ref.dtype)


def _gate_mix(h, oa, ob, w_gates, wa, wb, tm, tn):
    n, d = oa.shape
    dm = wa.shape[1]
    nct = dm // tn
    rows = pl.BlockSpec((tm, d), lambda i, j: (i, 0))
    cols = pl.BlockSpec((d, tn), lambda i, j: (0, j))
    return pl.pallas_call(
        _gate_mix_kernel,
        grid=(n // tm, nct),
        in_specs=[rows, rows, rows, cols, pl.BlockSpec((d, tn), lambda i, j: (0, nct + j)), cols, cols],
        out_specs=pl.BlockSpec((tm, tn), lambda i, j: (i, j)),
        out_shape=jax.ShapeDtypeStruct((n, dm), BF16),
        compiler_params=_params("arbitrary", "arbitrary"),
        name="gate_mix",
    )(h, oa, ob, w_gates, w_gates, wa, wb)


def _layer_norm(y, g, b):
    mu = jnp.mean(y, axis=-1, keepdims=True)
    yc = y - mu
    var = jnp.mean(yc * yc, axis=-1, keepdims=True)
    return yc * lax.rsqrt(var + LN_EPS) * g + b


def _route(logits):
    lane = lax.broadcasted_iota(jnp.int32, logits.shape, 1).astype(F32)
    big = float(LANES)
    ninf = -jnp.inf
    gmask = lane < N_GROUPS
    gl = jnp.where(gmask, logits, ninf)
    gmax = jnp.max(gl, axis=-1, keepdims=True)
    grp = jnp.min(jnp.where(gl == gmax, lane, big), axis=-1, keepdims=True)
    den = jnp.sum(jnp.exp(gl - gmax), axis=-1, keepdims=True)
    p_grp = 1.0 / den
    lo = N_GROUPS + grp * EXPERTS_PER_GROUP
    el = jnp.where(lane >= lo, jnp.where(lane < lo + EXPERTS_PER_GROUP, logits, ninf), ninf)
    t1 = jnp.max(el, axis=-1, keepdims=True)
    i1 = jnp.min(jnp.where(el == t1, lane, big), axis=-1, keepdims=True)
    el2 = jnp.where(lane == i1, ninf, el)
    t2 = jnp.max(el2, axis=-1, keepdims=True)
    i2 = jnp.min(jnp.where(el2 == t2, lane, big), axis=-1, keepdims=True)
    e = jnp.exp(t2 - t1)
    w1 = p_grp / (1.0 + e)
    w2 = p_grp * e / (1.0 + e)
    out = jnp.where(lane == 0, w1, jnp.where(lane == 1, w2, jnp.where(lane == 2, i1 - N_GROUPS, i2 - N_GROUPS)))
    return jnp.where(lane < 4, out, 0.0)


def _ln1_kernel(seg_ref, g_ref, *refs, bounds):
    del seg_ref
    *x_refs, mod_ref, w_ref, lg_ref, lb_ref, wr_ref, br_ref, x1_ref, h2_ref, rt_ref = refs
    i = pl.program_id(0)
    g1 = mod_ref[0, 2:3, :]
    sh2 = mod_ref[0, 3:4, :]
    sc2 = mod_ref[0, 4:5, :]
    for rs in _sub_blocks(g_ref.shape[0], SUB_ROWS):
        attn = jnp.dot(g_ref[rs, :], w_ref[...], preferred_element_type=F32)
        x = _live_rows(i, x_refs, bounds, rs)
        x1 = _layer_norm(DEEPNORM_ALPHA * x + g1 * attn, lg_ref[...], lb_ref[...])
        x1_ref[rs, :] = x1
        h2 = x1 * (1.0 + sc2) + sh2
        h2_ref[rs, :] = h2
        h_hi = h2.astype(BF16)
        h_lo = (h2 - h_hi.astype(F32)).astype(BF16)
        a = jnp.dot(h_hi, wr_ref[...], preferred_element_type=F32)
        b = jnp.dot(h_lo, wr_ref[:, :LANES], preferred_element_type=F32)
        rt_ref[rs, :] = _route(a[:, :LANES] + a[:, LANES:] + b + br_ref[...])


def _ln1(gated, xs, mod, tile_seg, w_out, lg, lb, wr, br, tm):
    n, d = gated.shape
    row = lambda i, seg: (i, 0)
    const = lambda i, seg: (0, 0)
    x_specs, bounds = _group_specs(xs, tm)
    return pl.pallas_call(
        functools.partial(_ln1_kernel, bounds=bounds),
        grid_spec=pltpu.PrefetchScalarGridSpec(
            num_scalar_prefetch=1,
            grid=(n // tm,),
            in_specs=[pl.BlockSpec((tm, d), row)] + x_specs + [
                      pl.BlockSpec((1, 8, d), lambda i, seg: (seg[i], 0, 0)),
                      pl.BlockSpec((d, d), const),
                      pl.BlockSpec((1, d), const),
                      pl.BlockSpec((1, d), const),
                      pl.BlockSpec((d, 2 * LANES), const),
                      pl.BlockSpec((1, LANES), const)],
            out_specs=[pl.BlockSpec((tm, d), row), pl.BlockSpec((tm, d), row), pl.BlockSpec((tm, LANES), row)]),
        out_shape=[jax.ShapeDtypeStruct((n, d), F32), jax.ShapeDtypeStruct((n, d), F32),
                   jax.ShapeDtypeStruct((n, LANES), F32)],
        compiler_params=_params("arbitrary"),
        name="ln1_route",
    )(tile_seg, gated, *xs, mod, w_out, lg, lb, wr, br)


def _row_gather_start(src_hbm, dst, sem, idx, row):
    pltpu.make_async_copy(src_hbm.at[pl.ds(idx, 1), :], dst.at[pl.ds(row, 1), :], sem).start()


def _row_gather_wait(src_hbm, dst, sem, row):
    pltpu.make_async_copy(src_hbm.at[pl.ds(0, 1), :], dst.at[pl.ds(row, 1), :], sem).wait()


def _gather_rows_start(src_hbm, dst, sem, idx_ref, idx0, rows):
    for r in range(rows):
        _row_gather_start(src_hbm, dst, sem, idx_ref[0, 0, idx0 + r], r)


def _gather_rows_wait(src_hbm, dst, sem, rows):
    def body(r, carry):
        _row_gather_wait(src_hbm, dst, sem, r)
        return carry

    lax.fori_loop(0, rows, body, 0, unroll=GATHER_UNROLL)


def _expert_kernel(blk_e_ref, n_used_ref, tok_ref, tok_next_ref, h2_hbm, wg_ref, wu_ref, wd_ref, o_ref,
                   xbuf, wg_bf, wu_bf, wd_bf, sem):
    i = pl.program_id(0)
    n_used = n_used_ref[0]
    e = blk_e_ref[i]
    prev = blk_e_ref[jnp.maximum(i - 1, 0)]
    rows = xbuf.shape[1]
    slot = i % 2

    @pl.when(i == 0)
    def _():
        _gather_rows_start(h2_hbm, xbuf.at[0], sem.at[0], tok_ref, 0, rows)

    @pl.when(i + 1 < n_used)
    def _():
        _gather_rows_start(h2_hbm, xbuf.at[1 - slot], sem.at[1 - slot], tok_next_ref, 0, rows)

    @pl.when(jnp.logical_or(i == 0, e != prev))
    def _():
        wg_bf[...] = wg_ref[0].astype(BF16)
        wu_bf[...] = wu_ref[0].astype(BF16)
        wd_bf[...] = wd_ref[0].astype(BF16)

    @pl.when(i < n_used)
    def _():
        _gather_rows_wait(h2_hbm, xbuf.at[slot], sem.at[slot], rows)
        for rs in _sub_blocks(rows, SUB_ROWS):
            x = xbuf[slot, rs, :].astype(BF16)
            gate = jnp.dot(x, wg_bf[...], preferred_element_type=F32)
            up = jnp.dot(x, wu_bf[...], preferred_element_type=F32)
            act = (gate * jax.nn.sigmoid(gate) * up).astype(BF16)
            o_ref[rs, :] = jnp.dot(act, wd_bf[...], preferred_element_type=F32)

    @pl.when(i >= n_used)
    def _():
        o_ref[...] = jnp.zeros_like(o_ref)


def _experts(h2, tok_blocks, blk_e, n_used, w_gate, w_up, w_down):
    n_blk = tok_blocks.shape[0]
    rows = tok_blocks.shape[2]
    d = h2.shape[1]
    de = w_gate.shape[2]
    return pl.pallas_call(
        _expert_kernel,
        grid_spec=pltpu.PrefetchScalarGridSpec(
            num_scalar_prefetch=2,
            grid=(n_blk,),
            in_specs=[pl.BlockSpec((1, 1, rows), lambda i, be, nu: (i, 0, 0), memory_space=pltpu.SMEM),
                      pl.BlockSpec((1, 1, rows), lambda i, be, nu: (jnp.minimum(i + 1, n_blk - 1), 0, 0),
                                   memory_space=pltpu.SMEM),
                      pl.BlockSpec(memory_space=pl.ANY),
                      pl.BlockSpec((1, d, de), lambda i, be, nu: (be[i], 0, 0)),
                      pl.BlockSpec((1, d, de), lambda i, be, nu: (be[i], 0, 0)),
                      pl.BlockSpec((1, de, d), lambda i, be, nu: (be[i], 0, 0))],
            out_specs=pl.BlockSpec((rows, d), lambda i, be, nu: (i, 0)),
            scratch_shapes=[pltpu.VMEM((2, rows, d), F32),
                            pltpu.VMEM((d, de), BF16),
                            pltpu.VMEM((d, de), BF16),
                            pltpu.VMEM((de, d), BF16),
                            pltpu.SemaphoreType.DMA((2,))]),
        out_shape=jax.ShapeDtypeStruct((n_blk * rows, d), F32),
        compiler_params=_params("arbitrary"),
        name="experts",
    )(blk_e, n_used, tok_blocks, tok_blocks, h2, w_gate, w_up, w_down)


def _final_kernel(seg_ref, pos_ref, pos_next_ref, x1_ref, mod_ref, rt_ref, lg_ref, lb_ref, ys_hbm, o_ref, buf, sem):
    del seg_ref
    i = pl.program_id(0)
    tm = x1_ref.shape[0]
    slot = i % 2

    def start_tile(idx_ref, s):
        for k in range(TOP_K):
            _gather_rows_start(ys_hbm, buf.at[s, k], sem.at[s], idx_ref, k * tm, tm)

    @pl.when(i == 0)
    def _():
        start_tile(pos_ref, 0)

    @pl.when(i + 1 < pl.num_programs(0))
    def _():
        start_tile(pos_next_ref, 1 - slot)

    for k in range(TOP_K):
        _gather_rows_wait(ys_hbm, buf.at[slot, k], sem.at[slot], tm)
    g2 = mod_ref[0, 5:6, :]
    for rs in _sub_blocks(tm, SUB_ROWS):
        rt = rt_ref[rs, :]
        ffn = rt[:, 0:1] * buf[slot, 0, rs, :] + rt[:, 1:2] * buf[slot, 1, rs, :]
        o_ref[rs, :] = _layer_norm(DEEPNORM_ALPHA * x1_ref[rs, :] + g2 * ffn, lg_ref[...], lb_ref[...])


def _final(x1, mod, tile_seg, pos_blocks, rt, lg, lb, ys, tile0, n_tiles, tm):
    d = x1.shape[1]
    row = lambda i, seg: (tile0 + i, 0)
    const = lambda i, seg: (0, 0)
    return pl.pallas_call(
        _final_kernel,
        grid_spec=pltpu.PrefetchScalarGridSpec(
            num_scalar_prefetch=1,
            grid=(n_tiles,),
            in_specs=[pl.BlockSpec((1, 1, TOP_K * tm), lambda i, seg: (tile0 + i, 0, 0), memory_space=pltpu.SMEM),
                      pl.BlockSpec((1, 1, TOP_K * tm),
                                   lambda i, seg: (tile0 + jnp.minimum(i + 1, n_tiles - 1), 0, 0),
                                   memory_space=pltpu.SMEM),
                      pl.BlockSpec((tm, d), row),
                      pl.BlockSpec((1, 8, d), lambda i, seg: (seg[tile0 + i], 0, 0)),
                      pl.BlockSpec((tm, LANES), row),
                      pl.BlockSpec((1, d), const),
                      pl.BlockSpec((1, d), const),
                      pl.BlockSpec(memory_space=pl.ANY)],
            out_specs=pl.BlockSpec((tm, d), lambda i, seg: (i, 0)),
            scratch_shapes=[pltpu.VMEM((2, TOP_K, tm, d), F32), pltpu.SemaphoreType.DMA((2,))]),
        out_shape=jax.ShapeDtypeStruct((n_tiles * tm, d), F32),
        compiler_params=_params("arbitrary"),
        name="combine_ln2",
    )(tile_seg, pos_blocks, pos_blocks, x1, mod, rt, lg, lb, ys)


def _rope_tables(pos_a, pos_b):
    inv = ROPE_THETA ** (-jnp.arange(0, 64, 2, dtype=F32) / 64)

    def half(pos):
        if pos is None:
            z = jnp.zeros((pos_a.shape[0], 64), F32)
            return z, z, z
        ang = pos.astype(F32)[:, None] * inv[None, :]
        cos, sin = jnp.cos(ang), jnp.sin(ang)
        zero = jnp.zeros_like(sin)
        return (jnp.concatenate([cos, cos], 1), jnp.concatenate([-sin, zero], 1), jnp.concatenate([zero, sin], 1))

    a, b = half(pos_a), half(pos_b)
    return tuple(jnp.concatenate([u, v], 1) for u, v in zip(a, b))


def _dispatch(eid, rows):
    n = eid.shape[0]
    a = n * TOP_K
    i32 = jnp.int32
    iota = jnp.arange(a, dtype=i32)
    experts = jnp.arange(N_EXPERTS, dtype=i32)
    e_sorted, order = lax.sort((eid.reshape(-1), iota), num_keys=1, is_stable=True)
    starts = jnp.sum((e_sorted[None, :] < experts[:, None]).astype(i32), axis=1)
    counts = jnp.diff(starts, append=a)
    padded = ((counts + rows - 1) // rows) * rows
    pad_end = jnp.cumsum(padded)
    pad_start = pad_end - padded
    gap_step = jnp.diff(pad_start - starts, prepend=0)
    dest = iota + jnp.sum(jnp.where(e_sorted[:, None] >= experts[None, :], gap_step[None, :], 0), axis=1)
    _, pos = lax.sort((order, dest), num_keys=1)
    p_rows = ((a + rows - 1) // rows) * rows + N_EXPERTS * rows
    n_blk = p_rows // rows
    blk = jnp.arange(n_blk, dtype=i32)
    blk_e = jnp.minimum(jnp.sum((blk[:, None] * rows >= pad_end[None, :]).astype(i32), axis=1), N_EXPERTS - 1)
    onehot = blk_e[:, None] == experts[None, :]
    src0 = jnp.sum(jnp.where(onehot, (starts - pad_start)[None, :], 0), axis=1) + blk * rows
    src0 = jnp.minimum(src0, a)
    order_ext = jnp.concatenate([order, jnp.zeros((rows,), i32)])
    tok = jnp.take(order_ext, src0[:, None] + jnp.arange(rows, dtype=i32)[None, :], axis=0) // TOP_K
    n_used = (pad_end[-1] // rows).astype(i32).reshape(1)
    return tok.reshape(n_blk, 1, rows), pos.reshape(n, TOP_K), blk_e.astype(i32), n_used


def _layer(groups, w_ada, b_ada, w_in, a_q_norm, a_kv_norm, a_w_uq, a_w_ukv, a_w_o, b_q_norm, b_k_norm, b_w_o,
           w_out, ln1_g, ln1_b, w_group, b_group, w_expert, b_expert, e_w_gate, e_w_up, e_w_down, ln2_g, ln2_b):
    d = w_in.shape[0]
    seqs = [x.shape[1] for x, _ in groups]
    seg = functools.reduce(math.gcd, seqs)
    tm = _tile(seg, 512)
    s_max = max(seqs)

    xs = [x.reshape(-1, d) for x, _ in groups]
    n = sum(x.shape[0] for x in xs)
    c_all = jnp.concatenate([c for _, c in groups], axis=0)
    nb = c_all.shape[0]
    c_pad = jnp.zeros((-(-nb // 8) * 8, d), F32).at[:nb].set(c_all)
    seg_batch, tile_seg, tile_pos = [], [], []
    b0 = 0
    for x, _ in groups:
        bsz, s = x.shape[0], x.shape[1]
        for b in range(bsz):
            for t in range(s // tm):
                tile_seg.append(b0 + b)
                tile_pos.append(t)
        b0 += bsz
    tile_seg = jnp.asarray(np.asarray(tile_seg, np.int32))
    tile_pos = jnp.asarray(np.asarray(tile_pos, np.int32))

    mod = _ada(c_pad, w_ada, b_ada)[:nb].reshape(nb, 6, d)
    mod = jnp.concatenate([mod, jnp.zeros((nb, 2, d), F32)], axis=1)

    o_q, o_kv, o_kr, o_qb, o_kb, o_vb, o_ga, o_gb = np.cumsum(
        [0, A_Q_RANK, A_KV_RANK, A_ROPE, B_HEADS * B_HEAD_DIM, B_KV_HEADS * B_HEAD_DIM,
         B_KV_HEADS * B_HEAD_DIM, d]).tolist()
    w_lat = jnp.pad(w_in[:, :o_qb], ((0, 0), (0, o_kr + LANES - o_qb))).astype(BF16)
    w_qk = w_in[:, o_qb:o_vb].astype(BF16)
    w_v = w_in[:, o_vb:o_ga].astype(BF16)
    w_g = w_in[:, o_ga:].astype(BF16)
    w_uq = jnp.pad(a_w_uq.reshape(A_Q_RANK, A_HEADS, A_NOPE + A_ROPE),
                   ((0, 0), (0, 0), (0, A_QK_PAD - A_NOPE - A_ROPE))).reshape(A_Q_RANK, A_HEADS * A_QK_PAD).astype(BF16)
    w_ukv = a_w_ukv.reshape(A_KV_RANK, A_HEADS, A_NOPE + A_V)
    w_uk = w_ukv[:, :, :A_NOPE].reshape(A_KV_RANK, A_HEADS * A_NOPE).astype(BF16)
    w_uv = w_ukv[:, :, A_NOPE:].reshape(A_KV_RANK, A_HEADS * A_V).astype(BF16)
    w_r = jnp.zeros((d, LANES), F32).at[:, :N_GROUPS].set(w_group).at[:, N_GROUPS:N_GROUPS + N_EXPERTS].set(w_expert)
    b_r = jnp.zeros((1, LANES), F32).at[0, :N_GROUPS].set(b_group).at[0, N_GROUPS:N_GROUPS + N_EXPERTS].set(b_expert)
    w_r_hi = w_r.astype(BF16)
    w_r = jnp.concatenate([w_r_hi, (w_r - w_r_hi.astype(F32)).astype(BF16)], axis=1)

    t = jnp.arange(s_max, dtype=jnp.int32)
    tabs_a = _rope_tables(t, None)
    tabs_b = _rope_tables(t // GRID_W, t % GRID_W)

    h, lat = _mod_proj(xs, mod, tile_seg, w_lat, tm)
    qk_b = _proj_qk(h, w_qk, jnp.stack([b_q_norm, b_k_norm]).reshape(2, 1, B_HEAD_DIM), tabs_b, tile_pos, tm,
                    B_HEADS // B_GROUP)
    v_b = _proj_vt(h, w_v, tm, 512)
    q_a = _mla_q(lat, a_q_norm.reshape(1, -1), w_uq, tabs_a, tile_pos, tm)
    k_a, v_a = _mla_kv(lat, a_kv_norm.reshape(1, -1), w_uk, w_uv, tabs_a, tile_pos, tm)

    o_a = o_b = None
    row0 = 0
    for x, _ in groups:
        bsz, s = x.shape[0], x.shape[1]
        o_a = _flash(q_a, k_a, v_a, o_a, batch=bsz, seq=s, row0=row0, n_kv_heads=A_HEADS, group=1,
                     dq=A_QK_PAD, dv=A_V, k_col0=0, scale=(A_NOPE + A_ROPE) ** -0.5, tq=1024, tk=1024)
        o_b = _flash(qk_b, qk_b, v_b, o_b, batch=bsz, seq=s, row0=row0, n_kv_heads=B_KV_HEADS, group=B_GROUP,
                     dq=B_HEAD_DIM, dv=B_HEAD_DIM, k_col0=B_HEADS * B_HEAD_DIM, scale=B_HEAD_DIM ** -0.5,
                     tq=256, tk=1024)
        row0 += bsz * s

    gated = _gate_mix(h, o_a, o_b, w_g, a_w_o.astype(BF16), b_w_o.astype(BF16), tm, 512)
    tm2 = _tile(seg, 256)
    seg2 = jnp.repeat(tile_seg, tm // tm2)
    x1, h2, rt = _ln1(gated, xs, mod, seg2, w_out.astype(BF16), ln1_g.reshape(1, d), ln1_b.reshape(1, d),
                      w_r, b_r, tm2)

    eid = rt[:, 2:4].astype(jnp.int32)
    tok_blocks, pos, blk_e, n_used = _dispatch(eid, EXPERT_ROWS)
    ys = _experts(h2, tok_blocks, blk_e, n_used, e_w_gate, e_w_up, e_w_down)
    pos_blocks = pos.reshape(n // tm2, tm2, TOP_K).transpose(0, 2, 1).reshape(n // tm2, 1, TOP_K * tm2)

    outs = []
    tile0 = 0
    for x, _ in groups:
        bsz, s = x.shape[0], x.shape[1]
        nt = bsz * s // tm2
        y = _final(x1, mod, seg2, pos_blocks, rt, ln2_g.reshape(1, d), ln2_b.reshape(1, d), ys, tile0, nt, tm2)
        outs.append(y.reshape(bsz, s, d))
        tile0 += nt
    return tuple(outs)


def kernel(x_prompt, x_sample, c_prompt, c_sample, w_ada, b_ada, w_in, a_q_norm, a_kv_norm, a_w_uq, a_w_ukv, a_w_o,
           b_q_norm, b_k_norm, b_w_o, w_out, ln1_g, ln1_b, w_group, b_group, w_expert, b_expert, e_w_gate, e_w_up,
           e_w_down, ln2_g, ln2_b):
    weights = (w_ada, b_ada, w_in, a_q_norm, a_kv_norm, a_w_uq, a_w_ukv, a_w_o, b_q_norm, b_k_norm, b_w_o, w_out,
               ln1_g, ln1_b, w_group, b_group, w_expert, b_expert, e_w_gate, e_w_up, e_w_down, ln2_g, ln2_b)
    return _layer(((x_prompt, c_prompt), (x_sample, c_sample)), *[w[0] for w in weights])
```

```python
import functools
import math

import numpy as np
import jax
import jax.numpy as jnp
from jax import lax
from jax.experimental import pallas as pl
from jax.experimental.pallas import tpu as pltpu

F32 = jnp.float32
BF16 = jnp.bfloat16
HIGHEST = lax.Precision.HIGHEST

GRID_W = 64
ROPE_THETA = 10000.0
A_HEADS = 16
A_Q_RANK = 512
A_KV_RANK = 512
A_NOPE = 128
A_ROPE = 64
A_V = 128
A_QK_PAD = 256
B_HEADS = 16
B_KV_HEADS = 4
B_GROUP = B_HEADS // B_KV_HEADS
B_HEAD_DIM = 128
N_GROUPS = 8
EXPERTS_PER_GROUP = 8
N_EXPERTS = N_GROUPS * EXPERTS_PER_GROUP
TOP_K = 2
RMS_EPS = 1e-6
LN_EPS = 1e-5
DEPTH = 1
DEEPNORM_ALPHA = (2 * DEPTH) ** 0.25
LOG2E = 1.4426950408889634

LANES = 128
EXPERT_ROWS = 256
SUB_ROWS = 128
GATHER_UNROLL = 8
ONES_ROWS = 16
VMEM_LIMIT = 48 * 1024 * 1024


def _params(*sem):
    return pltpu.CompilerParams(dimension_semantics=sem, vmem_limit_bytes=VMEM_LIMIT)


def _tile(n, pref):
    t = min(n, pref)
    while n % t:
        t //= 2
    return t


def _sub_blocks(rows, sub):
    sub = _tile(rows, sub)
    return [slice(r, r + sub) for r in range(0, rows, sub)]


def _ada_kernel(c_ref, w_ref, b_ref, o_ref):
    c = c_ref[...]
    a = c * jax.nn.sigmoid(c)
    o_ref[...] = jnp.dot(a, w_ref[...], precision=HIGHEST, preferred_element_type=F32) + b_ref[...]


def _ada(c_pad, w_ada, b_ada):
    d, n = w_ada.shape
    tn = _tile(n, 1024)
    return pl.pallas_call(
        _ada_kernel,
        grid=(n // tn,),
        in_specs=[pl.BlockSpec(c_pad.shape, lambda j: (0, 0)),
                  pl.BlockSpec((d, tn), lambda j: (0, j)),
                  pl.BlockSpec((1, tn), lambda j: (0, j))],
        out_specs=pl.BlockSpec((c_pad.shape[0], tn), lambda j: (0, j)),
        out_shape=jax.ShapeDtypeStruct((c_pad.shape[0], n), F32),
        compiler_params=_params("arbitrary"),
        name="ada",
    )(c_pad, w_ada, b_ada.reshape(1, n))


def _group_specs(xs, tm):
    specs, bounds = [], []
    t0 = 0
    for x in xs:
        nt = x.shape[0] // tm
        specs.append(pl.BlockSpec((tm, x.shape[1]), lambda i, *_, t0=t0, nt=nt: (jnp.clip(i - t0, 0, nt - 1), 0)))
        t0 += nt
        bounds.append(t0)
    return specs, tuple(bounds[:-1])


def _live_rows(i, refs, bounds, rs):
    val = refs[-1][rs, :]
    for g in reversed(range(len(refs) - 1)):
        val = jnp.where(i < bounds[g], refs[g][rs, :], val)
    return val


def _mod_proj_kernel(seg_ref, *refs, bounds):
    del seg_ref
    *x_refs, mod_ref, w_ref, h_ref, o_ref = refs
    sh = mod_ref[0, 0:1, :]
    sc = mod_ref[0, 1:2, :]
    x = _live_rows(pl.program_id(0), x_refs, bounds, slice(None))
    h = (x * (1.0 + sc) + sh).astype(BF16)
    h_ref[...] = h
    o_ref[...] = jnp.dot(h, w_ref[...], preferred_element_type=F32)


def _mod_proj(xs, mod, tile_seg, w, tm):
    n = sum(x.shape[0] for x in xs)
    d = xs[0].shape[1]
    nc = w.shape[1]
    x_specs, bounds = _group_specs(xs, tm)
    return pl.pallas_call(
        functools.partial(_mod_proj_kernel, bounds=bounds),
        grid_spec=pltpu.PrefetchScalarGridSpec(
            num_scalar_prefetch=1,
            grid=(n // tm,),
            in_specs=x_specs + [pl.BlockSpec((1, 8, d), lambda i, seg: (seg[i], 0, 0)),
                                pl.BlockSpec((d, nc), lambda i, seg: (0, 0))],
            out_specs=[pl.BlockSpec((tm, d), lambda i, seg: (i, 0)),
                       pl.BlockSpec((tm, nc), lambda i, seg: (i, 0))]),
        out_shape=[jax.ShapeDtypeStruct((n, d), BF16), jax.ShapeDtypeStruct((n, nc), F32)],
        compiler_params=_params("arbitrary"),
        name="mod_proj",
    )(tile_seg, *xs, mod, w)


def _rope(x, cos, sin_a, sin_b):
    return x * cos + pltpu.roll(x, 96, 1) * sin_a + pltpu.roll(x, 32, 1) * sin_b


def _store_vt(vt_ref, cols, v, dv):
    vt = v.T.astype(vt_ref.dtype)
    dve = dv + ONES_ROWS
    for hh in range(v.shape[1] // dv):
        vt_ref[hh * dve:hh * dve + dv, cols] = vt[hh * dv:(hh + 1) * dv, :]
        vt_ref[hh * dve + dv:(hh + 1) * dve, cols] = jnp.ones((ONES_ROWS, vt.shape[1]), vt_ref.dtype)


def _proj_vt_kernel(h_ref, w_ref, o_ref):
    for rs in _sub_blocks(h_ref.shape[0], SUB_ROWS):
        _store_vt(o_ref, rs, jnp.dot(h_ref[rs, :], w_ref[...], preferred_element_type=F32), B_HEAD_DIM)


def _proj_qk_kernel(pos_ref, h_ref, w_ref, g_ref, cos_ref, sa_ref, sb_ref, o_ref, *, n_q_heads):
    del pos_ref
    for rs in _sub_blocks(h_ref.shape[0], SUB_ROWS):
        acc = jnp.dot(h_ref[rs, :], w_ref[...], preferred_element_type=F32)
        cos, sa, sb = cos_ref[rs, :], sa_ref[rs, :], sb_ref[rs, :]
        for hh in range(acc.shape[1] // B_HEAD_DIM):
            x = acc[:, hh * B_HEAD_DIM:(hh + 1) * B_HEAD_DIM]
            r = lax.rsqrt(jnp.mean(x * x, axis=-1, keepdims=True) + RMS_EPS)
            y = x * r * g_ref[0 if hh < n_q_heads else 1]
            o_ref[rs, hh * B_HEAD_DIM:(hh + 1) * B_HEAD_DIM] = _rope(y, cos, sa, sb).astype(o_ref.dtype)


def _proj_vt(h, w, tm, tn):
    n, d = h.shape
    nc = w.shape[1]
    tne = tn // B_HEAD_DIM * (B_HEAD_DIM + ONES_ROWS)
    return pl.pallas_call(
        _proj_vt_kernel,
        grid=(n // tm, nc // tn),
        in_specs=[pl.BlockSpec((tm, d), lambda i, j: (i, 0)),
                  pl.BlockSpec((d, tn), lambda i, j: (0, j))],
        out_specs=pl.BlockSpec((tne, tm), lambda i, j: (j, i)),
        out_shape=jax.ShapeDtypeStruct((nc // tn * tne, n), BF16),
        compiler_params=_params("arbitrary", "arbitrary"),
        name="proj_vt",
    )(h, w)


def _proj_qk(h, w, gains, tabs, tile_pos, tm, n_q_heads):
    n, d = h.shape
    nc = w.shape[1]
    tab_spec = pl.BlockSpec((tm, LANES), lambda i, pos: (pos[i], 0))
    return pl.pallas_call(
        functools.partial(_proj_qk_kernel, n_q_heads=n_q_heads),
        grid_spec=pltpu.PrefetchScalarGridSpec(
            num_scalar_prefetch=1,
            grid=(n // tm,),
            in_specs=[pl.BlockSpec((tm, d), lambda i, pos: (i, 0)),
                      pl.BlockSpec((d, nc), lambda i, pos: (0, 0)),
                      pl.BlockSpec((2, 1, LANES), lambda i, pos: (0, 0, 0)),
                      tab_spec, tab_spec, tab_spec],
            out_specs=pl.BlockSpec((tm, nc), lambda i, pos: (i, 0))),
        out_shape=jax.ShapeDtypeStruct((n, nc), BF16),
        compiler_params=_params("arbitrary"),
        name="proj_qk",
    )(tile_pos, h, w, gains, *tabs)


def _rms(x, g):
    return x * lax.rsqrt(jnp.mean(x * x, axis=-1, keepdims=True) + RMS_EPS) * g


def _mla_q_kernel(pos_ref, c_ref, g_ref, w_ref, cos_ref, sa_ref, sb_ref, o_ref):
    del pos_ref
    for rs in _sub_blocks(c_ref.shape[0], SUB_ROWS):
        xn = _rms(c_ref[rs, :], g_ref[...]).astype(BF16)
        acc = jnp.dot(xn, w_ref[...], preferred_element_type=F32)
        cos, sa, sb = cos_ref[rs, :], sa_ref[rs, :], sb_ref[rs, :]
        for hh in range(acc.shape[1] // A_QK_PAD):
            base = hh * A_QK_PAD
            o_ref[rs, base:base + A_NOPE] = acc[:, base:base + A_NOPE].astype(o_ref.dtype)
            o_ref[rs, base + A_NOPE:base + A_QK_PAD] = _rope(
                acc[:, base + A_NOPE:base + A_QK_PAD], cos, sa, sb).astype(o_ref.dtype)


def _mla_q(cqkv, gain, w, tabs, tile_pos, tm):
    n = cqkv.shape[0]
    nc = w.shape[1]
    tn = nc
    tab_spec = pl.BlockSpec((tm, LANES), lambda i, j, pos: (pos[i], 0))
    return pl.pallas_call(
        _mla_q_kernel,
        grid_spec=pltpu.PrefetchScalarGridSpec(
            num_scalar_prefetch=1,
            grid=(n // tm, nc // tn),
            in_specs=[pl.BlockSpec((tm, A_Q_RANK), lambda i, j, pos: (i, 0)),
                      pl.BlockSpec((1, A_Q_RANK), lambda i, j, pos: (0, 0)),
                      pl.BlockSpec((A_Q_RANK, tn), lambda i, j, pos: (0, j)),
                      tab_spec, tab_spec, tab_spec],
            out_specs=pl.BlockSpec((tm, tn), lambda i, j, pos: (i, j))),
        out_shape=jax.ShapeDtypeStruct((n, nc), BF16),
        compiler_params=_params("arbitrary", "arbitrary"),
        name="mla_q",
    )(tile_pos, cqkv, gain, w, *tabs)


def _mla_kv_kernel(pos_ref, c_ref, kr_ref, g_ref, wk_ref, wv_ref, cos_ref, sa_ref, sb_ref, k_ref, v_ref):
    del pos_ref
    for rs in _sub_blocks(c_ref.shape[0], SUB_ROWS):
        xn = _rms(c_ref[rs, :], g_ref[...]).astype(BF16)
        kn = jnp.dot(xn, wk_ref[...], preferred_element_type=F32)
        _store_vt(v_ref, rs, jnp.dot(xn, wv_ref[...], preferred_element_type=F32), A_V)
        kr = _rope(kr_ref[rs, :], cos_ref[rs, :], sa_ref[rs, :], sb_ref[rs, :]).astype(k_ref.dtype)
        for hh in range(kn.shape[1] // A_NOPE):
            k_ref[rs, hh * A_QK_PAD:hh * A_QK_PAD + A_NOPE] = kn[:, hh * A_NOPE:(hh + 1) * A_NOPE].astype(k_ref.dtype)
            k_ref[rs, hh * A_QK_PAD + A_NOPE:(hh + 1) * A_QK_PAD] = kr


def _mla_kv(cqkv, gain, wk, wv, tabs, tile_pos, tm):
    n = cqkv.shape[0]
    hpt = A_HEADS
    n_ct = A_HEADS // hpt
    kr_blk = (A_Q_RANK + A_KV_RANK) // LANES
    tab_spec = pl.BlockSpec((tm, LANES), lambda i, j, pos: (pos[i], 0))
    return pl.pallas_call(
        _mla_kv_kernel,
        grid_spec=pltpu.PrefetchScalarGridSpec(
            num_scalar_prefetch=1,
            grid=(n // tm, n_ct),
            in_specs=[pl.BlockSpec((tm, A_KV_RANK), lambda i, j, pos: (i, 1)),
                      pl.BlockSpec((tm, LANES), lambda i, j, pos: (i, kr_blk)),
                      pl.BlockSpec((1, A_KV_RANK), lambda i, j, pos: (0, 0)),
                      pl.BlockSpec((A_KV_RANK, hpt * A_NOPE), lambda i, j, pos: (0, j)),
                      pl.BlockSpec((A_KV_RANK, hpt * A_V), lambda i, j, pos: (0, j)),
                      tab_spec, tab_spec, tab_spec],
            out_specs=[pl.BlockSpec((tm, hpt * A_QK_PAD), lambda i, j, pos: (i, j)),
                       pl.BlockSpec((hpt * (A_V + ONES_ROWS), tm), lambda i, j, pos: (j, i))]),
        out_shape=[jax.ShapeDtypeStruct((n, A_HEADS * A_QK_PAD), BF16),
                   jax.ShapeDtypeStruct((A_HEADS * (A_V + ONES_ROWS), n), BF16)],
        compiler_params=_params("arbitrary", "arbitrary"),
        name="mla_kv",
    )(tile_pos, cqkv, cqkv, gain, wk, wv, *tabs)


def _flash_kernel(q_ref, qn_ref, k_ref, vt_ref, o_ref, q_sc, qn_sc, m_sc, acc_sc, s_sc, *, group, dq, dv, tk, c):
    tq = q_ref.shape[0]
    n_kv = k_ref.shape[0] // tk
    for g in range(group):
        q_sc[g * tq:(g + 1) * tq, :] = (q_ref[:, g * dq:(g + 1) * dq].astype(F32) * c).astype(BF16)
        qn_sc[g * tq:(g + 1) * tq, :] = (qn_ref[:, g * dq:(g + 1) * dq].astype(F32) * c).astype(BF16)
    m_sc[...] = jnp.full(m_sc.shape, -1e30, F32)
    acc_sc[...] = jnp.zeros(acc_sc.shape, F32)

    def scores(j, queries):
        k = k_ref[pl.ds(pl.multiple_of(j * tk, tk), tk), :]
        return lax.dot_general(k, queries[...], (((1,), (1,)), ((), ())), preferred_element_type=F32)

    def step(j, slot, prefetch):
        s = s_sc[slot]
        if prefetch:
            s_sc[1 - slot] = scores(j + 1, q_sc)
        else:
            s_sc[1 - slot] = scores(0, qn_sc)
        vt = vt_ref[:, pl.ds(pl.multiple_of(j * tk, tk), tk)]
        m_prev = m_sc[...]
        m_new = jnp.maximum(m_prev, jnp.max(s, axis=0, keepdims=True))
        alpha = jnp.exp2(m_prev - m_new)
        p = jnp.exp2(s - m_new).astype(BF16)
        acc_sc[...] = alpha * acc_sc[...] + jnp.dot(vt, p, preferred_element_type=F32)
        m_sc[...] = m_new

    @pl.when(pl.program_id(2) == 0)
    def _():
        s_sc[0] = scores(0, q_sc)

    def pair(jj, carry):
        step(2 * jj, 0, True)
        step(2 * jj + 1, 1, True)
        return carry

    lax.fori_loop(0, n_kv // 2 - 1, pair, 0)
    step(n_kv - 2, 0, True)
    step(n_kv - 1, 1, False)
    acc = acc_sc[...]
    out = (acc[:dv] / acc[dv:dv + 1]).T
    for g in range(group):
        o_ref[:, g * dv:(g + 1) * dv] = out[g * tq:(g + 1) * tq, :].astype(o_ref.dtype)


def _flash_into_kernel(q_ref, qn_ref, k_ref, vt_ref, prev_ref, o_ref, *scratch, **kw):
    del prev_ref
    _flash_kernel(q_ref, qn_ref, k_ref, vt_ref, o_ref, *scratch, **kw)


def _flash(q2d, k2d, vt2d, out, *, batch, seq, row0, n_kv_heads, group, dq, dv, k_col0, scale, tq, tk):
    assert row0 % seq == 0 and k_col0 % dq == 0
    n = q2d.shape[0]
    tq = _tile(seq, tq)
    tk = _tile(seq // 2, tk)
    qb0, kb0 = row0 // tq, row0 // seq
    nq = seq // tq
    kc0 = k_col0 // dq
    rows = group * tq
    dve = dv + ONES_ROWS
    kw = dict(group=group, dq=dq, dv=dv, tk=tk, c=scale * LOG2E)
    in_specs = [pl.BlockSpec((tq, group * dq), lambda b, h, i: (qb0 + b * nq + i, h)),
                pl.BlockSpec((tq, group * dq), lambda b, h, i: (qb0 + b * nq + jnp.minimum(i + 1, nq - 1), h)),
                pl.BlockSpec((seq, dq), lambda b, h, i: (kb0 + b, kc0 + h)),
                pl.BlockSpec((dve, seq), lambda b, h, i: (h, kb0 + b))]
    if out is None:
        kern, operands, aliases = functools.partial(_flash_kernel, **kw), (q2d, q2d, k2d, vt2d), {}
    else:
        kern, operands, aliases = functools.partial(_flash_into_kernel, **kw), (q2d, q2d, k2d, vt2d, out), {4: 0}
        in_specs.append(pl.BlockSpec(memory_space=pl.ANY))
    return pl.pallas_call(
        kern,
        grid=(batch, n_kv_heads, nq),
        in_specs=in_specs,
        out_specs=pl.BlockSpec((tq, group * dv), lambda b, h, i: (qb0 + b * nq + i, h)),
        out_shape=jax.ShapeDtypeStruct((n, n_kv_heads * group * dv), BF16),
        scratch_shapes=[pltpu.VMEM((rows, dq), BF16), pltpu.VMEM((rows, dq), BF16), pltpu.VMEM((1, rows), F32),
                        pltpu.VMEM((dve, rows), F32), pltpu.VMEM((2, tk, rows), F32)],
        input_output_aliases=aliases,
        compiler_params=_params("arbitrary", "arbitrary", "arbitrary"),
        name="flash",
    )(*operands)


def _gate_mix_kernel(h_ref, oa_ref, ob_ref, wga_ref, wgb_ref, wa_ref, wb_ref, o_ref):
    for rs in _sub_blocks(h_ref.shape[0], 2 * SUB_ROWS):
        h = h_ref[rs, :]
        ga = jax.nn.sigmoid(jnp.dot(h, wga_ref[...], preferred_element_type=F32))
        gb = jax.nn.sigmoid(jnp.dot(h, wgb_ref[...], preferred_element_type=F32))
        ya = jnp.dot(oa_ref[rs, :], wa_ref[...], preferred_element_type=F32)
        yb = jnp.dot(ob_ref[rs, :], wb_ref[...], preferred_element_type=F32)
        o_ref[rs, :] = (ga * ya + gb * yb).astype(o_ref.dtype)


def _gate_mix(h, oa, ob, w_gates, wa, wb, tm, tn):
    n, d = oa.shape
    dm = wa.shape[1]
    nct = dm // tn
    rows = pl.BlockSpec((tm, d), lambda i, j: (i, 0))
    cols = pl.BlockSpec((d, tn), lambda i, j: (0, j))
    return pl.pallas_call(
        _gate_mix_kernel,
        grid=(n // tm, nct),
        in_specs=[rows, rows, rows, cols, pl.BlockSpec((d, tn), lambda i, j: (0, nct + j)), cols, cols],
        out_specs=pl.BlockSpec((tm, tn), lambda i, j: (i, j)),
        out_shape=jax.ShapeDtypeStruct((n, dm), BF16),
        compiler_params=_params("arbitrary", "arbitrary"),
        name="gate_mix",
    )(h, oa, ob, w_gates, w_gates, wa, wb)


def _layer_norm(y, g, b):
    mu = jnp.mean(y, axis=-1, keepdims=True)
    yc = y - mu
    var = jnp.mean(yc * yc, axis=-1, keepdims=True)
    return yc * lax.rsqrt(var + LN_EPS) * g + b


def _route(logits):
    lane = lax.broadcasted_iota(jnp.int32, logits.shape, 1).astype(F32)
    big = float(LANES)
    ninf = -jnp.inf
    gmask = lane < N_GROUPS
    gl = jnp.where(gmask, logits, ninf)
    gmax = jnp.max(gl, axis=-1, keepdims=True)
    grp = jnp.min(jnp.where(gl == gmax, lane, big), axis=-1, keepdims=True)
    den = jnp.sum(jnp.exp(gl - gmax), axis=-1, keepdims=True)
    p_grp = 1.0 / den
    lo = N_GROUPS + grp * EXPERTS_PER_GROUP
    el = jnp.where(lane >= lo, jnp.where(lane < lo + EXPERTS_PER_GROUP, logits, ninf), ninf)
    t1 = jnp.max(el, axis=-1, keepdims=True)
    i1 = jnp.min(jnp.where(el == t1, lane, big), axis=-1, keepdims=True)
    el2 = jnp.where(lane == i1, ninf, el)
    t2 = jnp.max(el2, axis=-1, keepdims=True)
    i2 = jnp.min(jnp.where(el2 == t2, lane, big), axis=-1, keepdims=True)
    e = jnp.exp(t2 - t1)
    w1 = p_grp / (1.0 + e)
    w2 = p_grp * e / (1.0 + e)
    out = jnp.where(lane == 0, w1, jnp.where(lane == 1, w2, jnp.where(lane == 2, i1 - N_GROUPS, i2 - N_GROUPS)))
    return jnp.where(lane < 4, out, 0.0)


def _ln1_kernel(seg_ref, g_ref, *refs, bounds):
    del seg_ref
    *x_refs, mod_ref, w_ref, lg_ref, lb_ref, wr_ref, br_ref, x1_ref, h2_ref, rt_ref = refs
    i = pl.program_id(0)
    g1 = mod_ref[0, 2:3, :]
    sh2 = mod_ref[0, 3:4, :]
    sc2 = mod_ref[0, 4:5, :]
    for rs in _sub_blocks(g_ref.shape[0], SUB_ROWS):
        attn = jnp.dot(g_ref[rs, :], w_ref[...], preferred_element_type=F32)
        x = _live_rows(i, x_refs, bounds, rs)
        x1 = _layer_norm(DEEPNORM_ALPHA * x + g1 * attn, lg_ref[...], lb_ref[...])
        x1_ref[rs, :] = x1
        h2 = x1 * (1.0 + sc2) + sh2
        h2_ref[rs, :] = h2
        h_hi = h2.astype(BF16)
        h_lo = (h2 - h_hi.astype(F32)).astype(BF16)
        a = jnp.dot(h_hi, wr_ref[...], preferred_element_type=F32)
        b = jnp.dot(h_lo, wr_ref[:, :LANES], preferred_element_type=F32)
        rt_ref[rs, :] = _route(a[:, :LANES] + a[:, LANES:] + b + br_ref[...])


def _ln1(gated, xs, mod, tile_seg, w_out, lg, lb, wr, br, tm):
    n, d = gated.shape
    row = lambda i, seg: (i, 0)
    const = lambda i, seg: (0, 0)
    x_specs, bounds = _group_specs(xs, tm)
    return pl.pallas_call(
        functools.partial(_ln1_kernel, bounds=bounds),
        grid_spec=pltpu.PrefetchScalarGridSpec(
            num_scalar_prefetch=1,
            grid=(n // tm,),
            in_specs=[pl.BlockSpec((tm, d), row)] + x_specs + [
                      pl.BlockSpec((1, 8, d), lambda i, seg: (seg[i], 0, 0)),
                      pl.BlockSpec((d, d), const),
                      pl.BlockSpec((1, d), const),
                      pl.BlockSpec((1, d), const),
                      pl.BlockSpec((d, 2 * LANES), const),
                      pl.BlockSpec((1, LANES), const)],
            out_specs=[pl.BlockSpec((tm, d), row), pl.BlockSpec((tm, d), row), pl.BlockSpec((tm, LANES), row)]),
        out_shape=[jax.ShapeDtypeStruct((n, d), F32), jax.ShapeDtypeStruct((n, d), F32),
                   jax.ShapeDtypeStruct((n, LANES), F32)],
        compiler_params=_params("arbitrary"),
        name="ln1_route",
    )(tile_seg, gated, *xs, mod, w_out, lg, lb, wr, br)


def _row_gather_start(src_hbm, dst, sem, idx, row):
    pltpu.make_async_copy(src_hbm.at[pl.ds(idx, 1), :], dst.at[pl.ds(row, 1), :], sem).start(priority=row % 2)


def _row_gather_wait(src_hbm, dst, sem, row):
    pltpu.make_async_copy(src_hbm.at[pl.ds(0, 1), :], dst.at[pl.ds(row, 1), :], sem).wait()


def _gather_rows_start(src_hbm, dst, sem, idx_ref, idx0, rows):
    for r in range(rows):
        _row_gather_start(src_hbm, dst, sem, idx_ref[0, 0, idx0 + r], r)


def _gather_rows_wait(src_hbm, dst, sem, rows):
    def body(r, carry):
        _row_gather_wait(src_hbm, dst, sem, r)
        return carry

    lax.fori_loop(0, rows, body, 0, unroll=GATHER_UNROLL)


def _expert_kernel(blk_e_ref, n_used_ref, tok_ref, tok_next_ref, h2_hbm, wg_ref, wu_ref, wd_ref, o_ref,
                   xbuf, wg_bf, wu_bf, wd_bf, sem):
    i = pl.program_id(0)
    n_used = n_used_ref[0]
    e = blk_e_ref[i]
    prev = blk_e_ref[jnp.maximum(i - 1, 0)]
    rows = xbuf.shape[1]
    slot = i % 2

    @pl.when(i == 0)
    def _():
        _gather_rows_start(h2_hbm, xbuf.at[0], sem.at[0], tok_ref, 0, rows)

    @pl.when(i + 1 < n_used)
    def _():
        _gather_rows_start(h2_hbm, xbuf.at[1 - slot], sem.at[1 - slot], tok_next_ref, 0, rows)

    @pl.when(jnp.logical_or(i == 0, e != prev))
    def _():
        wg_bf[...] = wg_ref[0].astype(BF16)
        wu_bf[...] = wu_ref[0].astype(BF16)
        wd_bf[...] = wd_ref[0].astype(BF16)

    @pl.when(i < n_used)
    def _():
        _gather_rows_wait(h2_hbm, xbuf.at[slot], sem.at[slot], rows)
        for rs in _sub_blocks(rows, SUB_ROWS):
            x = xbuf[slot, rs, :].astype(BF16)
            gate = jnp.dot(x, wg_bf[...], preferred_element_type=F32)
            up = jnp.dot(x, wu_bf[...], preferred_element_type=F32)
            act = (gate * jax.nn.sigmoid(gate) * up).astype(BF16)
            o_ref[rs, :] = jnp.dot(act, wd_bf[...], preferred_element_type=F32)

    @pl.when(i >= n_used)
    def _():
        o_ref[...] = jnp.zeros_like(o_ref)


def _experts(h2, tok_blocks, blk_e, n_used, w_gate, w_up, w_down):
    n_blk = tok_blocks.shape[0]
    rows = tok_blocks.shape[2]
    d = h2.shape[1]
    de = w_gate.shape[2]
    return pl.pallas_call(
        _expert_kernel,
        grid_spec=pltpu.PrefetchScalarGridSpec(
            num_scalar_prefetch=2,
            grid=(n_blk,),
            in_specs=[pl.BlockSpec((1, 1, rows), lambda i, be, nu: (i, 0, 0), memory_space=pltpu.SMEM),
                      pl.BlockSpec((1, 1, rows), lambda i, be, nu: (jnp.minimum(i + 1, n_blk - 1), 0, 0),
                                   memory_space=pltpu.SMEM),
                      pl.BlockSpec(memory_space=pl.ANY),
                      pl.BlockSpec((1, d, de), lambda i, be, nu: (be[i], 0, 0)),
                      pl.BlockSpec((1, d, de), lambda i, be, nu: (be[i], 0, 0)),
                      pl.BlockSpec((1, de, d), lambda i, be, nu: (be[i], 0, 0))],
            out_specs=pl.BlockSpec((rows, d), lambda i, be, nu: (i, 0)),
            scratch_shapes=[pltpu.VMEM((2, rows, d), F32),
                            pltpu.VMEM((d, de), BF16),
                            pltpu.VMEM((d, de), BF16),
                            pltpu.VMEM((de, d), BF16),
                            pltpu.SemaphoreType.DMA((2,))]),
        out_shape=jax.ShapeDtypeStruct((n_blk * rows, d), F32),
        compiler_params=_params("arbitrary"),
        name="experts",
    )(blk_e, n_used, tok_blocks, tok_blocks, h2, w_gate, w_up, w_down)


def _final_kernel(seg_ref, pos_ref, pos_next_ref, x1_ref, mod_ref, rt_ref, lg_ref, lb_ref, ys_hbm, o_ref, buf, sem):
    del seg_ref
    i = pl.program_id(0)
    tm = x1_ref.shape[0]
    slot = i % 2

    def start_tile(idx_ref, s):
        for k in range(TOP_K):
            _gather_rows_start(ys_hbm, buf.at[s, k], sem.at[s], idx_ref, k * tm, tm)

    @pl.when(i == 0)
    def _():
        start_tile(pos_ref, 0)

    @pl.when(i + 1 < pl.num_programs(0))
    def _():
        start_tile(pos_next_ref, 1 - slot)

    for k in range(TOP_K):
        _gather_rows_wait(ys_hbm, buf.at[slot, k], sem.at[slot], tm)
    g2 = mod_ref[0, 5:6, :]
    for rs in _sub_blocks(tm, SUB_ROWS):
        rt = rt_ref[rs, :]
        ffn = rt[:, 0:1] * buf[slot, 0, rs, :] + rt[:, 1:2] * buf[slot, 1, rs, :]
        o_ref[rs, :] = _layer_norm(DEEPNORM_ALPHA * x1_ref[rs, :] + g2 * ffn, lg_ref[...], lb_ref[...])


def _final(x1, mod, tile_seg, pos_blocks, rt, lg, lb, ys, tile0, n_tiles, tm):
    d = x1.shape[1]
    row = lambda i, seg: (tile0 + i, 0)
    const = lambda i, seg: (0, 0)
    return pl.pallas_call(
        _final_kernel,
        grid_spec=pltpu.PrefetchScalarGridSpec(
            num_scalar_prefetch=1,
            grid=(n_tiles,),
            in_specs=[pl.BlockSpec((1, 1, TOP_K * tm), lambda i, seg: (tile0 + i, 0, 0), memory_space=pltpu.SMEM),
                      pl.BlockSpec((1, 1, TOP_K * tm),
                                   lambda i, seg: (tile0 + jnp.minimum(i + 1, n_tiles - 1), 0, 0),
                                   memory_space=pltpu.SMEM),
                      pl.BlockSpec((tm, d), row),
                      pl.BlockSpec((1, 8, d), lambda i, seg: (seg[tile0 + i], 0, 0)),
                      pl.BlockSpec((tm, LANES), row),
                      pl.BlockSpec((1, d), const),
                      pl.BlockSpec((1, d), const),
                      pl.BlockSpec(memory_space=pl.ANY)],
            out_specs=pl.BlockSpec((tm, d), lambda i, seg: (i, 0)),
            scratch_shapes=[pltpu.VMEM((2, TOP_K, tm, d), F32), pltpu.SemaphoreType.DMA((2,))]),
        out_shape=jax.ShapeDtypeStruct((n_tiles * tm, d), F32),
        compiler_params=_params("arbitrary"),
        name="combine_ln2",
    )(tile_seg, pos_blocks, pos_blocks, x1, mod, rt, lg, lb, ys)


def _rope_tables(pos_a, pos_b):
    inv = ROPE_THETA ** (-jnp.arange(0, 64, 2, dtype=F32) / 64)

    def half(pos):
        if pos is None:
            z = jnp.zeros((pos_a.shape[0], 64), F32)
            return z, z, z
        ang = pos.astype(F32)[:, None] * inv[None, :]
        cos, sin = jnp.cos(ang), jnp.sin(ang)
        zero = jnp.zeros_like(sin)
        return (jnp.concatenate([cos, cos], 1), jnp.concatenate([-sin, zero], 1), jnp.concatenate([zero, sin], 1))

    a, b = half(pos_a), half(pos_b)
    return tuple(jnp.concatenate([u, v], 1) for u, v in zip(a, b))


def _dispatch(eid, rows):
    n = eid.shape[0]
    a = n * TOP_K
    i32 = jnp.int32
    iota = jnp.arange(a, dtype=i32)
    experts = jnp.arange(N_EXPERTS, dtype=i32)
    e_sorted, order = lax.sort((eid.reshape(-1), iota), num_keys=1, is_stable=True)
    starts = jnp.sum((e_sorted[None, :] < experts[:, None]).astype(i32), axis=1)
    counts = jnp.diff(starts, append=a)
    padded = ((counts + rows - 1) // rows) * rows
    pad_end = jnp.cumsum(padded)
    pad_start = pad_end - padded
    gap_step = jnp.diff(pad_start - starts, prepend=0)
    dest = iota + jnp.sum(jnp.where(e_sorted[:, None] >= experts[None, :], gap_step[None, :], 0), axis=1)
    _, pos = lax.sort((order, dest), num_keys=1)
    p_rows = ((a + rows - 1) // rows) * rows + N_EXPERTS * rows
    n_blk = p_rows // rows
    blk = jnp.arange(n_blk, dtype=i32)
    blk_e = jnp.minimum(jnp.sum((blk[:, None] * rows >= pad_end[None, :]).astype(i32), axis=1), N_EXPERTS - 1)
    onehot = blk_e[:, None] == experts[None, :]
    src0 = jnp.sum(jnp.where(onehot, (starts - pad_start)[None, :], 0), axis=1) + blk * rows
    src0 = jnp.minimum(src0, a)
    order_ext = jnp.concatenate([order, jnp.zeros((rows,), i32)])
    tok = jnp.take(order_ext, src0[:, None] + jnp.arange(rows, dtype=i32)[None, :], axis=0) // TOP_K
    n_used = (pad_end[-1] // rows).astype(i32).reshape(1)
    return tok.reshape(n_blk, 1, rows), pos.reshape(n, TOP_K), blk_e.astype(i32), n_used


def _layer(groups, w_ada, b_ada, w_in, a_q_norm, a_kv_norm, a_w_uq, a_w_ukv, a_w_o, b_q_norm, b_k_norm, b_w_o,
           w_out, ln1_g, ln1_b, w_group, b_group, w_expert, b_expert, e_w_gate, e_w_up, e_w_down, ln2_g, ln2_b):
    d = w_in.shape[0]
    seqs = [x.shape[1] for x, _ in groups]
    seg = functools.reduce(math.gcd, seqs)
    tm = _tile(seg, 512)
    s_max = max(seqs)

    xs = [x.reshape(-1, d) for x, _ in groups]
    n = sum(x.shape[0] for x in xs)
    c_all = jnp.concatenate([c for _, c in groups], axis=0)
    nb = c_all.shape[0]
    c_pad = jnp.zeros((-(-nb // 8) * 8, d), F32).at[:nb].set(c_all)
    seg_batch, tile_seg, tile_pos = [], [], []
    b0 = 0
    for x, _ in groups:
        bsz, s = x.shape[0], x.shape[1]
        for b in range(bsz):
            for t in range(s // tm):
                tile_seg.append(b0 + b)
                tile_pos.append(t)
        b0 += bsz
    tile_seg = jnp.asarray(np.asarray(tile_seg, np.int32))
    tile_pos = jnp.asarray(np.asarray(tile_pos, np.int32))

    mod = _ada(c_pad, w_ada, b_ada)[:nb].reshape(nb, 6, d)
    mod = jnp.concatenate([mod, jnp.zeros((nb, 2, d), F32)], axis=1)

    o_q, o_kv, o_kr, o_qb, o_kb, o_vb, o_ga, o_gb = np.cumsum(
        [0, A_Q_RANK, A_KV_RANK, A_ROPE, B_HEADS * B_HEAD_DIM, B_KV_HEADS * B_HEAD_DIM,
         B_KV_HEADS * B_HEAD_DIM, d]).tolist()
    w_lat = jnp.pad(w_in[:, :o_qb], ((0, 0), (0, o_kr + LANES - o_qb))).astype(BF16)
    w_qk = w_in[:, o_qb:o_vb].astype(BF16)
    w_v = w_in[:, o_vb:o_ga].astype(BF16)
    w_g = w_in[:, o_ga:].astype(BF16)
    w_uq = jnp.pad(a_w_uq.reshape(A_Q_RANK, A_HEADS, A_NOPE + A_ROPE),
                   ((0, 0), (0, 0), (0, A_QK_PAD - A_NOPE - A_ROPE))).reshape(A_Q_RANK, A_HEADS * A_QK_PAD).astype(BF16)
    w_ukv = a_w_ukv.reshape(A_KV_RANK, A_HEADS, A_NOPE + A_V)
    w_uk = w_ukv[:, :, :A_NOPE].reshape(A_KV_RANK, A_HEADS * A_NOPE).astype(BF16)
    w_uv = w_ukv[:, :, A_NOPE:].reshape(A_KV_RANK, A_HEADS * A_V).astype(BF16)
    w_r = jnp.zeros((d, LANES), F32).at[:, :N_GROUPS].set(w_group).at[:, N_GROUPS:N_GROUPS + N_EXPERTS].set(w_expert)
    b_r = jnp.zeros((1, LANES), F32).at[0, :N_GROUPS].set(b_group).at[0, N_GROUPS:N_GROUPS + N_EXPERTS].set(b_expert)
    w_r_hi = w_r.astype(BF16)
    w_r = jnp.concatenate([w_r_hi, (w_r - w_r_hi.astype(F32)).astype(BF16)], axis=1)

    t = jnp.arange(s_max, dtype=jnp.int32)
    tabs_a = _rope_tables(t, None)
    tabs_b = _rope_tables(t // GRID_W, t % GRID_W)

    h, lat = _mod_proj(xs, mod, tile_seg, w_lat, tm)
    qk_b = _proj_qk(h, w_qk, jnp.stack([b_q_norm, b_k_norm]).reshape(2, 1, B_HEAD_DIM), tabs_b, tile_pos, tm,
                    B_HEADS)
    v_b = _proj_vt(h, w_v, tm, 512)
    q_a = _mla_q(lat, a_q_norm.reshape(1, -1), w_uq, tabs_a, tile_pos, tm)
    k_a, v_a = _mla_kv(lat, a_kv_norm.reshape(1, -1), w_uk, w_uv, tabs_a, tile_pos, tm)

    o_a = o_b = None
    row0 = 0
    for x, _ in groups:
        bsz, s = x.shape[0], x.shape[1]
        o_a = _flash(q_a, k_a, v_a, o_a, batch=bsz, seq=s, row0=row0, n_kv_heads=A_HEADS, group=1,
                     dq=A_QK_PAD, dv=A_V, k_col0=0, scale=(A_NOPE + A_ROPE) ** -0.5, tq=1024, tk=1024)
        o_b = _flash(qk_b, qk_b, v_b, o_b, batch=bsz, seq=s, row0=row0, n_kv_heads=B_KV_HEADS, group=B_GROUP,
                     dq=B_HEAD_DIM, dv=B_HEAD_DIM, k_col0=B_HEADS * B_HEAD_DIM, scale=B_HEAD_DIM ** -0.5,
                     tq=256, tk=1024)
        row0 += bsz * s

    gated = _gate_mix(h, o_a, o_b, w_g, a_w_o.astype(BF16), b_w_o.astype(BF16), tm, 512)
    tm2 = _tile(seg, 256)
    seg2 = jnp.repeat(tile_seg, tm // tm2)
    x1, h2, rt = _ln1(gated, xs, mod, seg2, w_out.astype(BF16), ln1_g.reshape(1, d), ln1_b.reshape(1, d),
                      w_r, b_r, tm2)

    eid = rt[:, 2:4].astype(jnp.int32)
    tok_blocks, pos, blk_e, n_used = _dispatch(eid, EXPERT_ROWS)
    ys = _experts(h2, tok_blocks, blk_e, n_used, e_w_gate, e_w_up, e_w_down)
    pos_blocks = pos.reshape(n // tm2, tm2, TOP_K).transpose(0, 2, 1).reshape(n // tm2, 1, TOP_K * tm2)

    outs = []
    tile0 = 0
    for x, _ in groups:
        bsz, s = x.shape[0], x.shape[1]
        nt = bsz * s // tm2
        y = _final(x1, mod, seg2, pos_blocks, rt, ln2_g.reshape(1, d), ln2_b.reshape(1, d), ys, tile0, nt, tm2)
        outs.append(y.reshape(bsz, s, d))
        tile0 += nt
    return tuple(outs)


def kernel(x_prompt, x_sample, c_prompt, c_sample, w_ada, b_ada, w_in, a_q_norm, a_kv_norm, a_w_uq, a_w_ukv, a_w_o,
           b_q_norm, b_k_norm, b_w_o, w_out, ln1_g, ln1_b, w_group, b_group, w_expert, b_expert, e_w_gate, e_w_up,
           e_w_down, ln2_g, ln2_b):
    weights = (w_ada, b_ada, w_in, a_q_norm, a_kv_norm, a_w_uq, a_w_ukv, a_w_o, b_q_norm, b_k_norm, b_w_o, w_out,
               ln1_g, ln1_b, w_group, b_group, w_expert, b_expert, e_w_gate, e_w_up, e_w_down, ln2_g, ln2_b)
    return _layer(((x_prompt, c_prompt), (x_sample, c_sample)), *[w[0] for w in weights])
```

```python
import functools
import math

import numpy as np
import jax
import jax.numpy as jnp
from jax import lax
from jax.experimental import pallas as pl
from jax.experimental.pallas import tpu as pltpu

F32 = jnp.float32
BF16 = jnp.bfloat16
HIGHEST = lax.Precision.HIGHEST

GRID_W = 64
ROPE_THETA = 10000.0
A_HEADS = 16
A_Q_RANK = 512
A_KV_RANK = 512
A_NOPE = 128
A_ROPE = 64
A_V = 128
A_QK_PAD = 256
B_HEADS = 16
B_KV_HEADS = 4
B_GROUP = B_HEADS // B_KV_HEADS
B_HEAD_DIM = 128
N_GROUPS = 8
EXPERTS_PER_GROUP = 8
N_EXPERTS = N_GROUPS * EXPERTS_PER_GROUP
TOP_K = 2
RMS_EPS = 1e-6
LN_EPS = 1e-5
DEPTH = 1
DEEPNORM_ALPHA = (2 * DEPTH) ** 0.25
LOG2E = 1.4426950408889634

LANES = 128
EXPERT_ROWS = 256
SUB_ROWS = 128
GATHER_UNROLL = 8
ONES_ROWS = 16
VMEM_LIMIT = 48 * 1024 * 1024


def _params(*sem):
    return pltpu.CompilerParams(dimension_semantics=sem, vmem_limit_bytes=VMEM_LIMIT)


def _tile(n, pref):
    t = min(n, pref)
    while n % t:
        t //= 2
    return t


def _sub_blocks(rows, sub):
    sub = _tile(rows, sub)
    return [slice(r, r + sub) for r in range(0, rows, sub)]


def _ada_kernel(c_ref, w_ref, b_ref, o_ref):
    c = c_ref[...]
    a = c * jax.nn.sigmoid(c)
    o_ref[...] = jnp.dot(a, w_ref[...], precision=HIGHEST, preferred_element_type=F32) + b_ref[...]


def _ada(c_pad, w_ada, b_ada):
    d, n = w_ada.shape
    tn = _tile(n, 1024)
    return pl.pallas_call(
        _ada_kernel,
        grid=(n // tn,),
        in_specs=[pl.BlockSpec(c_pad.shape, lambda j: (0, 0)),
                  pl.BlockSpec((d, tn), lambda j: (0, j)),
                  pl.BlockSpec((1, tn), lambda j: (0, j))],
        out_specs=pl.BlockSpec((c_pad.shape[0], tn), lambda j: (0, j)),
        out_shape=jax.ShapeDtypeStruct((c_pad.shape[0], n), F32),
        compiler_params=_params("arbitrary"),
        name="ada",
    )(c_pad, w_ada, b_ada.reshape(1, n))


def _group_specs(xs, tm):
    specs, bounds = [], []
    t0 = 0
    for x in xs:
        nt = x.shape[0] // tm
        specs.append(pl.BlockSpec((tm, x.shape[1]), lambda i, *_, t0=t0, nt=nt: (jnp.clip(i - t0, 0, nt - 1), 0)))
        t0 += nt
        bounds.append(t0)
    return specs, tuple(bounds[:-1])


def _live_rows(i, refs, bounds, rs):
    val = refs[-1][rs, :]
    for g in reversed(range(len(refs) - 1)):
        val = jnp.where(i < bounds[g], refs[g][rs, :], val)
    return val


def _mod_proj_kernel(seg_ref, *refs, bounds):
    del seg_ref
    *x_refs, mod_ref, w_ref, h_ref, o_ref = refs
    sh = mod_ref[0, 0:1, :]
    sc = mod_ref[0, 1:2, :]
    x = _live_rows(pl.program_id(0), x_refs, bounds, slice(None))
    h = (x * (1.0 + sc) + sh).astype(BF16)
    h_ref[...] = h
    o_ref[...] = jnp.dot(h, w_ref[...], preferred_element_type=F32)


def _mod_proj(xs, mod, tile_seg, w, tm):
    n = sum(x.shape[0] for x in xs)
    d = xs[0].shape[1]
    nc = w.shape[1]
    x_specs, bounds = _group_specs(xs, tm)
    return pl.pallas_call(
        functools.partial(_mod_proj_kernel, bounds=bounds),
        grid_spec=pltpu.PrefetchScalarGridSpec(
            num_scalar_prefetch=1,
            grid=(n // tm,),
            in_specs=x_specs + [pl.BlockSpec((1, 8, d), lambda i, seg: (seg[i], 0, 0)),
                                pl.BlockSpec((d, nc), lambda i, seg: (0, 0))],
            out_specs=[pl.BlockSpec((tm, d), lambda i, seg: (i, 0)),
                       pl.BlockSpec((tm, nc), lambda i, seg: (i, 0))]),
        out_shape=[jax.ShapeDtypeStruct((n, d), BF16), jax.ShapeDtypeStruct((n, nc), F32)],
        compiler_params=_params("arbitrary"),
        name="mod_proj",
    )(tile_seg, *xs, mod, w)


def _rope(x, cos, sin):
    return x * cos + pltpu.roll(x, 64, 1) * sin


def _store_vt(vt_ref, cols, v, dv):
    vt = v.T.astype(vt_ref.dtype)
    dve = dv + ONES_ROWS
    for hh in range(v.shape[1] // dv):
        vt_ref[hh * dve:hh * dve + dv, cols] = vt[hh * dv:(hh + 1) * dv, :]
        vt_ref[hh * dve + dv:(hh + 1) * dve, cols] = jnp.ones((ONES_ROWS, vt.shape[1]), vt_ref.dtype)


def _proj_vt_kernel(h_ref, w_ref, o_ref):
    for rs in _sub_blocks(h_ref.shape[0], SUB_ROWS):
        _store_vt(o_ref, rs, jnp.dot(h_ref[rs, :], w_ref[...], preferred_element_type=F32), B_HEAD_DIM)


def _proj_qk_kernel(pos_ref, h_ref, w_ref, g_ref, cos_ref, sin_ref, o_ref, *, n_q_heads):
    del pos_ref
    for rs in _sub_blocks(h_ref.shape[0], SUB_ROWS):
        acc = jnp.dot(h_ref[rs, :], w_ref[...], preferred_element_type=F32)
        cos, sin = cos_ref[rs, :], sin_ref[rs, :]
        for hh in range(acc.shape[1] // B_HEAD_DIM):
            x = acc[:, hh * B_HEAD_DIM:(hh + 1) * B_HEAD_DIM]
            r = lax.rsqrt(jnp.mean(x * x, axis=-1, keepdims=True) + RMS_EPS)
            y = x * r * g_ref[0 if hh < n_q_heads else 1]
            o_ref[rs, hh * B_HEAD_DIM:(hh + 1) * B_HEAD_DIM] = _rope(y, cos, sin).astype(o_ref.dtype)


def _proj_vt(h, w, tm, tn):
    n, d = h.shape
    nc = w.shape[1]
    tne = tn // B_HEAD_DIM * (B_HEAD_DIM + ONES_ROWS)
    return pl.pallas_call(
        _proj_vt_kernel,
        grid=(n // tm, nc // tn),
        in_specs=[pl.BlockSpec((tm, d), lambda i, j: (i, 0)),
                  pl.BlockSpec((d, tn), lambda i, j: (0, j))],
        out_specs=pl.BlockSpec((tne, tm), lambda i, j: (j, i)),
        out_shape=jax.ShapeDtypeStruct((nc // tn * tne, n), BF16),
        compiler_params=_params("arbitrary", "arbitrary"),
        name="proj_vt",
    )(h, w)


def _proj_qk(h, w, gains, tabs, tile_pos, tm, n_q_heads):
    n, d = h.shape
    nc = w.shape[1]
    tab_spec = pl.BlockSpec((tm, LANES), lambda i, pos: (pos[i], 0))
    return pl.pallas_call(
        functools.partial(_proj_qk_kernel, n_q_heads=n_q_heads),
        grid_spec=pltpu.PrefetchScalarGridSpec(
            num_scalar_prefetch=1,
            grid=(n // tm,),
            in_specs=[pl.BlockSpec((tm, d), lambda i, pos: (i, 0)),
                      pl.BlockSpec((d, nc), lambda i, pos: (0, 0)),
                      pl.BlockSpec((2, 1, LANES), lambda i, pos: (0, 0, 0)),
                      tab_spec, tab_spec],
            out_specs=pl.BlockSpec((tm, nc), lambda i, pos: (i, 0))),
        out_shape=jax.ShapeDtypeStruct((n, nc), BF16),
        compiler_params=_params("arbitrary"),
        name="proj_qk",
    )(tile_pos, h, w, gains, *tabs)


def _rms(x, g):
    return x * lax.rsqrt(jnp.mean(x * x, axis=-1, keepdims=True) + RMS_EPS) * g


def _mla_q_kernel(pos_ref, c_ref, g_ref, w_ref, cos_ref, sin_ref, o_ref):
    del pos_ref
    for rs in _sub_blocks(c_ref.shape[0], SUB_ROWS):
        xn = _rms(c_ref[rs, :], g_ref[...]).astype(BF16)
        acc = jnp.dot(xn, w_ref[...], preferred_element_type=F32)
        cos, sin = cos_ref[rs, :], sin_ref[rs, :]
        for hh in range(acc.shape[1] // A_QK_PAD):
            base = hh * A_QK_PAD
            o_ref[rs, base:base + A_NOPE] = acc[:, base:base + A_NOPE].astype(o_ref.dtype)
            o_ref[rs, base + A_NOPE:base + A_QK_PAD] = _rope(
                acc[:, base + A_NOPE:base + A_QK_PAD], cos, sin).astype(o_ref.dtype)


def _mla_q(cqkv, gain, w, tabs, tile_pos, tm):
    n = cqkv.shape[0]
    nc = w.shape[1]
    tn = nc
    tab_spec = pl.BlockSpec((tm, LANES), lambda i, j, pos: (pos[i], 0))
    return pl.pallas_call(
        _mla_q_kernel,
        grid_spec=pltpu.PrefetchScalarGridSpec(
            num_scalar_prefetch=1,
            grid=(n // tm, nc // tn),
            in_specs=[pl.BlockSpec((tm, A_Q_RANK), lambda i, j, pos: (i, 0)),
                      pl.BlockSpec((1, A_Q_RANK), lambda i, j, pos: (0, 0)),
                      pl.BlockSpec((A_Q_RANK, tn), lambda i, j, pos: (0, j)),
                      tab_spec, tab_spec],
            out_specs=pl.BlockSpec((tm, tn), lambda i, j, pos: (i, j))),
        out_shape=jax.ShapeDtypeStruct((n, nc), BF16),
        compiler_params=_params("arbitrary", "arbitrary"),
        name="mla_q",
    )(tile_pos, cqkv, gain, w, *tabs)


def _mla_kv_kernel(pos_ref, c_ref, kr_ref, g_ref, wk_ref, wv_ref, cos_ref, sin_ref, k_ref, v_ref):
    del pos_ref
    for rs in _sub_blocks(c_ref.shape[0], SUB_ROWS):
        xn = _rms(c_ref[rs, :], g_ref[...]).astype(BF16)
        kn = jnp.dot(xn, wk_ref[...], preferred_element_type=F32)
        _store_vt(v_ref, rs, jnp.dot(xn, wv_ref[...], preferred_element_type=F32), A_V)
        kr = _rope(kr_ref[rs, :], cos_ref[rs, :], sin_ref[rs, :]).astype(k_ref.dtype)
        for hh in range(kn.shape[1] // A_NOPE):
            k_ref[rs, hh * A_QK_PAD:hh * A_QK_PAD + A_NOPE] = kn[:, hh * A_NOPE:(hh + 1) * A_NOPE].astype(k_ref.dtype)
            k_ref[rs, hh * A_QK_PAD + A_NOPE:(hh + 1) * A_QK_PAD] = kr


def _mla_kv(cqkv, gain, wk, wv, tabs, tile_pos, tm):
    n = cqkv.shape[0]
    hpt = A_HEADS
    n_ct = A_HEADS // hpt
    kr_blk = (A_Q_RANK + A_KV_RANK) // LANES
    tab_spec = pl.BlockSpec((tm, LANES), lambda i, j, pos: (pos[i], 0))
    return pl.pallas_call(
        _mla_kv_kernel,
        grid_spec=pltpu.PrefetchScalarGridSpec(
            num_scalar_prefetch=1,
            grid=(n // tm, n_ct),
            in_specs=[pl.BlockSpec((tm, A_KV_RANK), lambda i, j, pos: (i, 1)),
                      pl.BlockSpec((tm, LANES), lambda i, j, pos: (i, kr_blk)),
                      pl.BlockSpec((1, A_KV_RANK), lambda i, j, pos: (0, 0)),
                      pl.BlockSpec((A_KV_RANK, hpt * A_NOPE), lambda i, j, pos: (0, j)),
                      pl.BlockSpec((A_KV_RANK, hpt * A_V), lambda i, j, pos: (0, j)),
                      tab_spec, tab_spec],
            out_specs=[pl.BlockSpec((tm, hpt * A_QK_PAD), lambda i, j, pos: (i, j)),
                       pl.BlockSpec((hpt * (A_V + ONES_ROWS), tm), lambda i, j, pos: (j, i))]),
        out_shape=[jax.ShapeDtypeStruct((n, A_HEADS * A_QK_PAD), BF16),
                   jax.ShapeDtypeStruct((A_HEADS * (A_V + ONES_ROWS), n), BF16)],
        compiler_params=_params("arbitrary", "arbitrary"),
        name="mla_kv",
    )(tile_pos, cqkv, cqkv, gain, wk, wv, *tabs)


def _flash_kernel(q_ref, qn_ref, k_ref, vt_ref, o_ref, q_sc, qn_sc, m_sc, acc_sc, s_sc, *, group, dq, dv, tk, c):
    tq = q_ref.shape[0]
    n_kv = k_ref.shape[0] // tk
    for g in range(group):
        q_sc[g * tq:(g + 1) * tq, :] = (q_ref[:, g * dq:(g + 1) * dq].astype(F32) * c).astype(BF16)
        qn_sc[g * tq:(g + 1) * tq, :] = (qn_ref[:, g * dq:(g + 1) * dq].astype(F32) * c).astype(BF16)
    m_sc[...] = jnp.full(m_sc.shape, -1e30, F32)
    acc_sc[...] = jnp.zeros(acc_sc.shape, F32)

    def scores(j, queries):
        k = k_ref[pl.ds(pl.multiple_of(j * tk, tk), tk), :]
        return lax.dot_general(k, queries[...], (((1,), (1,)), ((), ())), preferred_element_type=F32)

    def step(j, slot, prefetch):
        s = s_sc[slot]
        if prefetch:
            s_sc[1 - slot] = scores(j + 1, q_sc)
        else:
            s_sc[1 - slot] = scores(0, qn_sc)
        vt = vt_ref[:, pl.ds(pl.multiple_of(j * tk, tk), tk)]
        m_prev = m_sc[...]
        m_new = jnp.maximum(m_prev, jnp.max(s, axis=0, keepdims=True))
        alpha = jnp.exp2(m_prev - m_new)
        p = jnp.exp2(s - m_new).astype(BF16)
        acc_sc[...] = alpha * acc_sc[...] + jnp.dot(vt, p, preferred_element_type=F32)
        m_sc[...] = m_new

    @pl.when(pl.program_id(2) == 0)
    def _():
        s_sc[0] = scores(0, q_sc)

    def pair(jj, carry):
        step(2 * jj, 0, True)
        step(2 * jj + 1, 1, True)
        return carry

    lax.fori_loop(0, n_kv // 2 - 1, pair, 0)
    step(n_kv - 2, 0, True)
    step(n_kv - 1, 1, False)
    acc = acc_sc[...]
    out = (acc[:dv] / acc[dv:dv + 1]).T
    for g in range(group):
        o_ref[:, g * dv:(g + 1) * dv] = out[g * tq:(g + 1) * tq, :].astype(o_ref.dtype)


def _flash_into_kernel(q_ref, qn_ref, k_ref, vt_ref, prev_ref, o_ref, *scratch, **kw):
    del prev_ref
    _flash_kernel(q_ref, qn_ref, k_ref, vt_ref, o_ref, *scratch, **kw)


def _flash(q2d, k2d, vt2d, out, *, batch, seq, row0, n_kv_heads, group, dq, dv, k_col0, scale, tq, tk):
    assert row0 % seq == 0 and k_col0 % dq == 0
    n = q2d.shape[0]
    tq = _tile(seq, tq)
    tk = _tile(seq // 2, tk)
    qb0, kb0 = row0 // tq, row0 // seq
    nq = seq // tq
    kc0 = k_col0 // dq
    rows = group * tq
    dve = dv + ONES_ROWS
    kw = dict(group=group, dq=dq, dv=dv, tk=tk, c=scale * LOG2E)
    in_specs = [pl.BlockSpec((tq, group * dq), lambda b, h, i: (qb0 + b * nq + i, h)),
                pl.BlockSpec((tq, group * dq), lambda b, h, i: (qb0 + b * nq + jnp.minimum(i + 1, nq - 1), h)),
                pl.BlockSpec((seq, dq), lambda b, h, i: (kb0 + b, kc0 + h)),
                pl.BlockSpec((dve, seq), lambda b, h, i: (h, kb0 + b))]
    if out is None:
        kern, operands, aliases = functools.partial(_flash_kernel, **kw), (q2d, q2d, k2d, vt2d), {}
    else:
        kern, operands, aliases = functools.partial(_flash_into_kernel, **kw), (q2d, q2d, k2d, vt2d, out), {4: 0}
        in_specs.append(pl.BlockSpec(memory_space=pl.ANY))
    return pl.pallas_call(
        kern,
        grid=(batch, n_kv_heads, nq),
        in_specs=in_specs,
        out_specs=pl.BlockSpec((tq, group * dv), lambda b, h, i: (qb0 + b * nq + i, h)),
        out_shape=jax.ShapeDtypeStruct((n, n_kv_heads * group * dv), BF16),
        scratch_shapes=[pltpu.VMEM((rows, dq), BF16), pltpu.VMEM((rows, dq), BF16), pltpu.VMEM((1, rows), F32),
                        pltpu.VMEM((dve, rows), F32), pltpu.VMEM((2, tk, rows), F32)],
        input_output_aliases=aliases,
        compiler_params=_params("arbitrary", "arbitrary", "arbitrary"),
        name="flash",
    )(*operands)


def _gate_mix_kernel(h_ref, oa_ref, ob_ref, wga_ref, wgb_ref, wa_ref, wb_ref, o_ref):
    for rs in _sub_blocks(h_ref.shape[0], 2 * SUB_ROWS):
        h = h_ref[rs, :]
        ga = jax.nn.sigmoid(jnp.dot(h, wga_ref[...], preferred_element_type=F32))
        gb = jax.nn.sigmoid(jnp.dot(h, wgb_ref[...], preferred_element_type=F32))
        ya = jnp.dot(oa_ref[rs, :], wa_ref[...], preferred_element_type=F32)
        yb = jnp.dot(ob_ref[rs, :], wb_ref[...], preferred_element_type=F32)
        o_ref[rs, :] = (ga * ya + gb * yb).astype(o_ref.dtype)


def _gate_mix(h, oa, ob, w_gates, wa, wb, tm, tn):
    n, d = oa.shape
    dm = wa.shape[1]
    nct = dm // tn
    rows = pl.BlockSpec((tm, d), lambda i, j: (i, 0))
    cols = pl.BlockSpec((d, tn), lambda i, j: (0, j))
    return pl.pallas_call(
        _gate_mix_kernel,
        grid=(n // tm, nct),
        in_specs=[rows, rows, rows, cols, pl.BlockSpec((d, tn), lambda i, j: (0, nct + j)), cols, cols],
        out_specs=pl.BlockSpec((tm, tn), lambda i, j: (i, j)),
        out_shape=jax.ShapeDtypeStruct((n, dm), BF16),
        compiler_params=_params("arbitrary", "arbitrary"),
        name="gate_mix",
    )(h, oa, ob, w_gates, w_gates, wa, wb)


def _layer_norm(y, g, b):
    mu = jnp.mean(y, axis=-1, keepdims=True)
    yc = y - mu
    var = jnp.mean(yc * yc, axis=-1, keepdims=True)
    return yc * lax.rsqrt(var + LN_EPS) * g + b


def _route(logits):
    lane = lax.broadcasted_iota(jnp.int32, logits.shape, 1).astype(F32)
    big = float(LANES)
    ninf = -jnp.inf
    gmask = lane < N_GROUPS
    gl = jnp.where(gmask, logits, ninf)
    gmax = jnp.max(gl, axis=-1, keepdims=True)
    grp = jnp.min(jnp.where(gl == gmax, lane, big), axis=-1, keepdims=True)
    den = jnp.sum(jnp.exp(gl - gmax), axis=-1, keepdims=True)
    p_grp = 1.0 / den
    lo = N_GROUPS + grp * EXPERTS_PER_GROUP
    el = jnp.where(lane >= lo, jnp.where(lane < lo + EXPERTS_PER_GROUP, logits, ninf), ninf)
    t1 = jnp.max(el, axis=-1, keepdims=True)
    i1 = jnp.min(jnp.where(el == t1, lane, big), axis=-1, keepdims=True)
    el2 = jnp.where(lane == i1, ninf, el)
    t2 = jnp.max(el2, axis=-1, keepdims=True)
    i2 = jnp.min(jnp.where(el2 == t2, lane, big), axis=-1, keepdims=True)
    e = jnp.exp(t2 - t1)
    w1 = p_grp / (1.0 + e)
    w2 = p_grp * e / (1.0 + e)
    out = jnp.where(lane == 0, w1, jnp.where(lane == 1, w2, jnp.where(lane == 2, i1 - N_GROUPS, i2 - N_GROUPS)))
    return jnp.where(lane < 4, out, 0.0)


def _ln1_kernel(seg_ref, g_ref, *refs, bounds):
    del seg_ref
    *x_refs, mod_ref, w_ref, lg_ref, lb_ref, wr_ref, br_ref, x1_ref, h2_ref, rt_ref = refs
    i = pl.program_id(0)
    g1 = mod_ref[0, 2:3, :]
    sh2 = mod_ref[0, 3:4, :]
    sc2 = mod_ref[0, 4:5, :]
    for rs in _sub_blocks(g_ref.shape[0], SUB_ROWS):
        attn = jnp.dot(g_ref[rs, :], w_ref[...], preferred_element_type=F32)
        x = _live_rows(i, x_refs, bounds, rs)
        x1 = _layer_norm(DEEPNORM_ALPHA * x + g1 * attn, lg_ref[...], lb_ref[...])
        x1_ref[rs, :] = x1
        h2 = x1 * (1.0 + sc2) + sh2
        h2_ref[rs, :] = h2
        h_hi = h2.astype(BF16)
        h_lo = (h2 - h_hi.astype(F32)).astype(BF16)
        a = jnp.dot(h_hi, wr_ref[...], preferred_element_type=F32)
        b = jnp.dot(h_lo, wr_ref[:, :LANES], preferred_element_type=F32)
        rt_ref[rs, :] = _route(a[:, :LANES] + a[:, LANES:] + b + br_ref[...])


def _ln1(gated, xs, mod, tile_seg, w_out, lg, lb, wr, br, tm):
    n, d = gated.shape
    row = lambda i, seg: (i, 0)
    const = lambda i, seg: (0, 0)
    x_specs, bounds = _group_specs(xs, tm)
    return pl.pallas_call(
        functools.partial(_ln1_kernel, bounds=bounds),
        grid_spec=pltpu.PrefetchScalarGridSpec(
            num_scalar_prefetch=1,
            grid=(n // tm,),
            in_specs=[pl.BlockSpec((tm, d), row)] + x_specs + [
                      pl.BlockSpec((1, 8, d), lambda i, seg: (seg[i], 0, 0)),
                      pl.BlockSpec((d, d), const),
                      pl.BlockSpec((1, d), const),
                      pl.BlockSpec((1, d), const),
                      pl.BlockSpec((d, 2 * LANES), const),
                      pl.BlockSpec((1, LANES), const)],
            out_specs=[pl.BlockSpec((tm, d), row), pl.BlockSpec((tm, d), row), pl.BlockSpec((tm, LANES), row)]),
        out_shape=[jax.ShapeDtypeStruct((n, d), F32), jax.ShapeDtypeStruct((n, d), F32),
                   jax.ShapeDtypeStruct((n, LANES), F32)],
        compiler_params=_params("arbitrary"),
        name="ln1_route",
    )(tile_seg, gated, *xs, mod, w_out, lg, lb, wr, br)


def _row_gather_start(src_hbm, dst, sem, idx, row):
    pltpu.make_async_copy(src_hbm.at[pl.ds(idx, 1), :], dst.at[pl.ds(row, 1), :], sem).start(priority=row % 2)


def _row_gather_wait(src_hbm, dst, sem, row):
    pltpu.make_async_copy(src_hbm.at[pl.ds(0, 1), :], dst.at[pl.ds(row, 1), :], sem).wait()


def _gather_rows_start(src_hbm, dst, sem, idx_ref, idx0, rows):
    for r in range(rows):
        _row_gather_start(src_hbm, dst, sem, idx_ref[0, 0, idx0 + r], r)


def _gather_rows_wait(src_hbm, dst, sem, rows):
    def body(r, carry):
        _row_gather_wait(src_hbm, dst, sem, r)
        return carry

    lax.fori_loop(0, rows, body, 0, unroll=GATHER_UNROLL)


def _expert_kernel(blk_e_ref, n_used_ref, tok_ref, tok_next_ref, h2_hbm, wg_ref, wu_ref, wd_ref, o_ref,
                   xbuf, wg_bf, wu_bf, wd_bf, sem):
    i = pl.program_id(0)
    n_used = n_used_ref[0]
    e = blk_e_ref[i]
    prev = blk_e_ref[jnp.maximum(i - 1, 0)]
    rows = xbuf.shape[1]
    slot = i % 2

    @pl.when(i == 0)
    def _():
        _gather_rows_start(h2_hbm, xbuf.at[0], sem.at[0], tok_ref, 0, rows)

    @pl.when(i + 1 < n_used)
    def _():
        _gather_rows_start(h2_hbm, xbuf.at[1 - slot], sem.at[1 - slot], tok_next_ref, 0, rows)

    @pl.when(jnp.logical_or(i == 0, e != prev))
    def _():
        wg_bf[...] = wg_ref[0].astype(BF16)
        wu_bf[...] = wu_ref[0].astype(BF16)
        wd_bf[...] = wd_ref[0].astype(BF16)

    @pl.when(i < n_used)
    def _():
        _gather_rows_wait(h2_hbm, xbuf.at[slot], sem.at[slot], rows)
        for rs in _sub_blocks(rows, SUB_ROWS):
            x = xbuf[slot, rs, :].astype(BF16)
            gate = jnp.dot(x, wg_bf[...], preferred_element_type=F32)
            up = jnp.dot(x, wu_bf[...], preferred_element_type=F32)
            act = (gate * jax.nn.sigmoid(gate) * up).astype(BF16)
            o_ref[rs, :] = jnp.dot(act, wd_bf[...], preferred_element_type=F32)

    @pl.when(i >= n_used)
    def _():
        o_ref[...] = jnp.zeros_like(o_ref)


def _experts(h2, tok_blocks, blk_e, n_used, w_gate, w_up, w_down):
    n_blk = tok_blocks.shape[0]
    rows = tok_blocks.shape[2]
    d = h2.shape[1]
    de = w_gate.shape[2]
    return pl.pallas_call(
        _expert_kernel,
        grid_spec=pltpu.PrefetchScalarGridSpec(
            num_scalar_prefetch=2,
            grid=(n_blk,),
            in_specs=[pl.BlockSpec((1, 1, rows), lambda i, be, nu: (i, 0, 0), memory_space=pltpu.SMEM),
                      pl.BlockSpec((1, 1, rows), lambda i, be, nu: (jnp.minimum(i + 1, n_blk - 1), 0, 0),
                                   memory_space=pltpu.SMEM),
                      pl.BlockSpec(memory_space=pl.ANY),
                      pl.BlockSpec((1, d, de), lambda i, be, nu: (be[i], 0, 0)),
                      pl.BlockSpec((1, d, de), lambda i, be, nu: (be[i], 0, 0)),
                      pl.BlockSpec((1, de, d), lambda i, be, nu: (be[i], 0, 0))],
            out_specs=pl.BlockSpec((rows, d), lambda i, be, nu: (i, 0)),
            scratch_shapes=[pltpu.VMEM((2, rows, d), F32),
                            pltpu.VMEM((d, de), BF16),
                            pltpu.VMEM((d, de), BF16),
                            pltpu.VMEM((de, d), BF16),
                            pltpu.SemaphoreType.DMA((2,))]),
        out_shape=jax.ShapeDtypeStruct((n_blk * rows, d), F32),
        compiler_params=_params("arbitrary"),
        name="experts",
    )(blk_e, n_used, tok_blocks, tok_blocks, h2, w_gate, w_up, w_down)


def _final_kernel(seg_ref, pos_ref, pos_next_ref, x1_ref, mod_ref, rt_ref, lg_ref, lb_ref, ys_hbm, o_ref, buf, sem):
    del seg_ref
    i = pl.program_id(0)
    tm = x1_ref.shape[0]
    slot = i % 2

    def start_tile(idx_ref, s):
        for k in range(TOP_K):
            _gather_rows_start(ys_hbm, buf.at[s, k], sem.at[s], idx_ref, k * tm, tm)

    @pl.when(i == 0)
    def _():
        start_tile(pos_ref, 0)

    @pl.when(i + 1 < pl.num_programs(0))
    def _():
        start_tile(pos_next_ref, 1 - slot)

    for k in range(TOP_K):
        _gather_rows_wait(ys_hbm, buf.at[slot, k], sem.at[slot], tm)
    g2 = mod_ref[0, 5:6, :]
    for rs in _sub_blocks(tm, SUB_ROWS):
        rt = rt_ref[rs, :]
        ffn = rt[:, 0:1] * buf[slot, 0, rs, :] + rt[:, 1:2] * buf[slot, 1, rs, :]
        o_ref[rs, :] = _layer_norm(DEEPNORM_ALPHA * x1_ref[rs, :] + g2 * ffn, lg_ref[...], lb_ref[...])


def _final(x1, mod, tile_seg, pos_blocks, rt, lg, lb, ys, tile0, n_tiles, tm):
    d = x1.shape[1]
    row = lambda i, seg: (tile0 + i, 0)
    const = lambda i, seg: (0, 0)
    return pl.pallas_call(
        _final_kernel,
        grid_spec=pltpu.PrefetchScalarGridSpec(
            num_scalar_prefetch=1,
            grid=(n_tiles,),
            in_specs=[pl.BlockSpec((1, 1, TOP_K * tm), lambda i, seg: (tile0 + i, 0, 0), memory_space=pltpu.SMEM),
                      pl.BlockSpec((1, 1, TOP_K * tm),
                                   lambda i, seg: (tile0 + jnp.minimum(i + 1, n_tiles - 1), 0, 0),
                                   memory_space=pltpu.SMEM),
                      pl.BlockSpec((tm, d), row),
                      pl.BlockSpec((1, 8, d), lambda i, seg: (seg[tile0 + i], 0, 0)),
                      pl.BlockSpec((tm, LANES), row),
                      pl.BlockSpec((1, d), const),
                      pl.BlockSpec((1, d), const),
                      pl.BlockSpec(memory_space=pl.ANY)],
            out_specs=pl.BlockSpec((tm, d), lambda i, seg: (i, 0)),
            scratch_shapes=[pltpu.VMEM((2, TOP_K, tm, d), F32), pltpu.SemaphoreType.DMA((2,))]),
        out_shape=jax.ShapeDtypeStruct((n_tiles * tm, d), F32),
        compiler_params=_params("arbitrary"),
        name="combine_ln2",
    )(tile_seg, pos_blocks, pos_blocks, x1, mod, rt, lg, lb, ys)


def _rope_tables(pos_a, pos_b):
    inv = ROPE_THETA ** (-jnp.arange(0, 64, 2, dtype=F32) / 64)

    def part(pos):
        if pos is None:
            z = jnp.zeros((pos_a.shape[0], 32), F32)
            return z, z
        ang = pos.astype(F32)[:, None] * inv[None, :]
        return jnp.cos(ang), jnp.sin(ang)

    (ca, sa), (cb, sb) = part(pos_a), part(pos_b)
    return jnp.concatenate([ca, cb, ca, cb], 1), jnp.concatenate([-sa, -sb, sa, sb], 1)


def _pair_lanes(w):
    z = jnp.zeros(w.shape[:-1] + (32,), w.dtype)
    return jnp.concatenate([w[..., :32], z, w[..., 32:], z], axis=-1)


_AXIAL_PERM = np.asarray([(n % 64) // 32 * 64 + n // 64 * 32 + n % 32 for n in range(B_HEAD_DIM)])


def _dispatch(eid, rows):
    n = eid.shape[0]
    a = n * TOP_K
    i32 = jnp.int32
    iota = jnp.arange(a, dtype=i32)
    experts = jnp.arange(N_EXPERTS, dtype=i32)
    e_sorted, order = lax.sort((eid.reshape(-1), iota), num_keys=1, is_stable=True)
    starts = jnp.sum((e_sorted[None, :] < experts[:, None]).astype(i32), axis=1)
    counts = jnp.diff(starts, append=a)
    padded = ((counts + rows - 1) // rows) * rows
    pad_end = jnp.cumsum(padded)
    pad_start = pad_end - padded
    gap_step = jnp.diff(pad_start - starts, prepend=0)
    dest = iota + jnp.sum(jnp.where(e_sorted[:, None] >= experts[None, :], gap_step[None, :], 0), axis=1)
    _, pos = lax.sort((order, dest), num_keys=1)
    p_rows = ((a + rows - 1) // rows) * rows + N_EXPERTS * rows
    n_blk = p_rows // rows
    blk = jnp.arange(n_blk, dtype=i32)
    blk_e = jnp.minimum(jnp.sum((blk[:, None] * rows >= pad_end[None, :]).astype(i32), axis=1), N_EXPERTS - 1)
    onehot = blk_e[:, None] == experts[None, :]
    src0 = jnp.sum(jnp.where(onehot, (starts - pad_start)[None, :], 0), axis=1) + blk * rows
    src0 = jnp.minimum(src0, a)
    order_ext = jnp.concatenate([order, jnp.zeros((rows,), i32)])
    tok = jnp.take(order_ext, src0[:, None] + jnp.arange(rows, dtype=i32)[None, :], axis=0) // TOP_K
    n_used = (pad_end[-1] // rows).astype(i32).reshape(1)
    return tok.reshape(n_blk, 1, rows), pos.reshape(n, TOP_K), blk_e.astype(i32), n_used


def _layer(groups, w_ada, b_ada, w_in, a_q_norm, a_kv_norm, a_w_uq, a_w_ukv, a_w_o, b_q_norm, b_k_norm, b_w_o,
           w_out, ln1_g, ln1_b, w_group, b_group, w_expert, b_expert, e_w_gate, e_w_up, e_w_down, ln2_g, ln2_b):
    d = w_in.shape[0]
    seqs = [x.shape[1] for x, _ in groups]
    seg = functools.reduce(math.gcd, seqs)
    tm = _tile(seg, 512)
    s_max = max(seqs)

    xs = [x.reshape(-1, d) for x, _ in groups]
    n = sum(x.shape[0] for x in xs)
    c_all = jnp.concatenate([c for _, c in groups], axis=0)
    nb = c_all.shape[0]
    c_pad = jnp.zeros((-(-nb // 8) * 8, d), F32).at[:nb].set(c_all)
    seg_batch, tile_seg, tile_pos = [], [], []
    b0 = 0
    for x, _ in groups:
        bsz, s = x.shape[0], x.shape[1]
        for b in range(bsz):
            for t in range(s // tm):
                tile_seg.append(b0 + b)
                tile_pos.append(t)
        b0 += bsz
    tile_seg = jnp.asarray(np.asarray(tile_seg, np.int32))
    tile_pos = jnp.asarray(np.asarray(tile_pos, np.int32))

    mod = _ada(c_pad, w_ada, b_ada)[:nb].reshape(nb, 6, d)
    mod = jnp.concatenate([mod, jnp.zeros((nb, 2, d), F32)], axis=1)

    o_q, o_kv, o_kr, o_qb, o_kb, o_vb, o_ga, o_gb = np.cumsum(
        [0, A_Q_RANK, A_KV_RANK, A_ROPE, B_HEADS * B_HEAD_DIM, B_KV_HEADS * B_HEAD_DIM,
         B_KV_HEADS * B_HEAD_DIM, d]).tolist()
    w_lat = jnp.concatenate([w_in[:, :o_kr], _pair_lanes(w_in[:, o_kr:o_qb])], axis=1).astype(BF16)
    n_qk_heads = B_HEADS + B_KV_HEADS
    w_qk = w_in[:, o_qb:o_vb].reshape(d, n_qk_heads, B_HEAD_DIM)[:, :, _AXIAL_PERM].reshape(d, -1).astype(BF16)
    w_v = w_in[:, o_vb:o_ga].astype(BF16)
    w_g = w_in[:, o_ga:].astype(BF16)
    w_uq = a_w_uq.reshape(A_Q_RANK, A_HEADS, A_NOPE + A_ROPE)
    w_uq = jnp.concatenate([w_uq[:, :, :A_NOPE], _pair_lanes(w_uq[:, :, A_NOPE:])], axis=2)
    w_uq = w_uq.reshape(A_Q_RANK, A_HEADS * A_QK_PAD).astype(BF16)
    w_ukv = a_w_ukv.reshape(A_KV_RANK, A_HEADS, A_NOPE + A_V)
    w_uk = w_ukv[:, :, :A_NOPE].reshape(A_KV_RANK, A_HEADS * A_NOPE).astype(BF16)
    w_uv = w_ukv[:, :, A_NOPE:].reshape(A_KV_RANK, A_HEADS * A_V).astype(BF16)
    w_r = jnp.zeros((d, LANES), F32).at[:, :N_GROUPS].set(w_group).at[:, N_GROUPS:N_GROUPS + N_EXPERTS].set(w_expert)
    b_r = jnp.zeros((1, LANES), F32).at[0, :N_GROUPS].set(b_group).at[0, N_GROUPS:N_GROUPS + N_EXPERTS].set(b_expert)
    w_r_hi = w_r.astype(BF16)
    w_r = jnp.concatenate([w_r_hi, (w_r - w_r_hi.astype(F32)).astype(BF16)], axis=1)

    t = jnp.arange(s_max, dtype=jnp.int32)
    tabs_a = _rope_tables(t, None)
    tabs_b = _rope_tables(t // GRID_W, t % GRID_W)

    h, lat = _mod_proj(xs, mod, tile_seg, w_lat, tm)
    gains_b = jnp.stack([b_q_norm[_AXIAL_PERM], b_k_norm[_AXIAL_PERM]]).reshape(2, 1, B_HEAD_DIM)
    qk_b = _proj_qk(h, w_qk, gains_b, tabs_b, tile_pos, tm, B_HEADS)
    v_b = _proj_vt(h, w_v, tm, 512)
    q_a = _mla_q(lat, a_q_norm.reshape(1, -1), w_uq, tabs_a, tile_pos, tm)
    k_a, v_a = _mla_kv(lat, a_kv_norm.reshape(1, -1), w_uk, w_uv, tabs_a, tile_pos, tm)

    o_a = o_b = None
    row0 = 0
    for x, _ in groups:
        bsz, s = x.shape[0], x.shape[1]
        o_a = _flash(q_a, k_a, v_a, o_a, batch=bsz, seq=s, row0=row0, n_kv_heads=A_HEADS, group=1,
                     dq=A_QK_PAD, dv=A_V, k_col0=0, scale=(A_NOPE + A_ROPE) ** -0.5, tq=1024, tk=1024)
        o_b = _flash(qk_b, qk_b, v_b, o_b, batch=bsz, seq=s, row0=row0, n_kv_heads=B_KV_HEADS, group=B_GROUP,
                     dq=B_HEAD_DIM, dv=B_HEAD_DIM, k_col0=B_HEADS * B_HEAD_DIM, scale=B_HEAD_DIM ** -0.5,
                     tq=256, tk=1024)
        row0 += bsz * s

    gated = _gate_mix(h, o_a, o_b, w_g, a_w_o.astype(BF16), b_w_o.astype(BF16), tm, 512)
    tm2 = _tile(seg, 256)
    seg2 = jnp.repeat(tile_seg, tm // tm2)
    x1, h2, rt = _ln1(gated, xs, mod, seg2, w_out.astype(BF16), ln1_g.reshape(1, d), ln1_b.reshape(1, d),
                      w_r, b_r, tm2)

    eid = rt[:, 2:4].astype(jnp.int32)
    tok_blocks, pos, blk_e, n_used = _dispatch(eid, EXPERT_ROWS)
    ys = _experts(h2, tok_blocks, blk_e, n_used, e_w_gate, e_w_up, e_w_down)
    pos_blocks = pos.reshape(n // tm2, tm2, TOP_K).transpose(0, 2, 1).reshape(n // tm2, 1, TOP_K * tm2)

    outs = []
    tile0 = 0
    for x, _ in groups:
        bsz, s = x.shape[0], x.shape[1]
        nt = bsz * s // tm2
        y = _final(x1, mod, seg2, pos_blocks, rt, ln2_g.reshape(1, d), ln2_b.reshape(1, d), ys, tile0, nt, tm2)
        outs.append(y.reshape(bsz, s, d))
        tile0 += nt
    return tuple(outs)


def kernel(x_prompt, x_sample, c_prompt, c_sample, w_ada, b_ada, w_in, a_q_norm, a_kv_norm, a_w_uq, a_w_ukv, a_w_o,
           b_q_norm, b_k_norm, b_w_o, w_out, ln1_g, ln1_b, w_group, b_group, w_expert, b_expert, e_w_gate, e_w_up,
           e_w_down, ln2_g, ln2_b):
    weights = (w_ada, b_ada, w_in, a_q_norm, a_kv_norm, a_w_uq, a_w_ukv, a_w_o, b_q_norm, b_k_norm, b_w_o, w_out,
               ln1_g, ln1_b, w_group, b_group, w_expert, b_expert, e_w_gate, e_w_up, e_w_down, ln2_g, ln2_b)
    return _layer(((x_prompt, c_prompt), (x_sample, c_sample)), *[w[0] for w in weights])
```

```python
import functools
import math

import numpy as np
import jax
import jax.numpy as jnp
from jax import lax
from jax.experimental import pallas as pl
from jax.experimental.pallas import tpu as pltpu

F32 = jnp.float32
BF16 = jnp.bfloat16
HIGHEST = lax.Precision.HIGHEST

GRID_W = 64
ROPE_THETA = 10000.0
A_HEADS = 16
A_Q_RANK = 512
A_KV_RANK = 512
A_NOPE = 128
A_ROPE = 64
A_V = 128
A_QK_PAD = 256
B_HEADS = 16
B_KV_HEADS = 4
B_GROUP = B_HEADS // B_KV_HEADS
B_HEAD_DIM = 128
N_GROUPS = 8
EXPERTS_PER_GROUP = 8
N_EXPERTS = N_GROUPS * EXPERTS_PER_GROUP
TOP_K = 2
RMS_EPS = 1e-6
LN_EPS = 1e-5
DEPTH = 1
DEEPNORM_ALPHA = (2 * DEPTH) ** 0.25
LOG2E = 1.4426950408889634

LANES = 128
ROW_TILE = 512
FULL_WEIGHT_ROW_TILE = 256
COL_TILE = 512
ADA_COL_TILE = 1024
MLA_QUERY_TILE, MLA_KEY_TILE = 1024, 1024
GQA_QUERY_TILE, GQA_KEY_TILE = 256, 1024
EXPERT_ROWS = 256
SUB_ROWS = 128
GATHER_UNROLL = 8
ONES_ROWS = 16
VMEM_LIMIT = 48 * 1024 * 1024


def _params(*sem):
    return pltpu.CompilerParams(dimension_semantics=sem, vmem_limit_bytes=VMEM_LIMIT)


def _tile(n, pref):
    t = min(n, pref)
    while n % t:
        t //= 2
    return t


def _sub_blocks(rows, sub):
    sub = _tile(rows, sub)
    return [slice(r, r + sub) for r in range(0, rows, sub)]


def _ada_kernel(c_ref, w_ref, b_ref, o_ref):
    c = c_ref[...]
    a = c * jax.nn.sigmoid(c)
    o_ref[...] = jnp.dot(a, w_ref[...], precision=HIGHEST, preferred_element_type=F32) + b_ref[...]


def _ada(c_pad, w_ada, b_ada):
    d, n = w_ada.shape
    tn = _tile(n, ADA_COL_TILE)
    return pl.pallas_call(
        _ada_kernel,
        grid=(n // tn,),
        in_specs=[pl.BlockSpec(c_pad.shape, lambda j: (0, 0)),
                  pl.BlockSpec((d, tn), lambda j: (0, j)),
                  pl.BlockSpec((1, tn), lambda j: (0, j))],
        out_specs=pl.BlockSpec((c_pad.shape[0], tn), lambda j: (0, j)),
        out_shape=jax.ShapeDtypeStruct((c_pad.shape[0], n), F32),
        compiler_params=_params("arbitrary"),
        name="ada",
    )(c_pad, w_ada, b_ada.reshape(1, n))


def _group_specs(xs, tm):
    specs, bounds = [], []
    t0 = 0
    for x in xs:
        nt = x.shape[0] // tm
        specs.append(pl.BlockSpec((tm, x.shape[1]), lambda i, *_, t0=t0, nt=nt: (jnp.clip(i - t0, 0, nt - 1), 0)))
        t0 += nt
        bounds.append(t0)
    return specs, tuple(bounds[:-1])


def _live_rows(i, refs, bounds, rs):
    val = refs[-1][rs, :]
    for g in reversed(range(len(refs) - 1)):
        val = jnp.where(i < bounds[g], refs[g][rs, :], val)
    return val


def _mod_proj_kernel(seg_ref, *refs, bounds):
    del seg_ref
    *x_refs, mod_ref, w_ref, h_ref, o_ref = refs
    sh = mod_ref[0, 0:1, :]
    sc = mod_ref[0, 1:2, :]
    x = _live_rows(pl.program_id(0), x_refs, bounds, slice(None))
    h = (x * (1.0 + sc) + sh).astype(BF16)
    h_ref[...] = h
    o_ref[...] = jnp.dot(h, w_ref[...], preferred_element_type=F32)


def _mod_proj(xs, mod, tile_seg, w, tm):
    n = sum(x.shape[0] for x in xs)
    d = xs[0].shape[1]
    nc = w.shape[1]
    x_specs, bounds = _group_specs(xs, tm)
    return pl.pallas_call(
        functools.partial(_mod_proj_kernel, bounds=bounds),
        grid_spec=pltpu.PrefetchScalarGridSpec(
            num_scalar_prefetch=1,
            grid=(n // tm,),
            in_specs=x_specs + [pl.BlockSpec((1, 8, d), lambda i, seg: (seg[i], 0, 0)),
                                pl.BlockSpec((d, nc), lambda i, seg: (0, 0))],
            out_specs=[pl.BlockSpec((tm, d), lambda i, seg: (i, 0)),
                       pl.BlockSpec((tm, nc), lambda i, seg: (i, 0))]),
        out_shape=[jax.ShapeDtypeStruct((n, d), BF16), jax.ShapeDtypeStruct((n, nc), F32)],
        compiler_params=_params("arbitrary"),
        name="mod_proj",
    )(tile_seg, *xs, mod, w)


def _rope(x, cos, sin):
    return x * cos + pltpu.roll(x, 64, 1) * sin


def _store_vt(vt_ref, cols, v, dv):
    vt = v.T.astype(vt_ref.dtype)
    dve = dv + ONES_ROWS
    for hh in range(v.shape[1] // dv):
        vt_ref[hh * dve:hh * dve + dv, cols] = vt[hh * dv:(hh + 1) * dv, :]
        vt_ref[hh * dve + dv:(hh + 1) * dve, cols] = jnp.ones((ONES_ROWS, vt.shape[1]), vt_ref.dtype)


def _proj_vt_kernel(h_ref, w_ref, o_ref):
    for rs in _sub_blocks(h_ref.shape[0], SUB_ROWS):
        _store_vt(o_ref, rs, jnp.dot(h_ref[rs, :], w_ref[...], preferred_element_type=F32), B_HEAD_DIM)


def _proj_qk_kernel(pos_ref, h_ref, w_ref, g_ref, cos_ref, sin_ref, o_ref, *, n_q_heads):
    del pos_ref
    for rs in _sub_blocks(h_ref.shape[0], SUB_ROWS):
        acc = jnp.dot(h_ref[rs, :], w_ref[...], preferred_element_type=F32)
        cos, sin = cos_ref[rs, :], sin_ref[rs, :]
        for hh in range(acc.shape[1] // B_HEAD_DIM):
            x = acc[:, hh * B_HEAD_DIM:(hh + 1) * B_HEAD_DIM]
            r = lax.rsqrt(jnp.mean(x * x, axis=-1, keepdims=True) + RMS_EPS)
            y = x * r * g_ref[0 if hh < n_q_heads else 1]
            o_ref[rs, hh * B_HEAD_DIM:(hh + 1) * B_HEAD_DIM] = _rope(y, cos, sin).astype(o_ref.dtype)


def _proj_vt(h, w, tm, tn):
    n, d = h.shape
    nc = w.shape[1]
    tne = tn // B_HEAD_DIM * (B_HEAD_DIM + ONES_ROWS)
    return pl.pallas_call(
        _proj_vt_kernel,
        grid=(n // tm, nc // tn),
        in_specs=[pl.BlockSpec((tm, d), lambda i, j: (i, 0)),
                  pl.BlockSpec((d, tn), lambda i, j: (0, j))],
        out_specs=pl.BlockSpec((tne, tm), lambda i, j: (j, i)),
        out_shape=jax.ShapeDtypeStruct((nc // tn * tne, n), BF16),
        compiler_params=_params("arbitrary", "arbitrary"),
        name="proj_vt",
    )(h, w)


def _proj_qk(h, w, gains, tabs, tile_pos, tm, n_q_heads):
    n, d = h.shape
    nc = w.shape[1]
    tab_spec = pl.BlockSpec((tm, LANES), lambda i, pos: (pos[i], 0))
    return pl.pallas_call(
        functools.partial(_proj_qk_kernel, n_q_heads=n_q_heads),
        grid_spec=pltpu.PrefetchScalarGridSpec(
            num_scalar_prefetch=1,
            grid=(n // tm,),
            in_specs=[pl.BlockSpec((tm, d), lambda i, pos: (i, 0)),
                      pl.BlockSpec((d, nc), lambda i, pos: (0, 0)),
                      pl.BlockSpec((2, 1, LANES), lambda i, pos: (0, 0, 0)),
                      tab_spec, tab_spec],
            out_specs=pl.BlockSpec((tm, nc), lambda i, pos: (i, 0))),
        out_shape=jax.ShapeDtypeStruct((n, nc), BF16),
        compiler_params=_params("arbitrary"),
        name="proj_qk",
    )(tile_pos, h, w, gains, *tabs)


def _rms(x, g):
    return x * lax.rsqrt(jnp.mean(x * x, axis=-1, keepdims=True) + RMS_EPS) * g


def _mla_q_kernel(pos_ref, c_ref, g_ref, w_ref, cos_ref, sin_ref, o_ref):
    del pos_ref
    for rs in _sub_blocks(c_ref.shape[0], SUB_ROWS):
        xn = _rms(c_ref[rs, :], g_ref[...]).astype(BF16)
        acc = jnp.dot(xn, w_ref[...], preferred_element_type=F32)
        cos, sin = cos_ref[rs, :], sin_ref[rs, :]
        for hh in range(acc.shape[1] // A_QK_PAD):
            base = hh * A_QK_PAD
            o_ref[rs, base:base + A_NOPE] = acc[:, base:base + A_NOPE].astype(o_ref.dtype)
            o_ref[rs, base + A_NOPE:base + A_QK_PAD] = _rope(
                acc[:, base + A_NOPE:base + A_QK_PAD], cos, sin).astype(o_ref.dtype)


def _mla_q(cqkv, gain, w, tabs, tile_pos, tm):
    n = cqkv.shape[0]
    nc = w.shape[1]
    tn = nc
    tab_spec = pl.BlockSpec((tm, LANES), lambda i, j, pos: (pos[i], 0))
    return pl.pallas_call(
        _mla_q_kernel,
        grid_spec=pltpu.PrefetchScalarGridSpec(
            num_scalar_prefetch=1,
            grid=(n // tm, nc // tn),
            in_specs=[pl.BlockSpec((tm, A_Q_RANK), lambda i, j, pos: (i, 0)),
                      pl.BlockSpec((1, A_Q_RANK), lambda i, j, pos: (0, 0)),
                      pl.BlockSpec((A_Q_RANK, tn), lambda i, j, pos: (0, j)),
                      tab_spec, tab_spec],
            out_specs=pl.BlockSpec((tm, tn), lambda i, j, pos: (i, j))),
        out_shape=jax.ShapeDtypeStruct((n, nc), BF16),
        compiler_params=_params("arbitrary", "arbitrary"),
        name="mla_q",
    )(tile_pos, cqkv, gain, w, *tabs)


def _mla_kv_kernel(pos_ref, c_ref, kr_ref, g_ref, wk_ref, wv_ref, cos_ref, sin_ref, k_ref, v_ref):
    del pos_ref
    for rs in _sub_blocks(c_ref.shape[0], SUB_ROWS):
        xn = _rms(c_ref[rs, :], g_ref[...]).astype(BF16)
        kn = jnp.dot(xn, wk_ref[...], preferred_element_type=F32)
        _store_vt(v_ref, rs, jnp.dot(xn, wv_ref[...], preferred_element_type=F32), A_V)
        kr = _rope(kr_ref[rs, :], cos_ref[rs, :], sin_ref[rs, :]).astype(k_ref.dtype)
        for hh in range(kn.shape[1] // A_NOPE):
            k_ref[rs, hh * A_QK_PAD:hh * A_QK_PAD + A_NOPE] = kn[:, hh * A_NOPE:(hh + 1) * A_NOPE].astype(k_ref.dtype)
            k_ref[rs, hh * A_QK_PAD + A_NOPE:(hh + 1) * A_QK_PAD] = kr


def _mla_kv(cqkv, gain, wk, wv, tabs, tile_pos, tm):
    n = cqkv.shape[0]
    hpt = A_HEADS
    n_ct = A_HEADS // hpt
    kr_blk = (A_Q_RANK + A_KV_RANK) // LANES
    tab_spec = pl.BlockSpec((tm, LANES), lambda i, j, pos: (pos[i], 0))
    return pl.pallas_call(
        _mla_kv_kernel,
        grid_spec=pltpu.PrefetchScalarGridSpec(
            num_scalar_prefetch=1,
            grid=(n // tm, n_ct),
            in_specs=[pl.BlockSpec((tm, A_KV_RANK), lambda i, j, pos: (i, 1)),
                      pl.BlockSpec((tm, LANES), lambda i, j, pos: (i, kr_blk)),
                      pl.BlockSpec((1, A_KV_RANK), lambda i, j, pos: (0, 0)),
                      pl.BlockSpec((A_KV_RANK, hpt * A_NOPE), lambda i, j, pos: (0, j)),
                      pl.BlockSpec((A_KV_RANK, hpt * A_V), lambda i, j, pos: (0, j)),
                      tab_spec, tab_spec],
            out_specs=[pl.BlockSpec((tm, hpt * A_QK_PAD), lambda i, j, pos: (i, j)),
                       pl.BlockSpec((hpt * (A_V + ONES_ROWS), tm), lambda i, j, pos: (j, i))]),
        out_shape=[jax.ShapeDtypeStruct((n, A_HEADS * A_QK_PAD), BF16),
                   jax.ShapeDtypeStruct((A_HEADS * (A_V + ONES_ROWS), n), BF16)],
        compiler_params=_params("arbitrary", "arbitrary"),
        name="mla_kv",
    )(tile_pos, cqkv, cqkv, gain, wk, wv, *tabs)


def _flash_kernel(q_ref, qn_ref, k_ref, vt_ref, o_ref, q_sc, qn_sc, m_sc, acc_sc, s_sc, *, group, dq, dv, tk, c):
    tq = q_ref.shape[0]
    n_kv = k_ref.shape[0] // tk
    for g in range(group):
        q_sc[g * tq:(g + 1) * tq, :] = (q_ref[:, g * dq:(g + 1) * dq].astype(F32) * c).astype(BF16)
        qn_sc[g * tq:(g + 1) * tq, :] = (qn_ref[:, g * dq:(g + 1) * dq].astype(F32) * c).astype(BF16)
    m_sc[...] = jnp.full(m_sc.shape, -1e30, F32)
    acc_sc[...] = jnp.zeros(acc_sc.shape, F32)

    def scores(j, queries):
        k = k_ref[pl.ds(pl.multiple_of(j * tk, tk), tk), :]
        return lax.dot_general(k, queries[...], (((1,), (1,)), ((), ())), preferred_element_type=F32)

    def step(j, slot, prefetch):
        s = s_sc[slot]
        if prefetch:
            s_sc[1 - slot] = scores(j + 1, q_sc)
        else:
            s_sc[1 - slot] = scores(0, qn_sc)
        vt = vt_ref[:, pl.ds(pl.multiple_of(j * tk, tk), tk)]
        m_prev = m_sc[...]
        m_new = jnp.maximum(m_prev, jnp.max(s, axis=0, keepdims=True))
        alpha = jnp.exp2(m_prev - m_new)
        p = jnp.exp2(s - m_new).astype(BF16)
        acc_sc[...] = alpha * acc_sc[...] + jnp.dot(vt, p, preferred_element_type=F32)
        m_sc[...] = m_new

    @pl.when(pl.program_id(2) == 0)
    def _():
        s_sc[0] = scores(0, q_sc)

    def pair(jj, carry):
        step(2 * jj, 0, True)
        step(2 * jj + 1, 1, True)
        return carry

    lax.fori_loop(0, n_kv // 2 - 1, pair, 0)
    step(n_kv - 2, 0, True)
    step(n_kv - 1, 1, False)
    acc = acc_sc[...]
    out = (acc[:dv] / acc[dv:dv + 1]).T
    for g in range(group):
        o_ref[:, g * dv:(g + 1) * dv] = out[g * tq:(g + 1) * tq, :].astype(o_ref.dtype)


def _flash_into_kernel(q_ref, qn_ref, k_ref, vt_ref, prev_ref, o_ref, *scratch, **kw):
    del prev_ref
    _flash_kernel(q_ref, qn_ref, k_ref, vt_ref, o_ref, *scratch, **kw)


def _flash(q2d, k2d, vt2d, out, *, batch, seq, row0, n_kv_heads, group, dq, dv, k_col0, scale, tq, tk):
    assert row0 % seq == 0 and k_col0 % dq == 0
    n = q2d.shape[0]
    tq = _tile(seq, tq)
    tk = _tile(seq // 2, tk)
    qb0, kb0 = row0 // tq, row0 // seq
    nq = seq // tq
    kc0 = k_col0 // dq
    rows = group * tq
    dve = dv + ONES_ROWS
    kw = dict(group=group, dq=dq, dv=dv, tk=tk, c=scale * LOG2E)
    in_specs = [pl.BlockSpec((tq, group * dq), lambda b, h, i: (qb0 + b * nq + i, h)),
                pl.BlockSpec((tq, group * dq), lambda b, h, i: (qb0 + b * nq + jnp.minimum(i + 1, nq - 1), h)),
                pl.BlockSpec((seq, dq), lambda b, h, i: (kb0 + b, kc0 + h)),
                pl.BlockSpec((dve, seq), lambda b, h, i: (h, kb0 + b))]
    if out is None:
        kern, operands, aliases = functools.partial(_flash_kernel, **kw), (q2d, q2d, k2d, vt2d), {}
    else:
        kern, operands, aliases = functools.partial(_flash_into_kernel, **kw), (q2d, q2d, k2d, vt2d, out), {4: 0}
        in_specs.append(pl.BlockSpec(memory_space=pl.ANY))
    return pl.pallas_call(
        kern,
        grid=(batch, n_kv_heads, nq),
        in_specs=in_specs,
        out_specs=pl.BlockSpec((tq, group * dv), lambda b, h, i: (qb0 + b * nq + i, h)),
        out_shape=jax.ShapeDtypeStruct((n, n_kv_heads * group * dv), BF16),
        scratch_shapes=[pltpu.VMEM((rows, dq), BF16), pltpu.VMEM((rows, dq), BF16), pltpu.VMEM((1, rows), F32),
                        pltpu.VMEM((dve, rows), F32), pltpu.VMEM((2, tk, rows), F32)],
        input_output_aliases=aliases,
        compiler_params=_params("arbitrary", "arbitrary", "arbitrary"),
        name="flash",
    )(*operands)


def _gate_mix_kernel(h_ref, oa_ref, ob_ref, wga_ref, wgb_ref, wa_ref, wb_ref, o_ref):
    for rs in _sub_blocks(h_ref.shape[0], 2 * SUB_ROWS):
        h = h_ref[rs, :]
        ga = jax.nn.sigmoid(jnp.dot(h, wga_ref[...], preferred_element_type=F32))
        gb = jax.nn.sigmoid(jnp.dot(h, wgb_ref[...], preferred_element_type=F32))
        ya = jnp.dot(oa_ref[rs, :], wa_ref[...], preferred_element_type=F32)
        yb = jnp.dot(ob_ref[rs, :], wb_ref[...], preferred_element_type=F32)
        o_ref[rs, :] = (ga * ya + gb * yb).astype(o_ref.dtype)


def _gate_mix(h, oa, ob, w_gates, wa, wb, tm, tn):
    n, d = oa.shape
    dm = wa.shape[1]
    nct = dm // tn
    rows = pl.BlockSpec((tm, d), lambda i, j: (i, 0))
    cols = pl.BlockSpec((d, tn), lambda i, j: (0, j))
    return pl.pallas_call(
        _gate_mix_kernel,
        grid=(n // tm, nct),
        in_specs=[rows, rows, rows, cols, pl.BlockSpec((d, tn), lambda i, j: (0, nct + j)), cols, cols],
        out_specs=pl.BlockSpec((tm, tn), lambda i, j: (i, j)),
        out_shape=jax.ShapeDtypeStruct((n, dm), BF16),
        compiler_params=_params("arbitrary", "arbitrary"),
        name="gate_mix",
    )(h, oa, ob, w_gates, w_gates, wa, wb)


def _layer_norm(y, g, b):
    mu = jnp.mean(y, axis=-1, keepdims=True)
    yc = y - mu
    var = jnp.mean(yc * yc, axis=-1, keepdims=True)
    return yc * lax.rsqrt(var + LN_EPS) * g + b


def _route(logits):
    lane = lax.broadcasted_iota(jnp.int32, logits.shape, 1).astype(F32)
    big = float(LANES)
    ninf = -jnp.inf
    gmask = lane < N_GROUPS
    gl = jnp.where(gmask, logits, ninf)
    gmax = jnp.max(gl, axis=-1, keepdims=True)
    grp = jnp.min(jnp.where(gl == gmax, lane, big), axis=-1, keepdims=True)
    den = jnp.sum(jnp.exp(gl - gmax), axis=-1, keepdims=True)
    p_grp = 1.0 / den
    lo = N_GROUPS + grp * EXPERTS_PER_GROUP
    el = jnp.where(lane >= lo, jnp.where(lane < lo + EXPERTS_PER_GROUP, logits, ninf), ninf)
    t1 = jnp.max(el, axis=-1, keepdims=True)
    i1 = jnp.min(jnp.where(el == t1, lane, big), axis=-1, keepdims=True)
    el2 = jnp.where(lane == i1, ninf, el)
    t2 = jnp.max(el2, axis=-1, keepdims=True)
    i2 = jnp.min(jnp.where(el2 == t2, lane, big), axis=-1, keepdims=True)
    e = jnp.exp(t2 - t1)
    w1 = p_grp / (1.0 + e)
    w2 = p_grp * e / (1.0 + e)
    out = jnp.where(lane == 0, w1, jnp.where(lane == 1, w2, jnp.where(lane == 2, i1 - N_GROUPS, i2 - N_GROUPS)))
    return jnp.where(lane < 4, out, 0.0)


def _ln1_kernel(seg_ref, g_ref, *refs, bounds):
    del seg_ref
    *x_refs, mod_ref, w_ref, lg_ref, lb_ref, wr_ref, br_ref, x1_ref, h2_ref, rt_ref = refs
    i = pl.program_id(0)
    g1 = mod_ref[0, 2:3, :]
    sh2 = mod_ref[0, 3:4, :]
    sc2 = mod_ref[0, 4:5, :]
    for rs in _sub_blocks(g_ref.shape[0], SUB_ROWS):
        attn = jnp.dot(g_ref[rs, :], w_ref[...], preferred_element_type=F32)
        x = _live_rows(i, x_refs, bounds, rs)
        x1 = _layer_norm(DEEPNORM_ALPHA * x + g1 * attn, lg_ref[...], lb_ref[...])
        x1_ref[rs, :] = x1
        h2 = x1 * (1.0 + sc2) + sh2
        h2_ref[rs, :] = h2
        h_hi = h2.astype(BF16)
        h_lo = (h2 - h_hi.astype(F32)).astype(BF16)
        a = jnp.dot(h_hi, wr_ref[...], preferred_element_type=F32)
        b = jnp.dot(h_lo, wr_ref[:, :LANES], preferred_element_type=F32)
        rt_ref[rs, :] = _route(a[:, :LANES] + a[:, LANES:] + b + br_ref[...])


def _ln1(gated, xs, mod, tile_seg, w_out, lg, lb, wr, br, tm):
    n, d = gated.shape
    row = lambda i, seg: (i, 0)
    const = lambda i, seg: (0, 0)
    x_specs, bounds = _group_specs(xs, tm)
    return pl.pallas_call(
        functools.partial(_ln1_kernel, bounds=bounds),
        grid_spec=pltpu.PrefetchScalarGridSpec(
            num_scalar_prefetch=1,
            grid=(n // tm,),
            in_specs=[pl.BlockSpec((tm, d), row)] + x_specs + [
                      pl.BlockSpec((1, 8, d), lambda i, seg: (seg[i], 0, 0)),
                      pl.BlockSpec((d, d), const),
                      pl.BlockSpec((1, d), const),
                      pl.BlockSpec((1, d), const),
                      pl.BlockSpec((d, 2 * LANES), const),
                      pl.BlockSpec((1, LANES), const)],
            out_specs=[pl.BlockSpec((tm, d), row), pl.BlockSpec((tm, d), row), pl.BlockSpec((tm, LANES), row)]),
        out_shape=[jax.ShapeDtypeStruct((n, d), F32), jax.ShapeDtypeStruct((n, d), F32),
                   jax.ShapeDtypeStruct((n, LANES), F32)],
        compiler_params=_params("arbitrary"),
        name="ln1_route",
    )(tile_seg, gated, *xs, mod, w_out, lg, lb, wr, br)


def _row_gather_start(src_hbm, dst, sem, idx, row):
    pltpu.make_async_copy(src_hbm.at[pl.ds(idx, 1), :], dst.at[pl.ds(row, 1), :], sem).start(priority=row % 2)


def _row_gather_wait(src_hbm, dst, sem, row):
    pltpu.make_async_copy(src_hbm.at[pl.ds(0, 1), :], dst.at[pl.ds(row, 1), :], sem).wait()


def _gather_rows_start(src_hbm, dst, sem, idx_ref, idx0, rows):
    for r in range(rows):
        _row_gather_start(src_hbm, dst, sem, idx_ref[0, 0, idx0 + r], r)


def _gather_rows_wait(src_hbm, dst, sem, rows):
    def body(r, carry):
        _row_gather_wait(src_hbm, dst, sem, r)
        return carry

    lax.fori_loop(0, rows, body, 0, unroll=GATHER_UNROLL)


def _expert_kernel(blk_e_ref, n_used_ref, tok_ref, tok_next_ref, h2_hbm, wg_ref, wu_ref, wd_ref, o_ref,
                   xbuf, wg_bf, wu_bf, wd_bf, sem):
    i = pl.program_id(0)
    n_used = n_used_ref[0]
    e = blk_e_ref[i]
    prev = blk_e_ref[jnp.maximum(i - 1, 0)]
    rows = xbuf.shape[1]
    slot = i % 2

    @pl.when(i == 0)
    def _():
        _gather_rows_start(h2_hbm, xbuf.at[0], sem.at[0], tok_ref, 0, rows)

    @pl.when(i + 1 < n_used)
    def _():
        _gather_rows_start(h2_hbm, xbuf.at[1 - slot], sem.at[1 - slot], tok_next_ref, 0, rows)

    @pl.when(jnp.logical_or(i == 0, e != prev))
    def _():
        wg_bf[...] = wg_ref[0].astype(BF16)
        wu_bf[...] = wu_ref[0].astype(BF16)
        wd_bf[...] = wd_ref[0].astype(BF16)

    @pl.when(i < n_used)
    def _():
        _gather_rows_wait(h2_hbm, xbuf.at[slot], sem.at[slot], rows)
        for rs in _sub_blocks(rows, SUB_ROWS):
            x = xbuf[slot, rs, :].astype(BF16)
            gate = jnp.dot(x, wg_bf[...], preferred_element_type=F32)
            up = jnp.dot(x, wu_bf[...], preferred_element_type=F32)
            act = (gate * jax.nn.sigmoid(gate) * up).astype(BF16)
            o_ref[rs, :] = jnp.dot(act, wd_bf[...], preferred_element_type=F32)

    @pl.when(i >= n_used)
    def _():
        o_ref[...] = jnp.zeros_like(o_ref)


def _experts(h2, tok_blocks, blk_e, n_used, w_gate, w_up, w_down):
    n_blk = tok_blocks.shape[0]
    rows = tok_blocks.shape[2]
    d = h2.shape[1]
    de = w_gate.shape[2]
    return pl.pallas_call(
        _expert_kernel,
        grid_spec=pltpu.PrefetchScalarGridSpec(
            num_scalar_prefetch=2,
            grid=(n_blk,),
            in_specs=[pl.BlockSpec((1, 1, rows), lambda i, be, nu: (i, 0, 0), memory_space=pltpu.SMEM),
                      pl.BlockSpec((1, 1, rows), lambda i, be, nu: (jnp.minimum(i + 1, n_blk - 1), 0, 0),
                                   memory_space=pltpu.SMEM),
                      pl.BlockSpec(memory_space=pl.ANY),
                      pl.BlockSpec((1, d, de), lambda i, be, nu: (be[i], 0, 0)),
                      pl.BlockSpec((1, d, de), lambda i, be, nu: (be[i], 0, 0)),
                      pl.BlockSpec((1, de, d), lambda i, be, nu: (be[i], 0, 0))],
            out_specs=pl.BlockSpec((rows, d), lambda i, be, nu: (i, 0)),
            scratch_shapes=[pltpu.VMEM((2, rows, d), F32),
                            pltpu.VMEM((d, de), BF16),
                            pltpu.VMEM((d, de), BF16),
                            pltpu.VMEM((de, d), BF16),
                            pltpu.SemaphoreType.DMA((2,))]),
        out_shape=jax.ShapeDtypeStruct((n_blk * rows, d), F32),
        compiler_params=_params("arbitrary"),
        name="experts",
    )(blk_e, n_used, tok_blocks, tok_blocks, h2, w_gate, w_up, w_down)


def _final_kernel(seg_ref, pos_ref, pos_next_ref, x1_ref, mod_ref, rt_ref, lg_ref, lb_ref, ys_hbm, o_ref, buf, sem):
    del seg_ref
    i = pl.program_id(0)
    tm = x1_ref.shape[0]
    slot = i % 2

    def start_tile(idx_ref, s):
        for k in range(TOP_K):
            _gather_rows_start(ys_hbm, buf.at[s, k], sem.at[s], idx_ref, k * tm, tm)

    @pl.when(i == 0)
    def _():
        start_tile(pos_ref, 0)

    @pl.when(i + 1 < pl.num_programs(0))
    def _():
        start_tile(pos_next_ref, 1 - slot)

    for k in range(TOP_K):
        _gather_rows_wait(ys_hbm, buf.at[slot, k], sem.at[slot], tm)
    g2 = mod_ref[0, 5:6, :]
    for rs in _sub_blocks(tm, SUB_ROWS):
        rt = rt_ref[rs, :]
        ffn = rt[:, 0:1] * buf[slot, 0, rs, :] + rt[:, 1:2] * buf[slot, 1, rs, :]
        o_ref[rs, :] = _layer_norm(DEEPNORM_ALPHA * x1_ref[rs, :] + g2 * ffn, lg_ref[...], lb_ref[...])


def _final(x1, mod, tile_seg, pos_blocks, rt, lg, lb, ys, tile0, n_tiles, tm):
    d = x1.shape[1]
    row = lambda i, seg: (tile0 + i, 0)
    const = lambda i, seg: (0, 0)
    return pl.pallas_call(
        _final_kernel,
        grid_spec=pltpu.PrefetchScalarGridSpec(
            num_scalar_prefetch=1,
            grid=(n_tiles,),
            in_specs=[pl.BlockSpec((1, 1, TOP_K * tm), lambda i, seg: (tile0 + i, 0, 0), memory_space=pltpu.SMEM),
                      pl.BlockSpec((1, 1, TOP_K * tm),
                                   lambda i, seg: (tile0 + jnp.minimum(i + 1, n_tiles - 1), 0, 0),
                                   memory_space=pltpu.SMEM),
                      pl.BlockSpec((tm, d), row),
                      pl.BlockSpec((1, 8, d), lambda i, seg: (seg[tile0 + i], 0, 0)),
                      pl.BlockSpec((tm, LANES), row),
                      pl.BlockSpec((1, d), const),
                      pl.BlockSpec((1, d), const),
                      pl.BlockSpec(memory_space=pl.ANY)],
            out_specs=pl.BlockSpec((tm, d), lambda i, seg: (i, 0)),
            scratch_shapes=[pltpu.VMEM((2, TOP_K, tm, d), F32), pltpu.SemaphoreType.DMA((2,))]),
        out_shape=jax.ShapeDtypeStruct((n_tiles * tm, d), F32),
        compiler_params=_params("arbitrary"),
        name="combine_ln2",
    )(tile_seg, pos_blocks, pos_blocks, x1, mod, rt, lg, lb, ys)


def _rope_tables(pos_a, pos_b):
    inv = ROPE_THETA ** (-jnp.arange(0, 64, 2, dtype=F32) / 64)

    def part(pos):
        if pos is None:
            z = jnp.zeros((pos_a.shape[0], 32), F32)
            return z, z
        ang = pos.astype(F32)[:, None] * inv[None, :]
        return jnp.cos(ang), jnp.sin(ang)

    (ca, sa), (cb, sb) = part(pos_a), part(pos_b)
    return jnp.concatenate([ca, cb, ca, cb], 1), jnp.concatenate([-sa, -sb, sa, sb], 1)


def _pair_lanes(w):
    z = jnp.zeros(w.shape[:-1] + (32,), w.dtype)
    return jnp.concatenate([w[..., :32], z, w[..., 32:], z], axis=-1)


_AXIAL_PERM = np.asarray([(n % 64) // 32 * 64 + n // 64 * 32 + n % 32 for n in range(B_HEAD_DIM)])


def _dispatch(eid, rows):
    n = eid.shape[0]
    a = n * TOP_K
    i32 = jnp.int32
    iota = jnp.arange(a, dtype=i32)
    experts = jnp.arange(N_EXPERTS, dtype=i32)
    e_sorted, order = lax.sort((eid.reshape(-1), iota), num_keys=1, is_stable=True)
    starts = jnp.sum((e_sorted[None, :] < experts[:, None]).astype(i32), axis=1)
    counts = jnp.diff(starts, append=a)
    padded = ((counts + rows - 1) // rows) * rows
    pad_end = jnp.cumsum(padded)
    pad_start = pad_end - padded
    gap_step = jnp.diff(pad_start - starts, prepend=0)
    dest = iota + jnp.sum(jnp.where(e_sorted[:, None] >= experts[None, :], gap_step[None, :], 0), axis=1)
    _, pos = lax.sort((order, dest), num_keys=1)
    p_rows = ((a + rows - 1) // rows) * rows + N_EXPERTS * rows
    n_blk = p_rows // rows
    blk = jnp.arange(n_blk, dtype=i32)
    blk_e = jnp.minimum(jnp.sum((blk[:, None] * rows >= pad_end[None, :]).astype(i32), axis=1), N_EXPERTS - 1)
    onehot = blk_e[:, None] == experts[None, :]
    src0 = jnp.sum(jnp.where(onehot, (starts - pad_start)[None, :], 0), axis=1) + blk * rows
    src0 = jnp.minimum(src0, a)
    order_ext = jnp.concatenate([order, jnp.zeros((rows,), i32)])
    tok = jnp.take(order_ext, src0[:, None] + jnp.arange(rows, dtype=i32)[None, :], axis=0) // TOP_K
    n_used = (pad_end[-1] // rows).astype(i32).reshape(1)
    return tok.reshape(n_blk, 1, rows), pos.reshape(n, TOP_K), blk_e.astype(i32), n_used


def _layer(groups, w_ada, b_ada, w_in, a_q_norm, a_kv_norm, a_w_uq, a_w_ukv, a_w_o, b_q_norm, b_k_norm, b_w_o,
           w_out, ln1_g, ln1_b, w_group, b_group, w_expert, b_expert, e_w_gate, e_w_up, e_w_down, ln2_g, ln2_b):
    d = w_in.shape[0]
    seqs = [x.shape[1] for x, _ in groups]
    seg = functools.reduce(math.gcd, seqs)
    tm = _tile(seg, ROW_TILE)
    s_max = max(seqs)

    xs = [x.reshape(-1, d) for x, _ in groups]
    n = sum(x.shape[0] for x in xs)
    c_all = jnp.concatenate([c for _, c in groups], axis=0)
    nb = c_all.shape[0]
    c_pad = jnp.zeros((-(-nb // 8) * 8, d), F32).at[:nb].set(c_all)
    tile_seg, tile_pos = [], []
    b0 = 0
    for x, _ in groups:
        bsz, s = x.shape[0], x.shape[1]
        for b in range(bsz):
            for t in range(s // tm):
                tile_seg.append(b0 + b)
                tile_pos.append(t)
        b0 += bsz
    tile_seg = jnp.asarray(np.asarray(tile_seg, np.int32))
    tile_pos = jnp.asarray(np.asarray(tile_pos, np.int32))

    mod = _ada(c_pad, w_ada, b_ada)[:nb].reshape(nb, 6, d)
    mod = jnp.concatenate([mod, jnp.zeros((nb, 2, d), F32)], axis=1)

    o_q, o_kv, o_kr, o_qb, o_kb, o_vb, o_ga, o_gb = np.cumsum(
        [0, A_Q_RANK, A_KV_RANK, A_ROPE, B_HEADS * B_HEAD_DIM, B_KV_HEADS * B_HEAD_DIM,
         B_KV_HEADS * B_HEAD_DIM, d]).tolist()
    w_lat = jnp.concatenate([w_in[:, :o_kr], _pair_lanes(w_in[:, o_kr:o_qb])], axis=1).astype(BF16)
    n_qk_heads = B_HEADS + B_KV_HEADS
    w_qk = w_in[:, o_qb:o_vb].reshape(d, n_qk_heads, B_HEAD_DIM)[:, :, _AXIAL_PERM].reshape(d, -1).astype(BF16)
    w_v = w_in[:, o_vb:o_ga].astype(BF16)
    w_g = w_in[:, o_ga:].astype(BF16)
    w_uq = a_w_uq.reshape(A_Q_RANK, A_HEADS, A_NOPE + A_ROPE)
    w_uq = jnp.concatenate([w_uq[:, :, :A_NOPE], _pair_lanes(w_uq[:, :, A_NOPE:])], axis=2)
    w_uq = w_uq.reshape(A_Q_RANK, A_HEADS * A_QK_PAD).astype(BF16)
    w_ukv = a_w_ukv.reshape(A_KV_RANK, A_HEADS, A_NOPE + A_V)
    w_uk = w_ukv[:, :, :A_NOPE].reshape(A_KV_RANK, A_HEADS * A_NOPE).astype(BF16)
    w_uv = w_ukv[:, :, A_NOPE:].reshape(A_KV_RANK, A_HEADS * A_V).astype(BF16)
    w_r = jnp.zeros((d, LANES), F32).at[:, :N_GROUPS].set(w_group).at[:, N_GROUPS:N_GROUPS + N_EXPERTS].set(w_expert)
    b_r = jnp.zeros((1, LANES), F32).at[0, :N_GROUPS].set(b_group).at[0, N_GROUPS:N_GROUPS + N_EXPERTS].set(b_expert)
    w_r_hi = w_r.astype(BF16)
    w_r = jnp.concatenate([w_r_hi, (w_r - w_r_hi.astype(F32)).astype(BF16)], axis=1)

    t = jnp.arange(s_max, dtype=jnp.int32)
    tabs_a = _rope_tables(t, None)
    tabs_b = _rope_tables(t // GRID_W, t % GRID_W)

    h, lat = _mod_proj(xs, mod, tile_seg, w_lat, tm)
    gains_b = jnp.stack([b_q_norm[_AXIAL_PERM], b_k_norm[_AXIAL_PERM]]).reshape(2, 1, B_HEAD_DIM)
    qk_b = _proj_qk(h, w_qk, gains_b, tabs_b, tile_pos, tm, B_HEADS)
    v_b = _proj_vt(h, w_v, tm, COL_TILE)
    q_a = _mla_q(lat, a_q_norm.reshape(1, -1), w_uq, tabs_a, tile_pos, tm)
    k_a, v_a = _mla_kv(lat, a_kv_norm.reshape(1, -1), w_uk, w_uv, tabs_a, tile_pos, tm)

    o_a = o_b = None
    row0 = 0
    for x, _ in groups:
        bsz, s = x.shape[0], x.shape[1]
        o_a = _flash(q_a, k_a, v_a, o_a, batch=bsz, seq=s, row0=row0, n_kv_heads=A_HEADS, group=1,
                     dq=A_QK_PAD, dv=A_V, k_col0=0, scale=(A_NOPE + A_ROPE) ** -0.5,
                     tq=MLA_QUERY_TILE, tk=MLA_KEY_TILE)
        o_b = _flash(qk_b, qk_b, v_b, o_b, batch=bsz, seq=s, row0=row0, n_kv_heads=B_KV_HEADS, group=B_GROUP,
                     dq=B_HEAD_DIM, dv=B_HEAD_DIM, k_col0=B_HEADS * B_HEAD_DIM, scale=B_HEAD_DIM ** -0.5,
                     tq=GQA_QUERY_TILE, tk=GQA_KEY_TILE)
        row0 += bsz * s

    gated = _gate_mix(h, o_a, o_b, w_g, a_w_o.astype(BF16), b_w_o.astype(BF16), tm, COL_TILE)
    tm2 = _tile(seg, FULL_WEIGHT_ROW_TILE)
    seg2 = jnp.repeat(tile_seg, tm // tm2)
    x1, h2, rt = _ln1(gated, xs, mod, seg2, w_out.astype(BF16), ln1_g.reshape(1, d), ln1_b.reshape(1, d),
                      w_r, b_r, tm2)

    eid = rt[:, 2:4].astype(jnp.int32)
    tok_blocks, pos, blk_e, n_used = _dispatch(eid, EXPERT_ROWS)
    ys = _experts(h2, tok_blocks, blk_e, n_used, e_w_gate, e_w_up, e_w_down)
    pos_blocks = pos.reshape(n // tm2, tm2, TOP_K).transpose(0, 2, 1).reshape(n // tm2, 1, TOP_K * tm2)

    outs = []
    tile0 = 0
    for x, _ in groups:
        bsz, s = x.shape[0], x.shape[1]
        nt = bsz * s // tm2
        y = _final(x1, mod, seg2, pos_blocks, rt, ln2_g.reshape(1, d), ln2_b.reshape(1, d), ys, tile0, nt, tm2)
        outs.append(y.reshape(bsz, s, d))
        tile0 += nt
    return tuple(outs)


def kernel(x_prompt, x_sample, c_prompt, c_sample, w_ada, b_ada, w_in, a_q_norm, a_kv_norm, a_w_uq, a_w_ukv, a_w_o,
           b_q_norm, b_k_norm, b_w_o, w_out, ln1_g, ln1_b, w_group, b_group, w_expert, b_expert, e_w_gate, e_w_up,
           e_w_down, ln2_g, ln2_b):
    weights = (w_ada, b_ada, w_in, a_q_norm, a_kv_norm, a_w_uq, a_w_ukv, a_w_o, b_q_norm, b_k_norm, b_w_o, w_out,
               ln1_g, ln1_b, w_group, b_group, w_expert, b_expert, e_w_gate, e_w_up, e_w_down, ln2_g, ln2_b)
    return _layer(((x_prompt, c_prompt), (x_sample, c_sample)), *[w[0] for w in weights])
```

```python
import functools
import math

import numpy as np
import jax
import jax.numpy as jnp
from jax import lax
from jax.experimental import pallas as pl
from jax.experimental.pallas import tpu as pltpu

F32 = jnp.float32
BF16 = jnp.bfloat16
HIGHEST = lax.Precision.HIGHEST

GRID_W = 64
ROPE_THETA = 10000.0
A_HEADS = 16
A_Q_RANK = 512
A_KV_RANK = 512
A_NOPE = 128
A_ROPE = 64
A_V = 128
A_QK_PAD = 256
B_HEADS = 16
B_KV_HEADS = 4
B_GROUP = B_HEADS // B_KV_HEADS
B_HEAD_DIM = 128
N_GROUPS = 8
EXPERTS_PER_GROUP = 8
N_EXPERTS = N_GROUPS * EXPERTS_PER_GROUP
TOP_K = 2
RMS_EPS = 1e-6
LN_EPS = 1e-5
DEPTH = 1
DEEPNORM_ALPHA = (2 * DEPTH) ** 0.25
LOG2E = 1.4426950408889634

LANES = 128
ROW_TILE = 512
FULL_WEIGHT_ROW_TILE = 256
COL_TILE = 512
ADA_COL_TILE = 1024
MLA_QUERY_TILE, MLA_KEY_TILE = 1024, 1024
GQA_QUERY_TILE, GQA_KEY_TILE = 256, 1024
EXPERT_ROWS = 256
SUB_ROWS = 128
GATHER_UNROLL = 8
ONES_ROWS = 16
VMEM_LIMIT = 48 * 1024 * 1024


def _params(*sem):
    return pltpu.CompilerParams(dimension_semantics=sem, vmem_limit_bytes=VMEM_LIMIT)


def _tile(n, pref):
    t = min(n, pref)
    while n % t:
        t //= 2
    return t


def _sub_blocks(rows, sub):
    sub = _tile(rows, sub)
    return [slice(r, r + sub) for r in range(0, rows, sub)]


def _ada_kernel(c_ref, w_ref, b_ref, o_ref):
    c = c_ref[...]
    a = c * jax.nn.sigmoid(c)
    o_ref[...] = jnp.dot(a, w_ref[...], precision=HIGHEST, preferred_element_type=F32) + b_ref[...]


def _ada(c_pad, w_ada, b_ada):
    d, n = w_ada.shape
    tn = _tile(n, ADA_COL_TILE)
    return pl.pallas_call(
        _ada_kernel,
        grid=(n // tn,),
        in_specs=[pl.BlockSpec(c_pad.shape, lambda j: (0, 0)),
                  pl.BlockSpec((d, tn), lambda j: (0, j)),
                  pl.BlockSpec((1, tn), lambda j: (0, j))],
        out_specs=pl.BlockSpec((c_pad.shape[0], tn), lambda j: (0, j)),
        out_shape=jax.ShapeDtypeStruct((c_pad.shape[0], n), F32),
        compiler_params=_params("arbitrary"),
        name="ada",
    )(c_pad, w_ada, b_ada.reshape(1, n))


def _group_specs(xs, tm):
    specs, bounds = [], []
    t0 = 0
    for x in xs:
        nt = x.shape[0] // tm
        specs.append(pl.BlockSpec((tm, x.shape[1]), lambda i, *_, t0=t0, nt=nt: (jnp.clip(i - t0, 0, nt - 1), 0)))
        t0 += nt
        bounds.append(t0)
    return specs, tuple(bounds[:-1])


def _live_rows(i, refs, bounds, rs):
    val = refs[-1][rs, :]
    for g in reversed(range(len(refs) - 1)):
        val = jnp.where(i < bounds[g], refs[g][rs, :], val)
    return val


def _mod_proj_kernel(seg_ref, *refs, bounds):
    del seg_ref
    *x_refs, mod_ref, w_ref, h_ref, o_ref = refs
    sh = mod_ref[0, 0:1, :]
    sc = mod_ref[0, 1:2, :]
    x = _live_rows(pl.program_id(0), x_refs, bounds, slice(None))
    h = (x * (1.0 + sc) + sh).astype(BF16)
    h_ref[...] = h
    o_ref[...] = jnp.dot(h, w_ref[...], preferred_element_type=F32)


def _mod_proj(xs, mod, tile_seg, w, tm):
    n = sum(x.shape[0] for x in xs)
    d = xs[0].shape[1]
    nc = w.shape[1]
    x_specs, bounds = _group_specs(xs, tm)
    return pl.pallas_call(
        functools.partial(_mod_proj_kernel, bounds=bounds),
        grid_spec=pltpu.PrefetchScalarGridSpec(
            num_scalar_prefetch=1,
            grid=(n // tm,),
            in_specs=x_specs + [pl.BlockSpec((1, 8, d), lambda i, seg: (seg[i], 0, 0)),
                                pl.BlockSpec((d, nc), lambda i, seg: (0, 0))],
            out_specs=[pl.BlockSpec((tm, d), lambda i, seg: (i, 0)),
                       pl.BlockSpec((tm, nc), lambda i, seg: (i, 0))]),
        out_shape=[jax.ShapeDtypeStruct((n, d), BF16), jax.ShapeDtypeStruct((n, nc), F32)],
        compiler_params=_params("arbitrary"),
        name="mod_proj",
    )(tile_seg, *xs, mod, w)


def _rope(x, cos, sin):
    return x * cos + pltpu.roll(x, 64, 1) * sin


def _store_vt(vt_ref, cols, v, dv):
    vt = v.T.astype(vt_ref.dtype)
    dve = dv + ONES_ROWS
    for hh in range(v.shape[1] // dv):
        vt_ref[hh * dve:hh * dve + dv, cols] = vt[hh * dv:(hh + 1) * dv, :]
        vt_ref[hh * dve + dv:(hh + 1) * dve, cols] = jnp.ones((ONES_ROWS, vt.shape[1]), vt_ref.dtype)


def _proj_vt_kernel(h_ref, w_ref, o_ref):
    for rs in _sub_blocks(h_ref.shape[0], SUB_ROWS):
        _store_vt(o_ref, rs, jnp.dot(h_ref[rs, :], w_ref[...], preferred_element_type=F32), B_HEAD_DIM)


def _proj_qk_kernel(pos_ref, h_ref, w_ref, g_ref, cos_ref, sin_ref, o_ref, *, n_q_heads):
    del pos_ref
    for rs in _sub_blocks(h_ref.shape[0], SUB_ROWS):
        acc = jnp.dot(h_ref[rs, :], w_ref[...], preferred_element_type=F32)
        cos, sin = cos_ref[rs, :], sin_ref[rs, :]
        for hh in range(acc.shape[1] // B_HEAD_DIM):
            x = acc[:, hh * B_HEAD_DIM:(hh + 1) * B_HEAD_DIM]
            r = lax.rsqrt(jnp.mean(x * x, axis=-1, keepdims=True) + RMS_EPS)
            y = x * r * g_ref[0 if hh < n_q_heads else 1]
            o_ref[rs, hh * B_HEAD_DIM:(hh + 1) * B_HEAD_DIM] = _rope(y, cos, sin).astype(o_ref.dtype)


def _proj_vt(h, w, tm, tn):
    n, d = h.shape
    nc = w.shape[1]
    tne = tn // B_HEAD_DIM * (B_HEAD_DIM + ONES_ROWS)
    return pl.pallas_call(
        _proj_vt_kernel,
        grid=(n // tm, nc // tn),
        in_specs=[pl.BlockSpec((tm, d), lambda i, j: (i, 0)),
                  pl.BlockSpec((d, tn), lambda i, j: (0, j))],
        out_specs=pl.BlockSpec((tne, tm), lambda i, j: (j, i)),
        out_shape=jax.ShapeDtypeStruct((nc // tn * tne, n), BF16),
        compiler_params=_params("arbitrary", "arbitrary"),
        name="proj_vt",
    )(h, w)


def _proj_qk(h, w, gains, tabs, tile_pos, tm, n_q_heads):
    n, d = h.shape
    nc = w.shape[1]
    tab_spec = pl.BlockSpec((tm, LANES), lambda i, pos: (pos[i], 0))
    return pl.pallas_call(
        functools.partial(_proj_qk_kernel, n_q_heads=n_q_heads),
        grid_spec=pltpu.PrefetchScalarGridSpec(
            num_scalar_prefetch=1,
            grid=(n // tm,),
            in_specs=[pl.BlockSpec((tm, d), lambda i, pos: (i, 0)),
                      pl.BlockSpec((d, nc), lambda i, pos: (0, 0)),
                      pl.BlockSpec((2, 1, LANES), lambda i, pos: (0, 0, 0)),
                      tab_spec, tab_spec],
            out_specs=pl.BlockSpec((tm, nc), lambda i, pos: (i, 0))),
        out_shape=jax.ShapeDtypeStruct((n, nc), BF16),
        compiler_params=_params("arbitrary"),
        name="proj_qk",
    )(tile_pos, h, w, gains, *tabs)


def _rms(x, g):
    return x * lax.rsqrt(jnp.mean(x * x, axis=-1, keepdims=True) + RMS_EPS) * g


def _mla_q_kernel(pos_ref, c_ref, g_ref, w_ref, cos_ref, sin_ref, o_ref):
    del pos_ref
    for rs in _sub_blocks(c_ref.shape[0], SUB_ROWS):
        xn = _rms(c_ref[rs, :], g_ref[...]).astype(BF16)
        acc = jnp.dot(xn, w_ref[...], preferred_element_type=F32)
        cos, sin = cos_ref[rs, :], sin_ref[rs, :]
        for hh in range(acc.shape[1] // A_QK_PAD):
            base = hh * A_QK_PAD
            o_ref[rs, base:base + A_NOPE] = acc[:, base:base + A_NOPE].astype(o_ref.dtype)
            o_ref[rs, base + A_NOPE:base + A_QK_PAD] = _rope(
                acc[:, base + A_NOPE:base + A_QK_PAD], cos, sin).astype(o_ref.dtype)


def _mla_q(cqkv, gain, w, tabs, tile_pos, tm):
    n = cqkv.shape[0]
    nc = w.shape[1]
    tn = nc
    tab_spec = pl.BlockSpec((tm, LANES), lambda i, j, pos: (pos[i], 0))
    return pl.pallas_call(
        _mla_q_kernel,
        grid_spec=pltpu.PrefetchScalarGridSpec(
            num_scalar_prefetch=1,
            grid=(n // tm, nc // tn),
            in_specs=[pl.BlockSpec((tm, A_Q_RANK), lambda i, j, pos: (i, 0)),
                      pl.BlockSpec((1, A_Q_RANK), lambda i, j, pos: (0, 0)),
                      pl.BlockSpec((A_Q_RANK, tn), lambda i, j, pos: (0, j)),
                      tab_spec, tab_spec],
            out_specs=pl.BlockSpec((tm, tn), lambda i, j, pos: (i, j))),
        out_shape=jax.ShapeDtypeStruct((n, nc), BF16),
        compiler_params=_params("arbitrary", "arbitrary"),
        name="mla_q",
    )(tile_pos, cqkv, gain, w, *tabs)


def _mla_kv_kernel(pos_ref, c_ref, kr_ref, g_ref, wk_ref, wv_ref, cos_ref, sin_ref, k_ref, v_ref):
    del pos_ref
    for rs in _sub_blocks(c_ref.shape[0], SUB_ROWS):
        xn = _rms(c_ref[rs, :], g_ref[...]).astype(BF16)
        kn = jnp.dot(xn, wk_ref[...], preferred_element_type=F32)
        _store_vt(v_ref, rs, jnp.dot(xn, wv_ref[...], preferred_element_type=F32), A_V)
        kr = _rope(kr_ref[rs, :], cos_ref[rs, :], sin_ref[rs, :]).astype(k_ref.dtype)
        for hh in range(kn.shape[1] // A_NOPE):
            k_ref[rs, hh * A_QK_PAD:hh * A_QK_PAD + A_NOPE] = kn[:, hh * A_NOPE:(hh + 1) * A_NOPE].astype(k_ref.dtype)
            k_ref[rs, hh * A_QK_PAD + A_NOPE:(hh + 1) * A_QK_PAD] = kr


def _mla_kv(cqkv, gain, wk, wv, tabs, tile_pos, tm):
    n = cqkv.shape[0]
    hpt = A_HEADS
    n_ct = A_HEADS // hpt
    kr_blk = (A_Q_RANK + A_KV_RANK) // LANES
    tab_spec = pl.BlockSpec((tm, LANES), lambda i, j, pos: (pos[i], 0))
    return pl.pallas_call(
        _mla_kv_kernel,
        grid_spec=pltpu.PrefetchScalarGridSpec(
            num_scalar_prefetch=1,
            grid=(n // tm, n_ct),
            in_specs=[pl.BlockSpec((tm, A_KV_RANK), lambda i, j, pos: (i, 1)),
                      pl.BlockSpec((tm, LANES), lambda i, j, pos: (i, kr_blk)),
                      pl.BlockSpec((1, A_KV_RANK), lambda i, j, pos: (0, 0)),
                      pl.BlockSpec((A_KV_RANK, hpt * A_NOPE), lambda i, j, pos: (0, j)),
                      pl.BlockSpec((A_KV_RANK, hpt * A_V), lambda i, j, pos: (0, j)),
                      tab_spec, tab_spec],
            out_specs=[pl.BlockSpec((tm, hpt * A_QK_PAD), lambda i, j, pos: (i, j)),
                       pl.BlockSpec((hpt * (A_V + ONES_ROWS), tm), lambda i, j, pos: (j, i))]),
        out_shape=[jax.ShapeDtypeStruct((n, A_HEADS * A_QK_PAD), BF16),
                   jax.ShapeDtypeStruct((A_HEADS * (A_V + ONES_ROWS), n), BF16)],
        compiler_params=_params("arbitrary", "arbitrary"),
        name="mla_kv",
    )(tile_pos, cqkv, cqkv, gain, wk, wv, *tabs)


def _flash_kernel(q_ref, qn_ref, k_ref, vt_ref, o_ref, q_sc, qn_sc, m_sc, acc_sc, s_sc, *, group, dq, dv, tk, c):
    tq = q_ref.shape[0]
    n_kv = k_ref.shape[0] // tk
    for g in range(group):
        q_sc[g * tq:(g + 1) * tq, :] = (q_ref[:, g * dq:(g + 1) * dq].astype(F32) * c).astype(BF16)
        qn_sc[g * tq:(g + 1) * tq, :] = (qn_ref[:, g * dq:(g + 1) * dq].astype(F32) * c).astype(BF16)
    m_sc[...] = jnp.full(m_sc.shape, -1e30, F32)
    acc_sc[...] = jnp.zeros(acc_sc.shape, F32)

    def scores(j, queries):
        k = k_ref[pl.ds(pl.multiple_of(j * tk, tk), tk), :]
        return lax.dot_general(k, queries[...], (((1,), (1,)), ((), ())), preferred_element_type=F32)

    def step(j, slot, prefetch):
        s = s_sc[slot]
        if prefetch:
            s_sc[1 - slot] = scores(j + 1, q_sc)
        else:
            s_sc[1 - slot] = scores(0, qn_sc)
        vt = vt_ref[:, pl.ds(pl.multiple_of(j * tk, tk), tk)]
        m_prev = m_sc[...]
        m_new = jnp.maximum(m_prev, jnp.max(s, axis=0, keepdims=True))
        alpha = jnp.exp2(m_prev - m_new)
        p = jnp.exp2(s - m_new).astype(BF16)
        acc_sc[...] = alpha * acc_sc[...] + jnp.dot(vt, p, preferred_element_type=F32)
        m_sc[...] = m_new

    @pl.when(pl.program_id(2) == 0)
    def _():
        s_sc[0] = scores(0, q_sc)

    def pair(jj, carry):
        step(2 * jj, 0, True)
        step(2 * jj + 1, 1, True)
        return carry

    lax.fori_loop(0, n_kv // 2 - 1, pair, 0)
    step(n_kv - 2, 0, True)
    step(n_kv - 1, 1, False)
    acc = acc_sc[...]
    out = (acc[:dv] / acc[dv:dv + 1]).T
    for g in range(group):
        o_ref[:, g * dv:(g + 1) * dv] = out[g * tq:(g + 1) * tq, :].astype(o_ref.dtype)


def _flash_into_kernel(q_ref, qn_ref, k_ref, vt_ref, prev_ref, o_ref, *scratch, **kw):
    del prev_ref
    _flash_kernel(q_ref, qn_ref, k_ref, vt_ref, o_ref, *scratch, **kw)


def _flash(q2d, k2d, vt2d, out, *, batch, seq, row0, n_kv_heads, group, dq, dv, k_col0, scale, tq, tk):
    assert row0 % seq == 0 and k_col0 % dq == 0
    n = q2d.shape[0]
    tq = _tile(seq, tq)
    tk = _tile(seq // 2, tk)
    qb0, kb0 = row0 // tq, row0 // seq
    nq = seq // tq
    kc0 = k_col0 // dq
    rows = group * tq
    dve = dv + ONES_ROWS
    kw = dict(group=group, dq=dq, dv=dv, tk=tk, c=scale * LOG2E)
    in_specs = [pl.BlockSpec((tq, group * dq), lambda b, h, i: (qb0 + b * nq + i, h)),
                pl.BlockSpec((tq, group * dq), lambda b, h, i: (qb0 + b * nq + jnp.minimum(i + 1, nq - 1), h)),
                pl.BlockSpec((seq, dq), lambda b, h, i: (kb0 + b, kc0 + h)),
                pl.BlockSpec((dve, seq), lambda b, h, i: (h, kb0 + b))]
    if out is None:
        kern, operands, aliases = functools.partial(_flash_kernel, **kw), (q2d, q2d, k2d, vt2d), {}
    else:
        kern, operands, aliases = functools.partial(_flash_into_kernel, **kw), (q2d, q2d, k2d, vt2d, out), {4: 0}
        in_specs.append(pl.BlockSpec(memory_space=pl.ANY))
    return pl.pallas_call(
        kern,
        grid=(batch, n_kv_heads, nq),
        in_specs=in_specs,
        out_specs=pl.BlockSpec((tq, group * dv), lambda b, h, i: (qb0 + b * nq + i, h)),
        out_shape=jax.ShapeDtypeStruct((n, n_kv_heads * group * dv), BF16),
        scratch_shapes=[pltpu.VMEM((rows, dq), BF16), pltpu.VMEM((rows, dq), BF16), pltpu.VMEM((1, rows), F32),
                        pltpu.VMEM((dve, rows), F32), pltpu.VMEM((2, tk, rows), F32)],
        input_output_aliases=aliases,
        compiler_params=_params("arbitrary", "arbitrary", "arbitrary"),
        name="flash",
    )(*operands)


def _gate_mix_kernel(h_ref, oa_ref, ob_ref, wga_ref, wgb_ref, wa_ref, wb_ref, o_ref):
    for rs in _sub_blocks(h_ref.shape[0], 2 * SUB_ROWS):
        h = h_ref[rs, :]
        ga = jax.nn.sigmoid(jnp.dot(h, wga_ref[...], preferred_element_type=F32))
        gb = jax.nn.sigmoid(jnp.dot(h, wgb_ref[...], preferred_element_type=F32))
        ya = jnp.dot(oa_ref[rs, :], wa_ref[...], preferred_element_type=F32)
        yb = jnp.dot(ob_ref[rs, :], wb_ref[...], preferred_element_type=F32)
        o_ref[rs, :] = (ga * ya + gb * yb).astype(o_ref.dtype)


def _gate_mix(h, oa, ob, w_gates, wa, wb, tm, tn):
    n, d = oa.shape
    dm = wa.shape[1]
    nct = dm // tn
    rows = pl.BlockSpec((tm, d), lambda i, j: (i, 0))
    cols = pl.BlockSpec((d, tn), lambda i, j: (0, j))
    return pl.pallas_call(
        _gate_mix_kernel,
        grid=(n // tm, nct),
        in_specs=[rows, rows, rows, cols, pl.BlockSpec((d, tn), lambda i, j: (0, nct + j)), cols, cols],
        out_specs=pl.BlockSpec((tm, tn), lambda i, j: (i, j)),
        out_shape=jax.ShapeDtypeStruct((n, dm), BF16),
        compiler_params=_params("arbitrary", "arbitrary"),
        name="gate_mix",
    )(h, oa, ob, w_gates, w_gates, wa, wb)


def _layer_norm(y, g, b):
    mu = jnp.mean(y, axis=-1, keepdims=True)
    yc = y - mu
    var = jnp.mean(yc * yc, axis=-1, keepdims=True)
    return yc * lax.rsqrt(var + LN_EPS) * g + b


def _route(logits):
    lane = lax.broadcasted_iota(jnp.int32, logits.shape, 1).astype(F32)
    big = float(LANES)
    ninf = -jnp.inf
    gmask = lane < N_GROUPS
    gl = jnp.where(gmask, logits, ninf)
    gmax = jnp.max(gl, axis=-1, keepdims=True)
    grp = jnp.min(jnp.where(gl == gmax, lane, big), axis=-1, keepdims=True)
    den = jnp.sum(jnp.exp(gl - gmax), axis=-1, keepdims=True)
    p_grp = 1.0 / den
    lo = N_GROUPS + grp * EXPERTS_PER_GROUP
    el = jnp.where(lane >= lo, jnp.where(lane < lo + EXPERTS_PER_GROUP, logits, ninf), ninf)
    t1 = jnp.max(el, axis=-1, keepdims=True)
    i1 = jnp.min(jnp.where(el == t1, lane, big), axis=-1, keepdims=True)
    el2 = jnp.where(lane == i1, ninf, el)
    t2 = jnp.max(el2, axis=-1, keepdims=True)
    i2 = jnp.min(jnp.where(el2 == t2, lane, big), axis=-1, keepdims=True)
    e = jnp.exp(t2 - t1)
    w1 = p_grp / (1.0 + e)
    w2 = p_grp * e / (1.0 + e)
    out = jnp.where(lane == 0, w1, jnp.where(lane == 1, w2, jnp.where(lane == 2, i1 - N_GROUPS, i2 - N_GROUPS)))
    return jnp.where(lane < 4, out, 0.0)


def _ln1_kernel(seg_ref, g_ref, *refs, bounds):
    del seg_ref
    *x_refs, mod_ref, w_ref, lg_ref, lb_ref, wr_ref, br_ref, x1_ref, h2_ref, rt_ref = refs
    i = pl.program_id(0)
    g1 = mod_ref[0, 2:3, :]
    sh2 = mod_ref[0, 3:4, :]
    sc2 = mod_ref[0, 4:5, :]
    for rs in _sub_blocks(g_ref.shape[0], SUB_ROWS):
        attn = jnp.dot(g_ref[rs, :], w_ref[...], preferred_element_type=F32)
        x = _live_rows(i, x_refs, bounds, rs)
        x1 = _layer_norm(DEEPNORM_ALPHA * x + g1 * attn, lg_ref[...], lb_ref[...])
        x1_ref[rs, :] = x1
        h2 = x1 * (1.0 + sc2) + sh2
        h2_ref[rs, :] = h2
        h_hi = h2.astype(BF16)
        h_lo = (h2 - h_hi.astype(F32)).astype(BF16)
        a = jnp.dot(h_hi, wr_ref[...], preferred_element_type=F32)
        b = jnp.dot(h_lo, wr_ref[:, :LANES], preferred_element_type=F32)
        rt_ref[rs, :] = _route(a[:, :LANES] + a[:, LANES:] + b + br_ref[...])


def _ln1(gated, xs, mod, tile_seg, w_out, lg, lb, wr, br, tm):
    n, d = gated.shape
    row = lambda i, seg: (i, 0)
    const = lambda i, seg: (0, 0)
    x_specs, bounds = _group_specs(xs, tm)
    return pl.pallas_call(
        functools.partial(_ln1_kernel, bounds=bounds),
        grid_spec=pltpu.PrefetchScalarGridSpec(
            num_scalar_prefetch=1,
            grid=(n // tm,),
            in_specs=[pl.BlockSpec((tm, d), row)] + x_specs + [
                      pl.BlockSpec((1, 8, d), lambda i, seg: (seg[i], 0, 0)),
                      pl.BlockSpec((d, d), const),
                      pl.BlockSpec((1, d), const),
                      pl.BlockSpec((1, d), const),
                      pl.BlockSpec((d, 2 * LANES), const),
                      pl.BlockSpec((1, LANES), const)],
            out_specs=[pl.BlockSpec((tm, d), row), pl.BlockSpec((tm, d), row), pl.BlockSpec((tm, LANES), row)]),
        out_shape=[jax.ShapeDtypeStruct((n, d), F32), jax.ShapeDtypeStruct((n, d), F32),
                   jax.ShapeDtypeStruct((n, LANES), F32)],
        compiler_params=_params("arbitrary"),
        name="ln1_route",
    )(tile_seg, gated, *xs, mod, w_out, lg, lb, wr, br)


def _row_gather_start(src_hbm, dst, sem, idx, row):
    pltpu.make_async_copy(src_hbm.at[pl.ds(idx, 1), :], dst.at[pl.ds(row, 1), :], sem).start()


def _row_gather_wait(src_hbm, dst, sem, row):
    pltpu.make_async_copy(src_hbm.at[pl.ds(0, 1), :], dst.at[pl.ds(row, 1), :], sem).wait()


def _gather_rows_start(src_hbm, dst, sem, idx_ref, idx0, rows):
    for r in range(rows):
        _row_gather_start(src_hbm, dst, sem, idx_ref[0, 0, idx0 + r], r)


def _gather_rows_wait(src_hbm, dst, sem, rows):
    def body(r, carry):
        _row_gather_wait(src_hbm, dst, sem, r)
        return carry

    lax.fori_loop(0, rows, body, 0, unroll=GATHER_UNROLL)


def _expert_kernel(blk_e_ref, n_used_ref, tok_ref, tok_next_ref, h2_hbm, wg_ref, wu_ref, wd_ref, o_ref,
                   xbuf, wg_bf, wu_bf, wd_bf, sem):
    i = pl.program_id(0)
    n_used = n_used_ref[0]
    e = blk_e_ref[i]
    prev = blk_e_ref[jnp.maximum(i - 1, 0)]
    rows = xbuf.shape[1]
    slot = i % 2

    @pl.when(i == 0)
    def _():
        _gather_rows_start(h2_hbm, xbuf.at[0], sem.at[0], tok_ref, 0, rows)

    @pl.when(i + 1 < n_used)
    def _():
        _gather_rows_start(h2_hbm, xbuf.at[1 - slot], sem.at[1 - slot], tok_next_ref, 0, rows)

    @pl.when(jnp.logical_or(i == 0, e != prev))
    def _():
        wg_bf[...] = wg_ref[0].astype(BF16)
        wu_bf[...] = wu_ref[0].astype(BF16)
        wd_bf[...] = wd_ref[0].astype(BF16)

    @pl.when(i < n_used)
    def _():
        _gather_rows_wait(h2_hbm, xbuf.at[slot], sem.at[slot], rows)
        for rs in _sub_blocks(rows, SUB_ROWS):
            x = xbuf[slot, rs, :].astype(BF16)
            gate = jnp.dot(x, wg_bf[...], preferred_element_type=F32)
            up = jnp.dot(x, wu_bf[...], preferred_element_type=F32)
            act = (gate * jax.nn.sigmoid(gate) * up).astype(BF16)
            o_ref[rs, :] = jnp.dot(act, wd_bf[...], preferred_element_type=F32)

    @pl.when(i >= n_used)
    def _():
        o_ref[...] = jnp.zeros_like(o_ref)


def _experts(h2, tok_blocks, blk_e, n_used, w_gate, w_up, w_down):
    n_blk = tok_blocks.shape[0]
    rows = tok_blocks.shape[2]
    d = h2.shape[1]
    de = w_gate.shape[2]
    return pl.pallas_call(
        _expert_kernel,
        grid_spec=pltpu.PrefetchScalarGridSpec(
            num_scalar_prefetch=2,
            grid=(n_blk,),
            in_specs=[pl.BlockSpec((1, 1, rows), lambda i, be, nu: (i, 0, 0), memory_space=pltpu.SMEM),
                      pl.BlockSpec((1, 1, rows), lambda i, be, nu: (jnp.minimum(i + 1, n_blk - 1), 0, 0),
                                   memory_space=pltpu.SMEM),
                      pl.BlockSpec(memory_space=pl.ANY),
                      pl.BlockSpec((1, d, de), lambda i, be, nu: (be[i], 0, 0)),
                      pl.BlockSpec((1, d, de), lambda i, be, nu: (be[i], 0, 0)),
                      pl.BlockSpec((1, de, d), lambda i, be, nu: (be[i], 0, 0))],
            out_specs=pl.BlockSpec((rows, d), lambda i, be, nu: (i, 0)),
            scratch_shapes=[pltpu.VMEM((2, rows, d), F32),
                            pltpu.VMEM((d, de), BF16),
                            pltpu.VMEM((d, de), BF16),
                            pltpu.VMEM((de, d), BF16),
                            pltpu.SemaphoreType.DMA((2,))]),
        out_shape=jax.ShapeDtypeStruct((n_blk * rows, d), F32),
        compiler_params=_params("arbitrary"),
        name="experts",
    )(blk_e, n_used, tok_blocks, tok_blocks, h2, w_gate, w_up, w_down)


def _final_kernel(seg_ref, pos_ref, pos_next_ref, x1_ref, mod_ref, rt_ref, lg_ref, lb_ref, ys_hbm, o_ref, buf, sem):
    del seg_ref
    i = pl.program_id(0)
    tm = x1_ref.shape[0]
    slot = i % 2

    def start_tile(idx_ref, s):
        for k in range(TOP_K):
            _gather_rows_start(ys_hbm, buf.at[s, k], sem.at[s], idx_ref, k * tm, tm)

    @pl.when(i == 0)
    def _():
        start_tile(pos_ref, 0)

    @pl.when(i + 1 < pl.num_programs(0))
    def _():
        start_tile(pos_next_ref, 1 - slot)

    for k in range(TOP_K):
        _gather_rows_wait(ys_hbm, buf.at[slot, k], sem.at[slot], tm)
    g2 = mod_ref[0, 5:6, :]
    for rs in _sub_blocks(tm, SUB_ROWS):
        rt = rt_ref[rs, :]
        ffn = rt[:, 0:1] * buf[slot, 0, rs, :] + rt[:, 1:2] * buf[slot, 1, rs, :]
        o_ref[rs, :] = _layer_norm(DEEPNORM_ALPHA * x1_ref[rs, :] + g2 * ffn, lg_ref[...], lb_ref[...])


def _final(x1, mod, tile_seg, pos_blocks, rt, lg, lb, ys, tile0, n_tiles, tm):
    d = x1.shape[1]
    row = lambda i, seg: (tile0 + i, 0)
    const = lambda i, seg: (0, 0)
    return pl.pallas_call(
        _final_kernel,
        grid_spec=pltpu.PrefetchScalarGridSpec(
            num_scalar_prefetch=1,
            grid=(n_tiles,),
            in_specs=[pl.BlockSpec((1, 1, TOP_K * tm), lambda i, seg: (tile0 + i, 0, 0), memory_space=pltpu.SMEM),
                      pl.BlockSpec((1, 1, TOP_K * tm),
                                   lambda i, seg: (tile0 + jnp.minimum(i + 1, n_tiles - 1), 0, 0),
                                   memory_space=pltpu.SMEM),
                      pl.BlockSpec((tm, d), row),
                      pl.BlockSpec((1, 8, d), lambda i, seg: (seg[tile0 + i], 0, 0)),
                      pl.BlockSpec((tm, LANES), row),
                      pl.BlockSpec((1, d), const),
                      pl.BlockSpec((1, d), const),
                      pl.BlockSpec(memory_space=pl.ANY)],
            out_specs=pl.BlockSpec((tm, d), lambda i, seg: (i, 0)),
            scratch_shapes=[pltpu.VMEM((2, TOP_K, tm, d), F32), pltpu.SemaphoreType.DMA((2,))]),
        out_shape=jax.ShapeDtypeStruct((n_tiles * tm, d), F32),
        compiler_params=_params("arbitrary"),
        name="combine_ln2",
    )(tile_seg, pos_blocks, pos_blocks, x1, mod, rt, lg, lb, ys)


def _rope_tables(pos_a, pos_b):
    inv = ROPE_THETA ** (-jnp.arange(0, 64, 2, dtype=F32) / 64)

    def part(pos):
        if pos is None:
            z = jnp.zeros((pos_a.shape[0], 32), F32)
            return z, z
        ang = pos.astype(F32)[:, None] * inv[None, :]
        return jnp.cos(ang), jnp.sin(ang)

    (ca, sa), (cb, sb) = part(pos_a), part(pos_b)
    return jnp.concatenate([ca, cb, ca, cb], 1), jnp.concatenate([-sa, -sb, sa, sb], 1)


def _pair_lanes(w):
    z = jnp.zeros(w.shape[:-1] + (32,), w.dtype)
    return jnp.concatenate([w[..., :32], z, w[..., 32:], z], axis=-1)


_AXIAL_PERM = np.asarray([(n % 64) // 32 * 64 + n // 64 * 32 + n % 32 for n in range(B_HEAD_DIM)])


def _dispatch(eid, rows):
    n = eid.shape[0]
    a = n * TOP_K
    i32 = jnp.int32
    iota = jnp.arange(a, dtype=i32)
    experts = jnp.arange(N_EXPERTS, dtype=i32)
    e_sorted, order = lax.sort((eid.reshape(-1), iota), num_keys=1, is_stable=True)
    starts = jnp.sum((e_sorted[None, :] < experts[:, None]).astype(i32), axis=1)
    counts = jnp.diff(starts, append=a)
    padded = ((counts + rows - 1) // rows) * rows
    pad_end = jnp.cumsum(padded)
    pad_start = pad_end - padded
    gap_step = jnp.diff(pad_start - starts, prepend=0)
    dest = iota + jnp.sum(jnp.where(e_sorted[:, None] >= experts[None, :], gap_step[None, :], 0), axis=1)
    _, pos = lax.sort((order, dest), num_keys=1)
    p_rows = ((a + rows - 1) // rows) * rows + N_EXPERTS * rows
    n_blk = p_rows // rows
    blk = jnp.arange(n_blk, dtype=i32)
    blk_e = jnp.minimum(jnp.sum((blk[:, None] * rows >= pad_end[None, :]).astype(i32), axis=1), N_EXPERTS - 1)
    onehot = blk_e[:, None] == experts[None, :]
    src0 = jnp.sum(jnp.where(onehot, (starts - pad_start)[None, :], 0), axis=1) + blk * rows
    src0 = jnp.minimum(src0, a)
    order_ext = jnp.concatenate([order, jnp.zeros((rows,), i32)])
    tok = jnp.take(order_ext, src0[:, None] + jnp.arange(rows, dtype=i32)[None, :], axis=0) // TOP_K
    n_used = (pad_end[-1] // rows).astype(i32).reshape(1)
    return tok.reshape(n_blk, 1, rows), pos.reshape(n, TOP_K), blk_e.astype(i32), n_used


def _layer(groups, w_ada, b_ada, w_in, a_q_norm, a_kv_norm, a_w_uq, a_w_ukv, a_w_o, b_q_norm, b_k_norm, b_w_o,
           w_out, ln1_g, ln1_b, w_group, b_group, w_expert, b_expert, e_w_gate, e_w_up, e_w_down, ln2_g, ln2_b):
    d = w_in.shape[0]
    seqs = [x.shape[1] for x, _ in groups]
    seg = functools.reduce(math.gcd, seqs)
    tm = _tile(seg, ROW_TILE)
    s_max = max(seqs)

    xs = [x.reshape(-1, d) for x, _ in groups]
    n = sum(x.shape[0] for x in xs)
    c_all = jnp.concatenate([c for _, c in groups], axis=0)
    nb = c_all.shape[0]
    c_pad = jnp.zeros((-(-nb // 8) * 8, d), F32).at[:nb].set(c_all)
    tile_seg, tile_pos = [], []
    b0 = 0
    for x, _ in groups:
        bsz, s = x.shape[0], x.shape[1]
        for b in range(bsz):
            for t in range(s // tm):
                tile_seg.append(b0 + b)
                tile_pos.append(t)
        b0 += bsz
    tile_seg = jnp.asarray(np.asarray(tile_seg, np.int32))
    tile_pos = jnp.asarray(np.asarray(tile_pos, np.int32))

    mod = _ada(c_pad, w_ada, b_ada)[:nb].reshape(nb, 6, d)
    mod = jnp.concatenate([mod, jnp.zeros((nb, 2, d), F32)], axis=1)

    o_q, o_kv, o_kr, o_qb, o_kb, o_vb, o_ga, o_gb = np.cumsum(
        [0, A_Q_RANK, A_KV_RANK, A_ROPE, B_HEADS * B_HEAD_DIM, B_KV_HEADS * B_HEAD_DIM,
         B_KV_HEADS * B_HEAD_DIM, d]).tolist()
    w_lat = jnp.concatenate([w_in[:, :o_kr], _pair_lanes(w_in[:, o_kr:o_qb])], axis=1).astype(BF16)
    n_qk_heads = B_HEADS + B_KV_HEADS
    w_qk = w_in[:, o_qb:o_vb].reshape(d, n_qk_heads, B_HEAD_DIM)[:, :, _AXIAL_PERM].reshape(d, -1).astype(BF16)
    w_v = w_in[:, o_vb:o_ga].astype(BF16)
    w_g = w_in[:, o_ga:].astype(BF16)
    w_uq = a_w_uq.reshape(A_Q_RANK, A_HEADS, A_NOPE + A_ROPE)
    w_uq = jnp.concatenate([w_uq[:, :, :A_NOPE], _pair_lanes(w_uq[:, :, A_NOPE:])], axis=2)
    w_uq = w_uq.reshape(A_Q_RANK, A_HEADS * A_QK_PAD).astype(BF16)
    w_ukv = a_w_ukv.reshape(A_KV_RANK, A_HEADS, A_NOPE + A_V)
    w_uk = w_ukv[:, :, :A_NOPE].reshape(A_KV_RANK, A_HEADS * A_NOPE).astype(BF16)
    w_uv = w_ukv[:, :, A_NOPE:].reshape(A_KV_RANK, A_HEADS * A_V).astype(BF16)
    w_r = jnp.zeros((d, LANES), F32).at[:, :N_GROUPS].set(w_group).at[:, N_GROUPS:N_GROUPS + N_EXPERTS].set(w_expert)
    b_r = jnp.zeros((1, LANES), F32).at[0, :N_GROUPS].set(b_group).at[0, N_GROUPS:N_GROUPS + N_EXPERTS].set(b_expert)
    w_r_hi = w_r.astype(BF16)
    w_r = jnp.concatenate([w_r_hi, (w_r - w_r_hi.astype(F32)).astype(BF16)], axis=1)

    t = jnp.arange(s_max, dtype=jnp.int32)
    tabs_a = _rope_tables(t, None)
    tabs_b = _rope_tables(t // GRID_W, t % GRID_W)

    h, lat = _mod_proj(xs, mod, tile_seg, w_lat, tm)
    gains_b = jnp.stack([b_q_norm[_AXIAL_PERM], b_k_norm[_AXIAL_PERM]]).reshape(2, 1, B_HEAD_DIM)
    qk_b = _proj_qk(h, w_qk, gains_b, tabs_b, tile_pos, tm, B_HEADS)
    v_b = _proj_vt(h, w_v, tm, COL_TILE)
    q_a = _mla_q(lat, a_q_norm.reshape(1, -1), w_uq, tabs_a, tile_pos, tm)
    k_a, v_a = _mla_kv(lat, a_kv_norm.reshape(1, -1), w_uk, w_uv, tabs_a, tile_pos, tm)

    o_a = o_b = None
    row0 = 0
    for x, _ in groups:
        bsz, s = x.shape[0], x.shape[1]
        o_a = _flash(q_a, k_a, v_a, o_a, batch=bsz, seq=s, row0=row0, n_kv_heads=A_HEADS, group=1,
                     dq=A_QK_PAD, dv=A_V, k_col0=0, scale=(A_NOPE + A_ROPE) ** -0.5,
                     tq=MLA_QUERY_TILE, tk=MLA_KEY_TILE)
        o_b = _flash(qk_b, qk_b, v_b, o_b, batch=bsz, seq=s, row0=row0, n_kv_heads=B_KV_HEADS, group=B_GROUP,
                     dq=B_HEAD_DIM, dv=B_HEAD_DIM, k_col0=B_HEADS * B_HEAD_DIM, scale=B_HEAD_DIM ** -0.5,
                     tq=GQA_QUERY_TILE, tk=GQA_KEY_TILE)
        row0 += bsz * s

    gated = _gate_mix(h, o_a, o_b, w_g, a_w_o.astype(BF16), b_w_o.astype(BF16), tm, COL_TILE)
    tm2 = _tile(seg, FULL_WEIGHT_ROW_TILE)
    seg2 = jnp.repeat(tile_seg, tm // tm2)
    x1, h2, rt = _ln1(gated, xs, mod, seg2, w_out.astype(BF16), ln1_g.reshape(1, d), ln1_b.reshape(1, d),
                      w_r, b_r, tm2)

    eid = rt[:, 2:4].astype(jnp.int32)
    tok_blocks, pos, blk_e, n_used = _dispatch(eid, EXPERT_ROWS)
    ys = _experts(h2, tok_blocks, blk_e, n_used, e_w_gate, e_w_up, e_w_down)
    pos_blocks = pos.reshape(n // tm2, tm2, TOP_K).transpose(0, 2, 1).reshape(n // tm2, 1, TOP_K * tm2)

    outs = []
    tile0 = 0
    for x, _ in groups:
        bsz, s = x.shape[0], x.shape[1]
        nt = bsz * s // tm2
        y = _final(x1, mod, seg2, pos_blocks, rt, ln2_g.reshape(1, d), ln2_b.reshape(1, d), ys, tile0, nt, tm2)
        outs.append(y.reshape(bsz, s, d))
        tile0 += nt
    return tuple(outs)


def kernel(x_prompt, x_sample, c_prompt, c_sample, w_ada, b_ada, w_in, a_q_norm, a_kv_norm, a_w_uq, a_w_ukv, a_w_o,
           b_q_norm, b_k_norm, b_w_o, w_out, ln1_g, ln1_b, w_group, b_group, w_expert, b_expert, e_w_gate, e_w_up,
           e_w_down, ln2_g, ln2_b):
    weights = (w_ada, b_ada, w_in, a_q_norm, a_kv_norm, a_w_uq, a_w_ukv, a_w_o, b_q_norm, b_k_norm, b_w_o, w_out,
               ln1_g, ln1_b, w_group, b_group, w_expert, b_expert, e_w_gate, e_w_up, e_w_down, ln2_g, ln2_b)
    return _layer(((x_prompt, c_prompt), (x_sample, c_sample)), *[w[0] for w in weights])
```

```python
import functools
import math

import numpy as np
import jax
import jax.numpy as jnp
from jax import lax
from jax.experimental import pallas as pl
from jax.experimental.pallas import tpu as pltpu

F32 = jnp.float32
BF16 = jnp.bfloat16
HIGHEST = lax.Precision.HIGHEST

GRID_W = 64
ROPE_THETA = 10000.0
A_HEADS = 16
A_Q_RANK = 512
A_KV_RANK = 512
A_NOPE = 128
A_ROPE = 64
A_V = 128
A_QK_PAD = 256
B_HEADS = 16
B_KV_HEADS = 4
B_GROUP = B_HEADS // B_KV_HEADS
B_HEAD_DIM = 128
N_GROUPS = 8
EXPERTS_PER_GROUP = 8
N_EXPERTS = N_GROUPS * EXPERTS_PER_GROUP
TOP_K = 2
RMS_EPS = 1e-6
LN_EPS = 1e-5
DEPTH = 1
DEEPNORM_ALPHA = (2 * DEPTH) ** 0.25
LOG2E = 1.4426950408889634

LANES = 128
ROW_TILE = 512
FULL_WEIGHT_ROW_TILE = 256
COL_TILE = 512
ADA_COL_TILE = 1024
MLA_QUERY_TILE, MLA_KEY_TILE = 1024, 1024
GQA_QUERY_TILE, GQA_KEY_TILE = 256, 1024
EXPERT_ROWS = 256
SUB_ROWS = 128
GATHER_UNROLL = 8
ONES_ROWS = 16
VMEM_LIMIT = 48 * 1024 * 1024


def _params(*sem):
    return pltpu.CompilerParams(dimension_semantics=sem, vmem_limit_bytes=VMEM_LIMIT)


def _tile(n, pref):
    t = min(n, pref)
    while n % t:
        t //= 2
    return t


def _sub_blocks(rows, sub):
    sub = _tile(rows, sub)
    return [slice(r, r + sub) for r in range(0, rows, sub)]


def _ada_kernel(c_ref, w_ref, b_ref, o_ref):
    c = c_ref[...]
    a = c * jax.nn.sigmoid(c)
    o_ref[...] = jnp.dot(a, w_ref[...], precision=HIGHEST, preferred_element_type=F32) + b_ref[...]


def _ada(c_pad, w_ada, b_ada):
    d, n = w_ada.shape
    tn = _tile(n, ADA_COL_TILE)
    return pl.pallas_call(
        _ada_kernel,
        grid=(n // tn,),
        in_specs=[pl.BlockSpec(c_pad.shape, lambda j: (0, 0)),
                  pl.BlockSpec((d, tn), lambda j: (0, j)),
                  pl.BlockSpec((1, tn), lambda j: (0, j))],
        out_specs=pl.BlockSpec((c_pad.shape[0], tn), lambda j: (0, j)),
        out_shape=jax.ShapeDtypeStruct((c_pad.shape[0], n), F32),
        compiler_params=_params("arbitrary"),
        name="ada",
    )(c_pad, w_ada, b_ada.reshape(1, n))


def _group_specs(xs, tm):
    specs, bounds = [], []
    t0 = 0
    for x in xs:
        nt = x.shape[0] // tm
        specs.append(pl.BlockSpec((tm, x.shape[1]), lambda i, *_, t0=t0, nt=nt: (jnp.clip(i - t0, 0, nt - 1), 0)))
        t0 += nt
        bounds.append(t0)
    return specs, tuple(bounds[:-1])


def _live_rows(i, refs, bounds, rs):
    val = refs[-1][rs, :]
    for g in reversed(range(len(refs) - 1)):
        val = jnp.where(i < bounds[g], refs[g][rs, :], val)
    return val


def _mod_proj_kernel(seg_ref, *refs, bounds):
    del seg_ref
    *x_refs, mod_ref, w_ref, h_ref, o_ref = refs
    sh = mod_ref[0, 0:1, :]
    sc = mod_ref[0, 1:2, :]
    x = _live_rows(pl.program_id(0), x_refs, bounds, slice(None))
    h = (x * (1.0 + sc) + sh).astype(BF16)
    h_ref[...] = h
    o_ref[...] = jnp.dot(h, w_ref[...], preferred_element_type=F32)


def _mod_proj(xs, mod, tile_seg, w, tm):
    n = sum(x.shape[0] for x in xs)
    d = xs[0].shape[1]
    nc = w.shape[1]
    x_specs, bounds = _group_specs(xs, tm)
    return pl.pallas_call(
        functools.partial(_mod_proj_kernel, bounds=bounds),
        grid_spec=pltpu.PrefetchScalarGridSpec(
            num_scalar_prefetch=1,
            grid=(n // tm,),
            in_specs=x_specs + [pl.BlockSpec((1, 8, d), lambda i, seg: (seg[i], 0, 0)),
                                pl.BlockSpec((d, nc), lambda i, seg: (0, 0))],
            out_specs=[pl.BlockSpec((tm, d), lambda i, seg: (i, 0)),
                       pl.BlockSpec((tm, nc), lambda i, seg: (i, 0))]),
        out_shape=[jax.ShapeDtypeStruct((n, d), BF16), jax.ShapeDtypeStruct((n, nc), F32)],
        compiler_params=_params("arbitrary"),
        name="mod_proj",
    )(tile_seg, *xs, mod, w)


def _rope(x, cos, sin):
    return x * cos + pltpu.roll(x, 64, 1) * sin


def _store_vt(vt_ref, cols, v, dv):
    vt = v.T.astype(vt_ref.dtype)
    dve = dv + ONES_ROWS
    for hh in range(v.shape[1] // dv):
        vt_ref[hh * dve:hh * dve + dv, cols] = vt[hh * dv:(hh + 1) * dv, :]
        vt_ref[hh * dve + dv:(hh + 1) * dve, cols] = jnp.ones((ONES_ROWS, vt.shape[1]), vt_ref.dtype)


def _proj_vt_kernel(h_ref, w_ref, o_ref):
    for rs in _sub_blocks(h_ref.shape[0], SUB_ROWS):
        _store_vt(o_ref, rs, jnp.dot(h_ref[rs, :], w_ref[...], preferred_element_type=F32), B_HEAD_DIM)


def _proj_qk_kernel(pos_ref, h_ref, w_ref, g_ref, cos_ref, sin_ref, o_ref, *, n_q_heads):
    del pos_ref
    for rs in _sub_blocks(h_ref.shape[0], SUB_ROWS):
        acc = jnp.dot(h_ref[rs, :], w_ref[...], preferred_element_type=F32)
        cos, sin = cos_ref[rs, :], sin_ref[rs, :]
        for hh in range(acc.shape[1] // B_HEAD_DIM):
            x = acc[:, hh * B_HEAD_DIM:(hh + 1) * B_HEAD_DIM]
            r = lax.rsqrt(jnp.mean(x * x, axis=-1, keepdims=True) + RMS_EPS)
            y = x * r * g_ref[0 if hh < n_q_heads else 1]
            o_ref[rs, hh * B_HEAD_DIM:(hh + 1) * B_HEAD_DIM] = _rope(y, cos, sin).astype(o_ref.dtype)


def _proj_vt(h, w, tm, tn):
    n, d = h.shape
    nc = w.shape[1]
    tne = tn // B_HEAD_DIM * (B_HEAD_DIM + ONES_ROWS)
    return pl.pallas_call(
        _proj_vt_kernel,
        grid=(n // tm, nc // tn),
        in_specs=[pl.BlockSpec((tm, d), lambda i, j: (i, 0)),
                  pl.BlockSpec((d, tn), lambda i, j: (0, j))],
        out_specs=pl.BlockSpec((tne, tm), lambda i, j: (j, i)),
        out_shape=jax.ShapeDtypeStruct((nc // tn * tne, n), BF16),
        compiler_params=_params("arbitrary", "arbitrary"),
        name="proj_vt",
    )(h, w)


def _proj_qk(h, w, gains, tabs, tile_pos, tm, n_q_heads):
    n, d = h.shape
    nc = w.shape[1]
    tab_spec = pl.BlockSpec((tm, LANES), lambda i, pos: (pos[i], 0))
    return pl.pallas_call(
        functools.partial(_proj_qk_kernel, n_q_heads=n_q_heads),
        grid_spec=pltpu.PrefetchScalarGridSpec(
            num_scalar_prefetch=1,
            grid=(n // tm,),
            in_specs=[pl.BlockSpec((tm, d), lambda i, pos: (i, 0)),
                      pl.BlockSpec((d, nc), lambda i, pos: (0, 0)),
                      pl.BlockSpec((2, 1, LANES), lambda i, pos: (0, 0, 0)),
                      tab_spec, tab_spec],
            out_specs=pl.BlockSpec((tm, nc), lambda i, pos: (i, 0))),
        out_shape=jax.ShapeDtypeStruct((n, nc), BF16),
        compiler_params=_params("arbitrary"),
        name="proj_qk",
    )(tile_pos, h, w, gains, *tabs)


def _rms(x, g):
    return x * lax.rsqrt(jnp.mean(x * x, axis=-1, keepdims=True) + RMS_EPS) * g


def _mla_q_kernel(pos_ref, c_ref, g_ref, w_ref, cos_ref, sin_ref, o_ref):
    del pos_ref
    for rs in _sub_blocks(c_ref.shape[0], SUB_ROWS):
        xn = _rms(c_ref[rs, :], g_ref[...]).astype(BF16)
        acc = jnp.dot(xn, w_ref[...], preferred_element_type=F32)
        cos, sin = cos_ref[rs, :], sin_ref[rs, :]
        for hh in range(acc.shape[1] // A_QK_PAD):
            base = hh * A_QK_PAD
            o_ref[rs, base:base + A_NOPE] = acc[:, base:base + A_NOPE].astype(o_ref.dtype)
            o_ref[rs, base + A_NOPE:base + A_QK_PAD] = _rope(
                acc[:, base + A_NOPE:base + A_QK_PAD], cos, sin).astype(o_ref.dtype)


def _mla_q(cqkv, gain, w, tabs, tile_pos, tm):
    n = cqkv.shape[0]
    nc = w.shape[1]
    tn = nc
    tab_spec = pl.BlockSpec((tm, LANES), lambda i, j, pos: (pos[i], 0))
    return pl.pallas_call(
        _mla_q_kernel,
        grid_spec=pltpu.PrefetchScalarGridSpec(
            num_scalar_prefetch=1,
            grid=(n // tm, nc // tn),
            in_specs=[pl.BlockSpec((tm, A_Q_RANK), lambda i, j, pos: (i, 0)),
                      pl.BlockSpec((1, A_Q_RANK), lambda i, j, pos: (0, 0)),
                      pl.BlockSpec((A_Q_RANK, tn), lambda i, j, pos: (0, j)),
                      tab_spec, tab_spec],
            out_specs=pl.BlockSpec((tm, tn), lambda i, j, pos: (i, j))),
        out_shape=jax.ShapeDtypeStruct((n, nc), BF16),
        compiler_params=_params("arbitrary", "arbitrary"),
        name="mla_q",
    )(tile_pos, cqkv, gain, w, *tabs)


def _mla_kv_kernel(pos_ref, c_ref, kr_ref, g_ref, wk_ref, wv_ref, cos_ref, sin_ref, k_ref, v_ref):
    del pos_ref
    for rs in _sub_blocks(c_ref.shape[0], SUB_ROWS):
        xn = _rms(c_ref[rs, :], g_ref[...]).astype(BF16)
        kn = jnp.dot(xn, wk_ref[...], preferred_element_type=F32)
        _store_vt(v_ref, rs, jnp.dot(xn, wv_ref[...], preferred_element_type=F32), A_V)
        kr = _rope(kr_ref[rs, :], cos_ref[rs, :], sin_ref[rs, :]).astype(k_ref.dtype)
        for hh in range(kn.shape[1] // A_NOPE):
            k_ref[rs, hh * A_QK_PAD:hh * A_QK_PAD + A_NOPE] = kn[:, hh * A_NOPE:(hh + 1) * A_NOPE].astype(k_ref.dtype)
            k_ref[rs, hh * A_QK_PAD + A_NOPE:(hh + 1) * A_QK_PAD] = kr


def _mla_kv(cqkv, gain, wk, wv, tabs, tile_pos, tm):
    n = cqkv.shape[0]
    hpt = A_HEADS
    n_ct = A_HEADS // hpt
    kr_blk = (A_Q_RANK + A_KV_RANK) // LANES
    tab_spec = pl.BlockSpec((tm, LANES), lambda i, j, pos: (pos[i], 0))
    return pl.pallas_call(
        _mla_kv_kernel,
        grid_spec=pltpu.PrefetchScalarGridSpec(
            num_scalar_prefetch=1,
            grid=(n // tm, n_ct),
            in_specs=[pl.BlockSpec((tm, A_KV_RANK), lambda i, j, pos: (i, 1)),
                      pl.BlockSpec((tm, LANES), lambda i, j, pos: (i, kr_blk)),
                      pl.BlockSpec((1, A_KV_RANK), lambda i, j, pos: (0, 0)),
                      pl.BlockSpec((A_KV_RANK, hpt * A_NOPE), lambda i, j, pos: (0, j)),
                      pl.BlockSpec((A_KV_RANK, hpt * A_V), lambda i, j, pos: (0, j)),
                      tab_spec, tab_spec],
            out_specs=[pl.BlockSpec((tm, hpt * A_QK_PAD), lambda i, j, pos: (i, j)),
                       pl.BlockSpec((hpt * (A_V + ONES_ROWS), tm), lambda i, j, pos: (j, i))]),
        out_shape=[jax.ShapeDtypeStruct((n, A_HEADS * A_QK_PAD), BF16),
                   jax.ShapeDtypeStruct((A_HEADS * (A_V + ONES_ROWS), n), BF16)],
        compiler_params=_params("arbitrary", "arbitrary"),
        name="mla_kv",
    )(tile_pos, cqkv, cqkv, gain, wk, wv, *tabs)


def _flash_kernel(q_ref, qn_ref, k_ref, vt_ref, o_ref, q_sc, qn_sc, m_sc, acc_sc, s_sc, *, group, dq, dv, tk, c):
    tq = q_ref.shape[0]
    n_kv = k_ref.shape[0] // tk
    for g in range(group):
        q_sc[g * tq:(g + 1) * tq, :] = (q_ref[:, g * dq:(g + 1) * dq].astype(F32) * c).astype(BF16)
        qn_sc[g * tq:(g + 1) * tq, :] = (qn_ref[:, g * dq:(g + 1) * dq].astype(F32) * c).astype(BF16)
    m_sc[...] = jnp.full(m_sc.shape, -1e30, F32)
    acc_sc[...] = jnp.zeros(acc_sc.shape, F32)

    def scores(j, queries):
        k = k_ref[pl.ds(pl.multiple_of(j * tk, tk), tk), :]
        return lax.dot_general(k, queries[...], (((1,), (1,)), ((), ())), preferred_element_type=F32)

    def step(j, slot, prefetch):
        s = s_sc[slot]
        if prefetch:
            s_sc[1 - slot] = scores(j + 1, q_sc)
        else:
            s_sc[1 - slot] = scores(0, qn_sc)
        vt = vt_ref[:, pl.ds(pl.multiple_of(j * tk, tk), tk)]
        m_prev = m_sc[...]
        m_new = jnp.maximum(m_prev, jnp.max(s, axis=0, keepdims=True))
        alpha = jnp.exp2(m_prev - m_new)
        p = jnp.exp2(s - m_new).astype(BF16)
        acc_sc[...] = alpha * acc_sc[...] + jnp.dot(vt, p, preferred_element_type=F32)
        m_sc[...] = m_new

    @pl.when(pl.program_id(2) == 0)
    def _():
        s_sc[0] = scores(0, q_sc)

    def pair(jj, carry):
        step(2 * jj, 0, True)
        step(2 * jj + 1, 1, True)
        return carry

    lax.fori_loop(0, n_kv // 2 - 1, pair, 0)
    step(n_kv - 2, 0, True)
    step(n_kv - 1, 1, False)
    acc = acc_sc[...]
    out = (acc[:dv] / acc[dv:dv + 1]).T
    for g in range(group):
        o_ref[:, g * dv:(g + 1) * dv] = out[g * tq:(g + 1) * tq, :].astype(o_ref.dtype)


def _flash_into_kernel(q_ref, qn_ref, k_ref, vt_ref, prev_ref, o_ref, *scratch, **kw):
    del prev_ref
    _flash_kernel(q_ref, qn_ref, k_ref, vt_ref, o_ref, *scratch, **kw)


def _flash(q2d, k2d, vt2d, out, *, batch, seq, row0, n_kv_heads, group, dq, dv, k_col0, scale, tq, tk):
    assert row0 % seq == 0 and k_col0 % dq == 0
    n = q2d.shape[0]
    tq = _tile(seq, tq)
    tk = _tile(seq // 2, tk)
    qb0, kb0 = row0 // tq, row0 // seq
    nq = seq // tq
    kc0 = k_col0 // dq
    rows = group * tq
    dve = dv + ONES_ROWS
    kw = dict(group=group, dq=dq, dv=dv, tk=tk, c=scale * LOG2E)
    in_specs = [pl.BlockSpec((tq, group * dq), lambda b, h, i: (qb0 + b * nq + i, h)),
                pl.BlockSpec((tq, group * dq), lambda b, h, i: (qb0 + b * nq + jnp.minimum(i + 1, nq - 1), h)),
                pl.BlockSpec((seq, dq), lambda b, h, i: (kb0 + b, kc0 + h)),
                pl.BlockSpec((dve, seq), lambda b, h, i: (h, kb0 + b))]
    if out is None:
        kern, operands, aliases = functools.partial(_flash_kernel, **kw), (q2d, q2d, k2d, vt2d), {}
    else:
        kern, operands, aliases = functools.partial(_flash_into_kernel, **kw), (q2d, q2d, k2d, vt2d, out), {4: 0}
        in_specs.append(pl.BlockSpec(memory_space=pl.ANY))
    return pl.pallas_call(
        kern,
        grid=(batch, n_kv_heads, nq),
        in_specs=in_specs,
        out_specs=pl.BlockSpec((tq, group * dv), lambda b, h, i: (qb0 + b * nq + i, h)),
        out_shape=jax.ShapeDtypeStruct((n, n_kv_heads * group * dv), BF16),
        scratch_shapes=[pltpu.VMEM((rows, dq), BF16), pltpu.VMEM((rows, dq), BF16), pltpu.VMEM((1, rows), F32),
                        pltpu.VMEM((dve, rows), F32), pltpu.VMEM((2, tk, rows), F32)],
        input_output_aliases=aliases,
        compiler_params=_params("arbitrary", "arbitrary", "arbitrary"),
        name="flash",
    )(*operands)


def _gate_mix_kernel(h_ref, oa_ref, ob_ref, wga_ref, wgb_ref, wa_ref, wb_ref, o_ref):
    for rs in _sub_blocks(h_ref.shape[0], 2 * SUB_ROWS):
        h = h_ref[rs, :]
        ga = jax.nn.sigmoid(jnp.dot(h, wga_ref[...], preferred_element_type=F32))
        gb = jax.nn.sigmoid(jnp.dot(h, wgb_ref[...], preferred_element_type=F32))
        ya = jnp.dot(oa_ref[rs, :], wa_ref[...], preferred_element_type=F32)
        yb = jnp.dot(ob_ref[rs, :], wb_ref[...], preferred_element_type=F32)
        o_ref[rs, :] = (ga * ya + gb * yb).astype(o_ref.dtype)


def _gate_mix(h, oa, ob, w_gates, wa, wb, tm, tn):
    n, d = oa.shape
    dm = wa.shape[1]
    nct = dm // tn
    rows = pl.BlockSpec((tm, d), lambda i, j: (i, 0))
    cols = pl.BlockSpec((d, tn), lambda i, j: (0, j))
    return pl.pallas_call(
        _gate_mix_kernel,
        grid=(n // tm, nct),
        in_specs=[rows, rows, rows, cols, pl.BlockSpec((d, tn), lambda i, j: (0, nct + j)), cols, cols],
        out_specs=pl.BlockSpec((tm, tn), lambda i, j: (i, j)),
        out_shape=jax.ShapeDtypeStruct((n, dm), BF16),
        compiler_params=_params("arbitrary", "arbitrary"),
        name="gate_mix",
    )(h, oa, ob, w_gates, w_gates, wa, wb)


def _layer_norm(y, g, b):
    mu = jnp.mean(y, axis=-1, keepdims=True)
    yc = y - mu
    var = jnp.mean(yc * yc, axis=-1, keepdims=True)
    return yc * lax.rsqrt(var + LN_EPS) * g + b


def _route(logits):
    lane = lax.broadcasted_iota(jnp.int32, logits.shape, 1).astype(F32)
    big = float(LANES)
    ninf = -jnp.inf
    gmask = lane < N_GROUPS
    gl = jnp.where(gmask, logits, ninf)
    gmax = jnp.max(gl, axis=-1, keepdims=True)
    grp = jnp.min(jnp.where(gl == gmax, lane, big), axis=-1, keepdims=True)
    den = jnp.sum(jnp.exp(gl - gmax), axis=-1, keepdims=True)
    p_grp = 1.0 / den
    lo = N_GROUPS + grp * EXPERTS_PER_GROUP
    el = jnp.where(lane >= lo, jnp.where(lane < lo + EXPERTS_PER_GROUP, logits, ninf), ninf)
    t1 = jnp.max(el, axis=-1, keepdims=True)
    i1 = jnp.min(jnp.where(el == t1, lane, big), axis=-1, keepdims=True)
    el2 = jnp.where(lane == i1, ninf, el)
    t2 = jnp.max(el2, axis=-1, keepdims=True)
    i2 = jnp.min(jnp.where(el2 == t2, lane, big), axis=-1, keepdims=True)
    e = jnp.exp(t2 - t1)
    w1 = p_grp / (1.0 + e)
    w2 = p_grp * e / (1.0 + e)
    out = jnp.where(lane == 0, w1, jnp.where(lane == 1, w2, jnp.where(lane == 2, i1 - N_GROUPS, i2 - N_GROUPS)))
    return jnp.where(lane < 4, out, 0.0)


def _ln1_kernel(seg_ref, g_ref, *refs, bounds):
    del seg_ref
    *x_refs, mod_ref, w_ref, lg_ref, lb_ref, wr_ref, br_ref, x1_ref, h2_ref, rt_ref = refs
    i = pl.program_id(0)
    g1 = mod_ref[0, 2:3, :]
    sh2 = mod_ref[0, 3:4, :]
    sc2 = mod_ref[0, 4:5, :]
    for rs in _sub_blocks(g_ref.shape[0], SUB_ROWS):
        attn = jnp.dot(g_ref[rs, :], w_ref[...], preferred_element_type=F32)
        x = _live_rows(i, x_refs, bounds, rs)
        x1 = _layer_norm(DEEPNORM_ALPHA * x + g1 * attn, lg_ref[...], lb_ref[...])
        x1_ref[rs, :] = x1
        h2 = x1 * (1.0 + sc2) + sh2
        h2_ref[rs, :] = h2
        h_hi = h2.astype(BF16)
        h_lo = (h2 - h_hi.astype(F32)).astype(BF16)
        a = jnp.dot(h_hi, wr_ref[...], preferred_element_type=F32)
        b = jnp.dot(h_lo, wr_ref[:, :LANES], preferred_element_type=F32)
        rt_ref[rs, :] = _route(a[:, :LANES] + a[:, LANES:] + b + br_ref[...])


def _ln1(gated, xs, mod, tile_seg, w_out, lg, lb, wr, br, tm):
    n, d = gated.shape
    row = lambda i, seg: (i, 0)
    const = lambda i, seg: (0, 0)
    x_specs, bounds = _group_specs(xs, tm)
    return pl.pallas_call(
        functools.partial(_ln1_kernel, bounds=bounds),
        grid_spec=pltpu.PrefetchScalarGridSpec(
            num_scalar_prefetch=1,
            grid=(n // tm,),
            in_specs=[pl.BlockSpec((tm, d), row)] + x_specs + [
                      pl.BlockSpec((1, 8, d), lambda i, seg: (seg[i], 0, 0)),
                      pl.BlockSpec((d, d), const, pipeline_mode=pl.Buffered(1)),
                      pl.BlockSpec((1, d), const),
                      pl.BlockSpec((1, d), const),
                      pl.BlockSpec((d, 2 * LANES), const, pipeline_mode=pl.Buffered(1)),
                      pl.BlockSpec((1, LANES), const)],
            out_specs=[pl.BlockSpec((tm, d), row), pl.BlockSpec((tm, d), row), pl.BlockSpec((tm, LANES), row)]),
        out_shape=[jax.ShapeDtypeStruct((n, d), F32), jax.ShapeDtypeStruct((n, d), F32),
                   jax.ShapeDtypeStruct((n, LANES), F32)],
        compiler_params=_params("arbitrary"),
        name="ln1_route",
    )(tile_seg, gated, *xs, mod, w_out, lg, lb, wr, br)


def _row_gather_start(src_hbm, dst, sem, idx, row):
    pltpu.make_async_copy(src_hbm.at[pl.ds(idx, 1), :], dst.at[pl.ds(row, 1), :], sem).start()


def _row_gather_wait(src_hbm, dst, sem, row):
    pltpu.make_async_copy(src_hbm.at[pl.ds(0, 1), :], dst.at[pl.ds(row, 1), :], sem).wait()


def _gather_rows_start(src_hbm, dst, sem, idx_ref, idx0, rows):
    for r in range(rows):
        _row_gather_start(src_hbm, dst, sem, idx_ref[0, 0, idx0 + r], r)


def _gather_rows_wait(src_hbm, dst, sem, rows):
    def body(r, carry):
        _row_gather_wait(src_hbm, dst, sem, r)
        return carry

    lax.fori_loop(0, rows, body, 0, unroll=GATHER_UNROLL)


def _expert_kernel(blk_e_ref, n_used_ref, tok_ref, tok_next_ref, h2_hbm, wg_ref, wu_ref, wd_ref, o_ref,
                   xbuf, wg_bf, wu_bf, wd_bf, sem):
    i = pl.program_id(0)
    n_used = n_used_ref[0]
    e = blk_e_ref[i]
    prev = blk_e_ref[jnp.maximum(i - 1, 0)]
    rows = xbuf.shape[1]
    slot = i % 2

    @pl.when(i == 0)
    def _():
        _gather_rows_start(h2_hbm, xbuf.at[0], sem.at[0], tok_ref, 0, rows)

    @pl.when(i + 1 < n_used)
    def _():
        _gather_rows_start(h2_hbm, xbuf.at[1 - slot], sem.at[1 - slot], tok_next_ref, 0, rows)

    @pl.when(jnp.logical_or(i == 0, e != prev))
    def _():
        wg_bf[...] = wg_ref[0].astype(BF16)
        wu_bf[...] = wu_ref[0].astype(BF16)
        wd_bf[...] = wd_ref[0].astype(BF16)

    @pl.when(i < n_used)
    def _():
        _gather_rows_wait(h2_hbm, xbuf.at[slot], sem.at[slot], rows)
        for rs in _sub_blocks(rows, SUB_ROWS):
            x = xbuf[slot, rs, :].astype(BF16)
            gate = jnp.dot(x, wg_bf[...], preferred_element_type=F32)
            up = jnp.dot(x, wu_bf[...], preferred_element_type=F32)
            act = (gate * jax.nn.sigmoid(gate) * up).astype(BF16)
            o_ref[rs, :] = jnp.dot(act, wd_bf[...], preferred_element_type=F32)

    @pl.when(i >= n_used)
    def _():
        o_ref[...] = jnp.zeros_like(o_ref)


def _experts(h2, tok_blocks, blk_e, n_used, w_gate, w_up, w_down):
    n_blk = tok_blocks.shape[0]
    rows = tok_blocks.shape[2]
    d = h2.shape[1]
    de = w_gate.shape[2]
    return pl.pallas_call(
        _expert_kernel,
        grid_spec=pltpu.PrefetchScalarGridSpec(
            num_scalar_prefetch=2,
            grid=(n_blk,),
            in_specs=[pl.BlockSpec((1, 1, rows), lambda i, be, nu: (i, 0, 0), memory_space=pltpu.SMEM),
                      pl.BlockSpec((1, 1, rows), lambda i, be, nu: (jnp.minimum(i + 1, n_blk - 1), 0, 0),
                                   memory_space=pltpu.SMEM),
                      pl.BlockSpec(memory_space=pl.ANY),
                      pl.BlockSpec((1, d, de), lambda i, be, nu: (be[i], 0, 0)),
                      pl.BlockSpec((1, d, de), lambda i, be, nu: (be[i], 0, 0)),
                      pl.BlockSpec((1, de, d), lambda i, be, nu: (be[i], 0, 0))],
            out_specs=pl.BlockSpec((rows, d), lambda i, be, nu: (i, 0)),
            scratch_shapes=[pltpu.VMEM((2, rows, d), F32),
                            pltpu.VMEM((d, de), BF16),
                            pltpu.VMEM((d, de), BF16),
                            pltpu.VMEM((de, d), BF16),
                            pltpu.SemaphoreType.DMA((2,))]),
        out_shape=jax.ShapeDtypeStruct((n_blk * rows, d), F32),
        compiler_params=_params("arbitrary"),
        name="experts",
    )(blk_e, n_used, tok_blocks, tok_blocks, h2, w_gate, w_up, w_down)


def _final_kernel(seg_ref, pos_ref, pos_next_ref, x1_ref, mod_ref, rt_ref, lg_ref, lb_ref, ys_hbm, o_ref, buf, sem):
    del seg_ref
    i = pl.program_id(0)
    tm = x1_ref.shape[0]
    slot = i % 2

    def start_tile(idx_ref, s):
        for k in range(TOP_K):
            _gather_rows_start(ys_hbm, buf.at[s, k], sem.at[s], idx_ref, k * tm, tm)

    @pl.when(i == 0)
    def _():
        start_tile(pos_ref, 0)

    @pl.when(i + 1 < pl.num_programs(0))
    def _():
        start_tile(pos_next_ref, 1 - slot)

    for k in range(TOP_K):
        _gather_rows_wait(ys_hbm, buf.at[slot, k], sem.at[slot], tm)
    g2 = mod_ref[0, 5:6, :]
    for rs in _sub_blocks(tm, SUB_ROWS):
        rt = rt_ref[rs, :]
        ffn = rt[:, 0:1] * buf[slot, 0, rs, :] + rt[:, 1:2] * buf[slot, 1, rs, :]
        o_ref[rs, :] = _layer_norm(DEEPNORM_ALPHA * x1_ref[rs, :] + g2 * ffn, lg_ref[...], lb_ref[...])


def _final(x1, mod, tile_seg, pos_blocks, rt, lg, lb, ys, tile0, n_tiles, tm):
    d = x1.shape[1]
    row = lambda i, seg: (tile0 + i, 0)
    const = lambda i, seg: (0, 0)
    return pl.pallas_call(
        _final_kernel,
        grid_spec=pltpu.PrefetchScalarGridSpec(
            num_scalar_prefetch=1,
            grid=(n_tiles,),
            in_specs=[pl.BlockSpec((1, 1, TOP_K * tm), lambda i, seg: (tile0 + i, 0, 0), memory_space=pltpu.SMEM),
                      pl.BlockSpec((1, 1, TOP_K * tm),
                                   lambda i, seg: (tile0 + jnp.minimum(i + 1, n_tiles - 1), 0, 0),
                                   memory_space=pltpu.SMEM),
                      pl.BlockSpec((tm, d), row),
                      pl.BlockSpec((1, 8, d), lambda i, seg: (seg[tile0 + i], 0, 0)),
                      pl.BlockSpec((tm, LANES), row),
                      pl.BlockSpec((1, d), const),
                      pl.BlockSpec((1, d), const),
                      pl.BlockSpec(memory_space=pl.ANY)],
            out_specs=pl.BlockSpec((tm, d), lambda i, seg: (i, 0)),
            scratch_shapes=[pltpu.VMEM((2, TOP_K, tm, d), F32), pltpu.SemaphoreType.DMA((2,))]),
        out_shape=jax.ShapeDtypeStruct((n_tiles * tm, d), F32),
        compiler_params=_params("arbitrary"),
        name="combine_ln2",
    )(tile_seg, pos_blocks, pos_blocks, x1, mod, rt, lg, lb, ys)


def _rope_tables(pos_a, pos_b):
    inv = ROPE_THETA ** (-jnp.arange(0, 64, 2, dtype=F32) / 64)

    def part(pos):
        if pos is None:
            z = jnp.zeros((pos_a.shape[0], 32), F32)
            return z, z
        ang = pos.astype(F32)[:, None] * inv[None, :]
        return jnp.cos(ang), jnp.sin(ang)

    (ca, sa), (cb, sb) = part(pos_a), part(pos_b)
    return jnp.concatenate([ca, cb, ca, cb], 1), jnp.concatenate([-sa, -sb, sa, sb], 1)


def _pair_lanes(w):
    z = jnp.zeros(w.shape[:-1] + (32,), w.dtype)
    return jnp.concatenate([w[..., :32], z, w[..., 32:], z], axis=-1)


_AXIAL_PERM = np.asarray([(n % 64) // 32 * 64 + n // 64 * 32 + n % 32 for n in range(B_HEAD_DIM)])


def _dispatch(eid, rows):
    n = eid.shape[0]
    a = n * TOP_K
    i32 = jnp.int32
    iota = jnp.arange(a, dtype=i32)
    experts = jnp.arange(N_EXPERTS, dtype=i32)
    e_sorted, order = lax.sort((eid.reshape(-1), iota), num_keys=1, is_stable=True)
    starts = jnp.sum((e_sorted[None, :] < experts[:, None]).astype(i32), axis=1)
    counts = jnp.diff(starts, append=a)
    padded = ((counts + rows - 1) // rows) * rows
    pad_end = jnp.cumsum(padded)
    pad_start = pad_end - padded
    gap_step = jnp.diff(pad_start - starts, prepend=0)
    dest = iota + jnp.sum(jnp.where(e_sorted[:, None] >= experts[None, :], gap_step[None, :], 0), axis=1)
    _, pos = lax.sort((order, dest), num_keys=1)
    p_rows = ((a + rows - 1) // rows) * rows + N_EXPERTS * rows
    n_blk = p_rows // rows
    blk = jnp.arange(n_blk, dtype=i32)
    blk_e = jnp.minimum(jnp.sum((blk[:, None] * rows >= pad_end[None, :]).astype(i32), axis=1), N_EXPERTS - 1)
    onehot = blk_e[:, None] == experts[None, :]
    src0 = jnp.sum(jnp.where(onehot, (starts - pad_start)[None, :], 0), axis=1) + blk * rows
    src0 = jnp.minimum(src0, a)
    order_ext = jnp.concatenate([order, jnp.zeros((rows,), i32)])
    tok = jnp.take(order_ext, src0[:, None] + jnp.arange(rows, dtype=i32)[None, :], axis=0) // TOP_K
    n_used = (pad_end[-1] // rows).astype(i32).reshape(1)
    return tok.reshape(n_blk, 1, rows), pos.reshape(n, TOP_K), blk_e.astype(i32), n_used


def _layer(groups, w_ada, b_ada, w_in, a_q_norm, a_kv_norm, a_w_uq, a_w_ukv, a_w_o, b_q_norm, b_k_norm, b_w_o,
           w_out, ln1_g, ln1_b, w_group, b_group, w_expert, b_expert, e_w_gate, e_w_up, e_w_down, ln2_g, ln2_b):
    d = w_in.shape[0]
    seqs = [x.shape[1] for x, _ in groups]
    seg = functools.reduce(math.gcd, seqs)
    tm = _tile(seg, ROW_TILE)
    s_max = max(seqs)

    xs = [x.reshape(-1, d) for x, _ in groups]
    n = sum(x.shape[0] for x in xs)
    c_all = jnp.concatenate([c for _, c in groups], axis=0)
    nb = c_all.shape[0]
    c_pad = jnp.zeros((-(-nb // 8) * 8, d), F32).at[:nb].set(c_all)
    tile_seg, tile_pos = [], []
    b0 = 0
    for x, _ in groups:
        bsz, s = x.shape[0], x.shape[1]
        for b in range(bsz):
            for t in range(s // tm):
                tile_seg.append(b0 + b)
                tile_pos.append(t)
        b0 += bsz
    tile_seg = jnp.asarray(np.asarray(tile_seg, np.int32))
    tile_pos = jnp.asarray(np.asarray(tile_pos, np.int32))

    mod = _ada(c_pad, w_ada, b_ada)[:nb].reshape(nb, 6, d)
    mod = jnp.concatenate([mod, jnp.zeros((nb, 2, d), F32)], axis=1)

    o_q, o_kv, o_kr, o_qb, o_kb, o_vb, o_ga, o_gb = np.cumsum(
        [0, A_Q_RANK, A_KV_RANK, A_ROPE, B_HEADS * B_HEAD_DIM, B_KV_HEADS * B_HEAD_DIM,
         B_KV_HEADS * B_HEAD_DIM, d]).tolist()
    w_lat = jnp.concatenate([w_in[:, :o_kr], _pair_lanes(w_in[:, o_kr:o_qb])], axis=1).astype(BF16)
    n_qk_heads = B_HEADS + B_KV_HEADS
    w_qk = w_in[:, o_qb:o_vb].reshape(d, n_qk_heads, B_HEAD_DIM)[:, :, _AXIAL_PERM].reshape(d, -1).astype(BF16)
    w_v = w_in[:, o_vb:o_ga].astype(BF16)
    w_g = w_in[:, o_ga:].astype(BF16)
    w_uq = a_w_uq.reshape(A_Q_RANK, A_HEADS, A_NOPE + A_ROPE)
    w_uq = jnp.concatenate([w_uq[:, :, :A_NOPE], _pair_lanes(w_uq[:, :, A_NOPE:])], axis=2)
    w_uq = w_uq.reshape(A_Q_RANK, A_HEADS * A_QK_PAD).astype(BF16)
    w_ukv = a_w_ukv.reshape(A_KV_RANK, A_HEADS, A_NOPE + A_V)
    w_uk = w_ukv[:, :, :A_NOPE].reshape(A_KV_RANK, A_HEADS * A_NOPE).astype(BF16)
    w_uv = w_ukv[:, :, A_NOPE:].reshape(A_KV_RANK, A_HEADS * A_V).astype(BF16)
    w_r = jnp.zeros((d, LANES), F32).at[:, :N_GROUPS].set(w_group).at[:, N_GROUPS:N_GROUPS + N_EXPERTS].set(w_expert)
    b_r = jnp.zeros((1, LANES), F32).at[0, :N_GROUPS].set(b_group).at[0, N_GROUPS:N_GROUPS + N_EXPERTS].set(b_expert)
    w_r_hi = w_r.astype(BF16)
    w_r = jnp.concatenate([w_r_hi, (w_r - w_r_hi.astype(F32)).astype(BF16)], axis=1)

    t = jnp.arange(s_max, dtype=jnp.int32)
    tabs_a = _rope_tables(t, None)
    tabs_b = _rope_tables(t // GRID_W, t % GRID_W)

    h, lat = _mod_proj(xs, mod, tile_seg, w_lat, tm)
    gains_b = jnp.stack([b_q_norm[_AXIAL_PERM], b_k_norm[_AXIAL_PERM]]).reshape(2, 1, B_HEAD_DIM)
    qk_b = _proj_qk(h, w_qk, gains_b, tabs_b, tile_pos, tm, B_HEADS)
    v_b = _proj_vt(h, w_v, tm, COL_TILE)
    q_a = _mla_q(lat, a_q_norm.reshape(1, -1), w_uq, tabs_a, tile_pos, tm)
    k_a, v_a = _mla_kv(lat, a_kv_norm.reshape(1, -1), w_uk, w_uv, tabs_a, tile_pos, tm)

    o_a = o_b = None
    row0 = 0
    for x, _ in groups:
        bsz, s = x.shape[0], x.shape[1]
        o_a = _flash(q_a, k_a, v_a, o_a, batch=bsz, seq=s, row0=row0, n_kv_heads=A_HEADS, group=1,
                     dq=A_QK_PAD, dv=A_V, k_col0=0, scale=(A_NOPE + A_ROPE) ** -0.5,
                     tq=MLA_QUERY_TILE, tk=MLA_KEY_TILE)
        o_b = _flash(qk_b, qk_b, v_b, o_b, batch=bsz, seq=s, row0=row0, n_kv_heads=B_KV_HEADS, group=B_GROUP,
                     dq=B_HEAD_DIM, dv=B_HEAD_DIM, k_col0=B_HEADS * B_HEAD_DIM, scale=B_HEAD_DIM ** -0.5,
                     tq=GQA_QUERY_TILE, tk=GQA_KEY_TILE)
        row0 += bsz * s

    gated = _gate_mix(h, o_a, o_b, w_g, a_w_o.astype(BF16), b_w_o.astype(BF16), tm, COL_TILE)
    tm2 = _tile(seg, FULL_WEIGHT_ROW_TILE)
    seg2 = jnp.repeat(tile_seg, tm // tm2)
    x1, h2, rt = _ln1(gated, xs, mod, tile_seg, w_out.astype(BF16), ln1_g.reshape(1, d), ln1_b.reshape(1, d),
                      w_r, b_r, tm)

    eid = rt[:, 2:4].astype(jnp.int32)
    tok_blocks, pos, blk_e, n_used = _dispatch(eid, EXPERT_ROWS)
    ys = _experts(h2, tok_blocks, blk_e, n_used, e_w_gate, e_w_up, e_w_down)
    pos_blocks = pos.reshape(n // tm2, tm2, TOP_K).transpose(0, 2, 1).reshape(n // tm2, 1, TOP_K * tm2)

    outs = []
    tile0 = 0
    for x, _ in groups:
        bsz, s = x.shape[0], x.shape[1]
        nt = bsz * s // tm2
        y = _final(x1, mod, seg2, pos_blocks, rt, ln2_g.reshape(1, d), ln2_b.reshape(1, d), ys, tile0, nt, tm2)
        outs.append(y.reshape(bsz, s, d))
        tile0 += nt
    return tuple(outs)


def kernel(x_prompt, x_sample, c_prompt, c_sample, w_ada, b_ada, w_in, a_q_norm, a_kv_norm, a_w_uq, a_w_ukv, a_w_o,
           b_q_norm, b_k_norm, b_w_o, w_out, ln1_g, ln1_b, w_group, b_group, w_expert, b_expert, e_w_gate, e_w_up,
           e_w_down, ln2_g, ln2_b):
    weights = (w_ada, b_ada, w_in, a_q_norm, a_kv_norm, a_w_uq, a_w_ukv, a_w_o, b_q_norm, b_k_norm, b_w_o, w_out,
               ln1_g, ln1_b, w_group, b_group, w_expert, b_expert, e_w_gate, e_w_up, e_w_down, ln2_g, ln2_b)
    return _layer(((x_prompt, c_prompt), (x_sample, c_sample)), *[w[0] for w in weights])
```
